```python
import jax
import jax.numpy as jnp
from jax import lax
import numpy as np

D_MODEL = 4096
BATCH = 1
SEQ = 16384
DEPTH = 2

HEAD_DIM = 128
ROPE_THETA = 10000.0
NORM_EPS = 1e-6
Q_BLOCK = 128
NEG = -1e30
FORCED = 1e9

NSA_HEADS = D_MODEL // (2 * HEAD_DIM)
NSA_KV_HEADS = NSA_HEADS // 4
NSA_REP = NSA_HEADS // NSA_KV_HEADS
CMP_LEN = 32
CMP_STRIDE = 16
SLC_LEN = 64
SLC_TOPK = 16
WINDOW = 512
PHI_HIDDEN = 512

SGU_WIDTH = D_MODEL // 4
SGU_GROUPS = 8
SGU_GROUP_DIM = SGU_WIDTH // SGU_GROUPS
SGU_CHUNK = 128

MOBA_HEADS = D_MODEL // (4 * HEAD_DIM)
MOBA_BLOCK = 256
MOBA_TOPK = 3

D_FF = 4 * D_MODEL

NSA_W = NSA_HEADS * HEAD_DIM
NSA_KV_W = NSA_KV_HEADS * HEAD_DIM
MOBA_W = MOBA_HEADS * HEAD_DIM
IN_SIZES = (NSA_W, NSA_KV_W, NSA_KV_W, NSA_KV_W, NSA_KV_W, NSA_KV_W, NSA_KV_W, 3 * NSA_HEADS,
            SGU_WIDTH, SGU_WIDTH, MOBA_W, MOBA_W, MOBA_W, D_MODEL, D_MODEL, D_MODEL)
IN_OFFSETS = tuple(int(o) for o in np.cumsum(IN_SIZES)[:-1])
IN_WIDTH = sum(IN_SIZES)

kernel_name = 'hybrid_nsa_sgu_moba_block'


def rms_norm(x, g):
    xf = x.astype(jnp.float32)
    y = xf * lax.rsqrt(jnp.mean(xf * xf, axis=-1, keepdims=True) + NORM_EPS)
    return (y * g.astype(jnp.float32)).astype(x.dtype)


def rope_tables(positions):
    inv = ROPE_THETA ** (-jnp.arange(0, HEAD_DIM, 2, dtype=jnp.float32) / HEAD_DIM)
    ang = positions.astype(jnp.float32)[..., None] * inv
    return jnp.cos(ang), jnp.sin(ang)


def apply_rope(x, cos, sin):
    xf = x.astype(jnp.float32)
    x1, x2 = jnp.split(xf, 2, axis=-1)
    c = cos[:, :, None, :]
    s = sin[:, :, None, :]
    return jnp.concatenate([x1 * c - x2 * s, x2 * c + x1 * s], axis=-1).astype(x.dtype)


def masked_softmax(s, mask):
    s = jnp.where(mask, s, NEG)
    e = jnp.where(mask, jnp.exp(s - jnp.max(s, axis=-1, keepdims=True)), 0.0)
    return e / jnp.maximum(jnp.sum(e, axis=-1, keepdims=True), 1e-30)


def nsa_compress(k_raw, pe, w1, w2):
    b, s = k_raw.shape[:2]
    n_cmp = (s - CMP_LEN) // CMP_STRIDE + 1
    idx = jnp.arange(n_cmp)[:, None] * CMP_STRIDE + jnp.arange(CMP_LEN)[None, :]
    blk = k_raw[:, idx] + pe[None, None, :, None, :]
    blk = jnp.moveaxis(blk, 3, 2).reshape(b, n_cmp, NSA_KV_HEADS, CMP_LEN * HEAD_DIM)
    return jax.nn.gelu(blk @ w1) @ w2


def nsa_attend(q, q_rot, k_cmp, v_cmp, k_slc, v_slc, k_win, v_win, gates):
    s_len = q.shape[0]
    n_cmp = k_cmp.shape[0]
    n_slc = s_len // SLC_LEN
    k_sel = min(SLC_TOPK, n_slc)
    scale = HEAD_DIM ** -0.5
    cmp_start = jnp.arange(n_cmp) * CMP_STRIDE
    cmp_end = cmp_start + CMP_LEN - 1
    blk_id = jnp.arange(n_slc)
    overlap = ((cmp_start[:, None] <= blk_id[None, :] * SLC_LEN + SLC_LEN - 1)
               & (cmp_end[:, None] >= blk_id[None, :] * SLC_LEN)).astype(jnp.float32)
    ks_blocks = k_slc.reshape(n_slc, SLC_LEN, NSA_KV_HEADS, HEAD_DIM).transpose(2, 0, 1, 3)
    vs_blocks = v_slc.reshape(n_slc, SLC_LEN, NSA_KV_HEADS, HEAD_DIM).transpose(2, 0, 1, 3)
    kw_pad = jnp.pad(k_win, ((WINDOW, 0), (0, 0), (0, 0)))
    vw_pad = jnp.pad(v_win, ((WINDOW, 0), (0, 0), (0, 0)))
    g_idx = jnp.arange(NSA_KV_HEADS)[:, None, None]

    def block(i):
        q0 = i * Q_BLOCK
        t = q0 + jnp.arange(Q_BLOCK)
        qc = lax.dynamic_slice_in_dim(q, q0, Q_BLOCK, 0).reshape(Q_BLOCK, NSA_KV_HEADS, NSA_REP, HEAD_DIM)
        qr = lax.dynamic_slice_in_dim(q_rot, q0, Q_BLOCK, 0).reshape(Q_BLOCK, NSA_KV_HEADS, NSA_REP, HEAD_DIM)
        g = lax.dynamic_slice_in_dim(gates, q0, Q_BLOCK, 0).reshape(Q_BLOCK, 3, NSA_KV_HEADS, NSA_REP)
        s_c = jnp.einsum('tgrd,ngd->grtn', qc, k_cmp, preferred_element_type=jnp.float32) * scale
        p_c = masked_softmax(s_c, (cmp_end[None, :] <= t[:, None])[None, None])
        o_c = jnp.einsum('grtn,ngd->tgrd', p_c.astype(v_cmp.dtype), v_cmp)
        imp = jnp.einsum('grtn,nj->gtj', p_c, overlap)
        cur = t // SLC_LEN
        j = blk_id[None, :]
        forced = (j == 0) | (j == cur[:, None]) | (j == cur[:, None] - 1)
        allowed = j <= cur[:, None]
        score = jnp.where(allowed, jnp.where(forced, FORCED, imp), NEG)
        _, idx = lax.top_k(score, k_sel)
        ks_sel = ks_blocks[g_idx, idx].reshape(NSA_KV_HEADS, Q_BLOCK, k_sel * SLC_LEN, HEAD_DIM)
        vs_sel = vs_blocks[g_idx, idx].reshape(NSA_KV_HEADS, Q_BLOCK, k_sel * SLC_LEN, HEAD_DIM)
        pos = idx[..., None] * SLC_LEN + jnp.arange(SLC_LEN)
        m_s = (idx[..., None] <= cur[None, :, None, None]) & (pos <= t[None, :, None, None])
        m_s = m_s.reshape(NSA_KV_HEADS, Q_BLOCK, k_sel * SLC_LEN)[:, None]
        s_s = jnp.einsum('tgrd,gtnd->grtn', qr, ks_sel, preferred_element_type=jnp.float32) * scale
        p_s = masked_softmax(s_s, m_s)
        o_s = jnp.einsum('grtn,gtnd->tgrd', p_s.astype(vs_sel.dtype), vs_sel)
        kw = lax.dynamic_slice_in_dim(kw_pad, q0, Q_BLOCK + WINDOW, 0)
        vw = lax.dynamic_slice_in_dim(vw_pad, q0, Q_BLOCK + WINDOW, 0)
        pos_w = q0 - WINDOW + jnp.arange(Q_BLOCK + WINDOW)
        m_w = ((pos_w[None, :] <= t[:, None]) & (pos_w[None, :] > t[:, None] - WINDOW)
               & (pos_w[None, :] >= 0))
        s_w = jnp.einsum('tgrd,ngd->grtn', qr, kw, preferred_element_type=jnp.float32) * scale
        p_w = masked_softmax(s_w, m_w[None, None])
        o_w = jnp.einsum('grtn,ngd->tgrd', p_w.astype(vw.dtype), vw)
        o = g[:, 0, :, :, None] * o_c + g[:, 1, :, :, None] * o_s + g[:, 2, :, :, None] * o_w
        return o.reshape(Q_BLOCK, NSA_W)

    out = lax.map(block, jnp.arange(s_len // Q_BLOCK))
    return out.reshape(s_len, NSA_W)


def moba_attend(q, k, v):
    s_len = q.shape[0]
    n_blk = -(-s_len // MOBA_BLOCK)
    k_top = min(MOBA_TOPK, n_blk)
    pad = n_blk * MOBA_BLOCK - s_len
    scale = HEAD_DIM ** -0.5
    k_pad = jnp.pad(k, ((0, pad), (0, 0), (0, 0)))
    v_pad = jnp.pad(v, ((0, pad), (0, 0), (0, 0)))
    kb = k_pad.reshape(n_blk, MOBA_BLOCK, MOBA_HEADS, HEAD_DIM).transpose(2, 0, 1, 3)
    vb = v_pad.reshape(n_blk, MOBA_BLOCK, MOBA_HEADS, HEAD_DIM).transpose(2, 0, 1, 3)
    k_mean = jnp.mean(kb.astype(jnp.float32), axis=2).astype(k.dtype)
    h_idx = jnp.arange(MOBA_HEADS)[:, None, None]
    blk_id = jnp.arange(n_blk)
    n_sel = k_top * MOBA_BLOCK

    def block(i):
        q0 = i * Q_BLOCK
        t = q0 + jnp.arange(Q_BLOCK)
        cur = q0 // MOBA_BLOCK
        qb = lax.dynamic_slice_in_dim(q, q0, Q_BLOCK, 0)
        s_g = jnp.einsum('thd,hnd->htn', qb, k_mean, preferred_element_type=jnp.float32)
        _, idx = lax.top_k(jnp.where(blk_id < cur, s_g, NEG), k_top)
        k_sel = kb[h_idx, idx].reshape(MOBA_HEADS, Q_BLOCK, n_sel, HEAD_DIM)
        v_sel = vb[h_idx, idx].reshape(MOBA_HEADS, Q_BLOCK, n_sel, HEAD_DIM)
        m_sel = jnp.repeat(idx < cur, MOBA_BLOCK, axis=-1)
        start = cur * MOBA_BLOCK
        k_own = lax.dynamic_slice_in_dim(k_pad, start, MOBA_BLOCK, 0)
        v_own = lax.dynamic_slice_in_dim(v_pad, start, MOBA_BLOCK, 0)
        m_own = (start + jnp.arange(MOBA_BLOCK))[None, :] <= t[:, None]
        s_sel = jnp.einsum('thd,htnd->htn', qb, k_sel, preferred_element_type=jnp.float32)
        s_own = jnp.einsum('thd,nhd->htn', qb, k_own, preferred_element_type=jnp.float32)
        s = jnp.concatenate([s_sel, s_own], axis=-1) * scale
        m = jnp.concatenate([m_sel, jnp.broadcast_to(m_own[None], (MOBA_HEADS, Q_BLOCK, MOBA_BLOCK))], axis=-1)
        p = masked_softmax(s, m).astype(v.dtype)
        o = (jnp.einsum('htn,htnd->thd', p[..., :n_sel], v_sel)
             + jnp.einsum('htn,nhd->thd', p[..., n_sel:], v_own))
        return o.reshape(Q_BLOCK, MOBA_W)

    out = lax.map(block, jnp.arange(s_len // Q_BLOCK))
    return out.reshape(s_len, MOBA_W)


def sgu_mix(u, v, gain, w_s, b_s):
    b, s = v.shape[:2]
    nc = s // SGU_CHUNK
    v = rms_norm(v.reshape(b, s, SGU_GROUPS, SGU_GROUP_DIM), gain.reshape(SGU_GROUPS, SGU_GROUP_DIM))
    w = jnp.where(jnp.tril(jnp.ones((SGU_CHUNK, SGU_CHUNK), dtype=bool)), w_s, 0.0)
    v = v.reshape(b, nc, SGU_CHUNK, SGU_GROUPS, SGU_GROUP_DIM)
    mixed = jnp.einsum('gts,bcsgd->bctgd', w, v) + b_s.T[None, None, :, :, None]
    return u * mixed.reshape(b, s, SGU_WIDTH)


def token_mixers(h, cos, sin, w_in, nsa_gate_b, nsa_q_norm, nsa_kc_norm, nsa_ks_norm, nsa_kw_norm,
                 phi_pe_k, phi_w1_k, phi_w2_k, phi_pe_v, phi_w1_v, phi_w2_v,
                 sgu_norm, sgu_w, sgu_b, moba_q_norm, moba_k_norm, proj_a, proj_b, proj_c, w_out):
    b, s, _ = h.shape
    z = h @ w_in
    (qa, kc, vc, ks, vs, kw, vw, ga, ub, vb, qc, kcm, vcm, gm_a, gm_b, gm_c) = jnp.split(z, IN_OFFSETS, axis=-1)
    qa = rms_norm(qa.reshape(b, s, NSA_HEADS, HEAD_DIM), nsa_q_norm)
    qa_rot = apply_rope(qa, cos, sin)
    kc = rms_norm(nsa_compress(kc.reshape(b, s, NSA_KV_HEADS, HEAD_DIM), phi_pe_k, phi_w1_k, phi_w2_k), nsa_kc_norm)
    vc = nsa_compress(vc.reshape(b, s, NSA_KV_HEADS, HEAD_DIM), phi_pe_v, phi_w1_v, phi_w2_v)
    ks = apply_rope(rms_norm(ks.reshape(b, s, NSA_KV_HEADS, HEAD_DIM), nsa_ks_norm), cos, sin)
    vs = vs.reshape(b, s, NSA_KV_HEADS, HEAD_DIM)
    kw = apply_rope(rms_norm(kw.reshape(b, s, NSA_KV_HEADS, HEAD_DIM), nsa_kw_norm), cos, sin)
    vw = vw.reshape(b, s, NSA_KV_HEADS, HEAD_DIM)
    ga = jax.nn.sigmoid(ga + nsa_gate_b).reshape(b, s, 3, NSA_HEADS)
    o_a = jax.vmap(nsa_attend)(qa, qa_rot, kc, vc, ks, vs, kw, vw, ga)
    o_b = sgu_mix(jax.nn.gelu(ub), jax.nn.gelu(vb), sgu_norm, sgu_w, sgu_b)
    qc = apply_rope(rms_norm(qc.reshape(b, s, MOBA_HEADS, HEAD_DIM), moba_q_norm), cos, sin)
    kcm = apply_rope(rms_norm(kcm.reshape(b, s, MOBA_HEADS, HEAD_DIM), moba_k_norm), cos, sin)
    vcm = vcm.reshape(b, s, MOBA_HEADS, HEAD_DIM)
    o_c = jax.vmap(moba_attend)(qc, kcm, vcm)
    y = (jax.nn.sigmoid(gm_a) * (o_a @ proj_a)
         + jax.nn.sigmoid(gm_b) * (o_b @ proj_b)
         + jax.nn.sigmoid(gm_c) * (o_c @ proj_c))
    return y @ w_out


def sq_relu_mlp(h, w1, w2):
    return jnp.square(jax.nn.relu(h @ w1)) @ w2


def setup_inputs(seed: int = 0) -> dict:
    key = jax.random.key(seed)
    k = jax.random.split(key, 27)
    L = DEPTH

    def nrm(kk, shape, scale):
        return jax.random.normal(kk, shape, jnp.float32) * scale

    def gain(kk, shape):
        return 1.0 + 0.02 * jax.random.normal(kk, shape, jnp.float32)

    return {
        'x': nrm(k[0], (BATCH, SEQ, D_MODEL), 1.0),
        'positions': jnp.broadcast_to(jnp.arange(SEQ, dtype=jnp.int32)[None, :], (BATCH, SEQ)),
        'norm_mix': gain(k[1], (L, D_MODEL)),
        'norm_mlp': gain(k[2], (L, D_MODEL)),
        'w_in': nrm(k[3], (L, D_MODEL, IN_WIDTH), D_MODEL ** -0.5),
        'nsa_gate_b': nrm(k[4], (L, 3 * NSA_HEADS), 0.1),
        'nsa_q_norm': gain(k[5], (L, HEAD_DIM)),
        'nsa_kc_norm': gain(k[6], (L, HEAD_DIM)),
        'nsa_ks_norm': gain(k[7], (L, HEAD_DIM)),
        'nsa_kw_norm': gain(k[8], (L, HEAD_DIM)),
        'phi_pe_k': nrm(k[9], (L, CMP_LEN, HEAD_DIM), 0.1),
        'phi_w1_k': nrm(k[10], (L, CMP_LEN * HEAD_DIM, PHI_HIDDEN), (CMP_LEN * HEAD_DIM) ** -0.5),
        'phi_w2_k': nrm(k[11], (L, PHI_HIDDEN, HEAD_DIM), PHI_HIDDEN ** -0.5),
        'phi_pe_v': nrm(k[12], (L, CMP_LEN, HEAD_DIM), 0.1),
        'phi_w1_v': nrm(k[13], (L, CMP_LEN * HEAD_DIM, PHI_HIDDEN), (CMP_LEN * HEAD_DIM) ** -0.5),
        'phi_w2_v': nrm(k[14], (L, PHI_HIDDEN, HEAD_DIM), PHI_HIDDEN ** -0.5),
        'sgu_norm': gain(k[15], (L, SGU_WIDTH)),
        'sgu_w': nrm(k[16], (L, SGU_GROUPS, SGU_CHUNK, SGU_CHUNK), 0.5 * SGU_CHUNK ** -0.5),
        'sgu_b': 1.0 + nrm(k[17], (L, SGU_GROUPS, SGU_CHUNK), 0.02),
        'moba_q_norm': gain(k[18], (L, HEAD_DIM)),
        'moba_k_norm': gain(k[19], (L, HEAD_DIM)),
        'proj_a': nrm(k[20], (L, NSA_W, D_MODEL), NSA_W ** -0.5),
        'proj_b': nrm(k[21], (L, SGU_WIDTH, D_MODEL), SGU_WIDTH ** -0.5),
        'proj_c': nrm(k[22], (L, MOBA_W, D_MODEL), MOBA_W ** -0.5),
        'w_out': nrm(k[23], (L, D_MODEL, D_MODEL), D_MODEL ** -0.5),
        'mlp_w1': nrm(k[24], (L, D_MODEL, D_FF), D_MODEL ** -0.5),
        'mlp_w2': nrm(k[25], (L, D_FF, D_MODEL), D_FF ** -0.5),
    }


def reference(x, positions, norm_mix, norm_mlp, w_in, nsa_gate_b, nsa_q_norm, nsa_kc_norm, nsa_ks_norm,
              nsa_kw_norm, phi_pe_k, phi_w1_k, phi_w2_k, phi_pe_v, phi_w1_v, phi_w2_v, sgu_norm, sgu_w, sgu_b,
              moba_q_norm, moba_k_norm, proj_a, proj_b, proj_c, w_out, mlp_w1, mlp_w2):
    cos, sin = rope_tables(positions)
    for l in range(DEPTH):
        h = rms_norm(x, norm_mix[l])
        x = x + token_mixers(h, cos, sin, w_in[l], nsa_gate_b[l], nsa_q_norm[l], nsa_kc_norm[l],
                             nsa_ks_norm[l], nsa_kw_norm[l], phi_pe_k[l], phi_w1_k[l], phi_w2_k[l],
                             phi_pe_v[l], phi_w1_v[l], phi_w2_v[l], sgu_norm[l], sgu_w[l], sgu_b[l],
                             moba_q_norm[l], moba_k_norm[l], proj_a[l], proj_b[l], proj_c[l], w_out[l])
        x = x + sq_relu_mlp(rms_norm(x, norm_mlp[l]), mlp_w1[l], mlp_w2[l])
    return x
```

```python
import functools
import math

import jax
import jax.numpy as jnp
import numpy as np
from jax import lax
from jax.experimental import pallas as pl
from jax.experimental.pallas import tpu as pltpu

F32 = jnp.float32
BF16 = jnp.bfloat16

HEAD_DIM = 128
LANES = 128
ROPE_THETA = 10000.0
NORM_EPS = 1e-6
NEG = -1e30
FORCED = 1e9
BELOW_NEG = -3e38

NSA_REP = 4
CMP_LEN = 32
CMP_STRIDE = 16
SLC_LEN = 64
SLC_TOPK = 16
WINDOW = 512
SGU_GROUPS = 8
SGU_CHUNK = 128
MOBA_BLOCK = 256
MOBA_TOPK = 3

MIB = 1024 * 1024
VMEM_LIMIT = 52 * MIB


def _cparams(sem, vmem=VMEM_LIMIT):
    return pltpu.CompilerParams(dimension_semantics=sem, vmem_limit_bytes=vmem)


def _tile(n, pref):
    if n <= pref:
        return n
    t = (pref // LANES) * LANES
    while t >= LANES:
        if n % t == 0:
            return t
        t -= LANES
    raise ValueError(f"no 128-multiple tile divides {n}")


def _gelu(x):
    c = math.sqrt(2.0 / math.pi)
    return 0.5 * x * (1.0 + jnp.tanh(c * (x + 0.044715 * (x * x * x))))


def _sigmoid(x):
    return 1.0 / (1.0 + jnp.exp(-x))


def _head_norm(x, gain):
    return x * lax.rsqrt(jnp.mean(x * x, axis=-1, keepdims=True) + NORM_EPS) * gain


def _rope(x, cos, sin_signed):
    return x * cos + pltpu.roll(x, HEAD_DIM // 2, 1) * sin_signed


def _rmsnorm_body(x_ref, g_ref, o_ref):
    x = x_ref[...]
    y = x * lax.rsqrt(jnp.mean(x * x, axis=-1, keepdims=True) + NORM_EPS)
    o_ref[...] = (y * g_ref[...]).astype(o_ref.dtype)


def _rmsnorm(x, gain, tm=256):
    m, d = x.shape
    tm = min(tm, m)
    return pl.pallas_call(
        _rmsnorm_body,
        grid=(m // tm,),
        in_specs=[pl.BlockSpec((tm, d), lambda i: (i, 0)),
                  pl.BlockSpec((1, d), lambda i: (0, 0))],
        out_specs=pl.BlockSpec((tm, d), lambda i: (i, 0)),
        out_shape=jax.ShapeDtypeStruct((m, d), BF16),
        compiler_params=_cparams(("parallel",)),
        name="rmsnorm",
    )(x, gain.reshape(1, d))


def _mm_body(*refs, n_extra, n_out, nk, epilogue):
    a_ref, b_ref = refs[0], refs[1]
    extra = refs[2:2 + n_extra]
    outs = refs[2 + n_extra:2 + n_extra + n_out]
    if nk == 1:
        epilogue(jnp.dot(a_ref[...], b_ref[...], preferred_element_type=F32), extra, outs)
        return
    acc_ref = refs[-1]
    k = pl.program_id(2)

    @pl.when(k == 0)
    def _():
        acc_ref[...] = jnp.dot(a_ref[...], b_ref[...], preferred_element_type=F32)

    @pl.when(k > 0)
    def _():
        acc_ref[...] += jnp.dot(a_ref[...], b_ref[...], preferred_element_type=F32)

    @pl.when(k == nk - 1)
    def _():
        epilogue(acc_ref[...], extra, outs)


def _matmul(a, b, epilogue, out_shapes, out_specs, extras=(), extra_specs=(),
            tm=1024, tn=1024, tk=2048, name="matmul"):
    m, kdim = a.shape
    n = b.shape[1]
    tm, tn, tk = min(tm, m), _tile(n, tn), _tile(kdim, tk)
    nk = kdim // tk
    body = functools.partial(_mm_body, n_extra=len(extras), n_out=len(out_shapes), nk=nk,
                             epilogue=functools.partial(epilogue, tm=tm, tn=tn))
    scratch = [] if nk == 1 else [pltpu.VMEM((tm, tn), F32)]
    return pl.pallas_call(
        body,
        grid=(m // tm, n // tn, nk),
        in_specs=[pl.BlockSpec((tm, tk), lambda i, j, k: (i, k)),
                  pl.BlockSpec((tk, tn), lambda i, j, k: (k, j))]
                 + [s(tm, tn) for s in extra_specs],
        out_specs=[s(tm, tn) for s in out_specs],
        out_shape=out_shapes,
        scratch_shapes=scratch,
        compiler_params=_cparams(("parallel", "parallel", "arbitrary")),
        name=name,
    )(a, b, *extras)


def _spec_tile(tm, tn):
    return pl.BlockSpec((tm, tn), lambda i, j, k: (i, j))


def _spec_col(tm, tn):
    return pl.BlockSpec((1, tn), lambda i, j, k: (0, j))


def _spec_rope(tm, tn):
    return pl.BlockSpec((tm, HEAD_DIM), lambda i, j, k: (i, 0))


def _spec_blockmean(tm, tn):
    return pl.BlockSpec((1, tm // MOBA_BLOCK, tn), lambda i, j, k: (i, 0, j))


def _ep_cast(acc, extra, outs, *, tm, tn):
    outs[0][...] = acc.astype(outs[0].dtype)


def _ep_sigmoid_bias(acc, extra, outs, *, tm, tn):
    outs[0][...] = _sigmoid(acc + extra[0][...])


def _ep_sigmoid(acc, extra, outs, *, tm, tn):
    outs[0][...] = _sigmoid(acc)


def _ep_relu2(acc, extra, outs, *, tm, tn):
    r = jnp.maximum(acc, 0.0)
    outs[0][...] = (r * r).astype(outs[0].dtype)


def _ep_residual(acc, extra, outs, *, tm, tn):
    outs[0][...] = extra[0][...] + acc


def _ep_q(acc, extra, outs, *, tm, tn, scale):
    gain, cos, sin = extra[0][...], extra[1][...], extra[2][...]
    for hd in range(tn // HEAD_DIM):
        sl = slice(hd * HEAD_DIM, (hd + 1) * HEAD_DIM)
        y = _head_norm(acc[:, sl], gain[:, sl])
        outs[0][:, sl] = (y * scale).astype(BF16)
        outs[1][:, sl] = (_rope(y, cos, sin) * scale).astype(BF16)


def _ep_qrot(acc, extra, outs, *, tm, tn, scale):
    gain, cos, sin = extra[0][...], extra[1][...], extra[2][...]
    for hd in range(tn // HEAD_DIM):
        sl = slice(hd * HEAD_DIM, (hd + 1) * HEAD_DIM)
        y = _head_norm(acc[:, sl], gain[:, sl])
        outs[0][:, sl] = (_rope(y, cos, sin) * scale).astype(BF16)


def _ep_krot(acc, extra, outs, *, tm, tn, block_mean):
    gain, cos, sin = extra[0][...], extra[1][...], extra[2][...]
    for hd in range(tn // HEAD_DIM):
        sl = slice(hd * HEAD_DIM, (hd + 1) * HEAD_DIM)
        y = _rope(_head_norm(acc[:, sl], gain[:, sl]), cos, sin)
        outs[0][:, sl] = y.astype(BF16)
        if block_mean:
            for blk in range(tm // MOBA_BLOCK):
                rows = y[blk * MOBA_BLOCK:(blk + 1) * MOBA_BLOCK]
                outs[1][0, blk:blk + 1, sl] = jnp.mean(rows, axis=0, keepdims=True)


def _compress_body(a_ref, pe_ref, w1_ref, w2_ref, g_ref, o_ref, *, norm):
    half = CMP_STRIDE * HEAD_DIM
    a = a_ref[0]
    n_chunk = a.shape[0]
    x1 = (a + pe_ref[:, :half]).astype(BF16)
    x2 = (a + pe_ref[:, half:]).astype(BF16)
    p1 = jnp.dot(x1, w1_ref[:half, :], preferred_element_type=F32)
    p2 = jnp.dot(x2, w1_ref[half:, :], preferred_element_type=F32)
    h = _gelu(p1 + pltpu.roll(p2, n_chunk - 1, 0))
    o = jnp.dot(h.astype(BF16), w2_ref[...], preferred_element_type=F32)
    if norm:
        o = _head_norm(o, g_ref[...])
    o_ref[0] = o.astype(o_ref.dtype)


def _compress(a, pe, w1, w2, gain, norm):
    g, n_chunk, width = a.shape
    hidden = w1.shape[1]
    return pl.pallas_call(
        functools.partial(_compress_body, norm=norm),
        grid=(g,),
        in_specs=[pl.BlockSpec((1, n_chunk, width), lambda i: (i, 0, 0)),
                  pl.BlockSpec((1, 2 * width), lambda i: (0, 0)),
                  pl.BlockSpec((2 * width, hidden), lambda i: (0, 0)),
                  pl.BlockSpec((hidden, HEAD_DIM), lambda i: (0, 0)),
                  pl.BlockSpec((1, HEAD_DIM), lambda i: (0, 0))],
        out_specs=pl.BlockSpec((1, n_chunk, HEAD_DIM), lambda i: (i, 0, 0)),
        out_shape=jax.ShapeDtypeStruct((g, n_chunk, HEAD_DIM), BF16),
        compiler_params=_cparams(("parallel",)),
        name="nsa_compress",
    )(a, pe.reshape(1, 2 * width), w1.astype(BF16), w2.astype(BF16), gain.reshape(1, HEAD_DIM))


def _topk_mask(score, lane_f, k):
    n = score.shape[-1]
    sel = jnp.zeros(score.shape, dtype=jnp.bool_)
    for _ in range(k):
        m = jnp.max(score, axis=-1, keepdims=True)
        first = jnp.min(jnp.where(score == m, lane_f, float(n)), axis=-1, keepdims=True)
        hit = lane_f == first
        sel = jnp.logical_or(sel, hit)
        score = jnp.where(hit, BELOW_NEG, score)
    return sel


def _nsa_cmp_body(q_ref, kt_ref, v_ref, ov_ref, o_ref, sb_ref, *, tq, n_slc):
    qi = pl.program_id(1)
    q0 = qi * tq
    q = jnp.concatenate([q_ref[:, r * HEAD_DIM:(r + 1) * HEAD_DIM] for r in range(NSA_REP)], axis=0)
    s = jnp.dot(q, kt_ref[0], preferred_element_type=F32)
    n_pad = s.shape[-1]
    t1 = q0 + lax.broadcasted_iota(jnp.int32, (tq, 1), 0)
    t = jnp.concatenate([t1] * NSA_REP, axis=0)
    cmp_end = lax.broadcasted_iota(jnp.int32, (1, n_pad), 1) * CMP_STRIDE + (CMP_LEN - 1)
    mask = cmp_end <= t
    s = jnp.where(mask, s, NEG)
    e = jnp.where(mask, jnp.exp(s - jnp.max(s, axis=-1, keepdims=True)), 0.0)
    p = e / jnp.maximum(jnp.sum(e, axis=-1, keepdims=True), 1e-30)
    o = jnp.dot(p.astype(BF16), v_ref[0], preferred_element_type=F32)
    for r in range(NSA_REP):
        o_ref[:, r * HEAD_DIM:(r + 1) * HEAD_DIM] = o[r * tq:(r + 1) * tq]
    ps = p[0:tq]
    for r in range(1, NSA_REP):
        ps = ps + p[r * tq:(r + 1) * tq]
    ps_hi = ps.astype(BF16)
    ps_lo = (ps - ps_hi.astype(F32)).astype(BF16)
    imp = (jnp.dot(ps_hi, ov_ref[...], preferred_element_type=F32)
           + jnp.dot(ps_lo, ov_ref[...], preferred_element_type=F32))
    n_slc_pad = imp.shape[-1]
    j = lax.broadcasted_iota(jnp.int32, (tq, n_slc_pad), 1)
    cur = t1 // SLC_LEN
    forced = (j == 0) | (j == cur) | (j == cur - 1)
    allowed = j <= cur
    score = jnp.where(allowed, jnp.where(forced, FORCED, imp), NEG)
    score = jnp.where(j < n_slc, score, BELOW_NEG)
    sel = _topk_mask(score, j.astype(F32), min(SLC_TOPK, n_slc))
    sb_ref[0] = jnp.where(sel & allowed, 0.0, NEG).astype(BF16)


def _nsa_cmp(q_c, kc_t, vc, overlap, n_slc, tq=128):
    s_len = q_c.shape[0]
    g, _, n_pad = kc_t.shape
    n_slc_pad = overlap.shape[1]
    gw = NSA_REP * HEAD_DIM
    return pl.pallas_call(
        functools.partial(_nsa_cmp_body, tq=tq, n_slc=n_slc),
        grid=(g, s_len // tq),
        in_specs=[pl.BlockSpec((tq, gw), lambda gi, qi: (qi, gi)),
                  pl.BlockSpec((1, HEAD_DIM, n_pad), lambda gi, qi: (gi, 0, 0)),
                  pl.BlockSpec((1, n_pad, HEAD_DIM), lambda gi, qi: (gi, 0, 0)),
                  pl.BlockSpec((n_pad, n_slc_pad), lambda gi, qi: (0, 0))],
        out_specs=[pl.BlockSpec((tq, gw), lambda gi, qi: (qi, gi)),
                   pl.BlockSpec((1, tq, n_slc_pad), lambda gi, qi: (gi, qi, 0))],
        out_shape=[jax.ShapeDtypeStruct((s_len, g * gw), F32),
                   jax.ShapeDtypeStruct((g, s_len, n_slc_pad), BF16)],
        compiler_params=_cparams(("parallel", "parallel")),
        name="nsa_cmp_select",
    )(q_c, kc_t, vc, overlap)


def _flash_body(*refs, rep, tq, tk, keys_per_var, n_var, moba):
    if moba:
        q_ref, km_ref, kt_ref, v_ref, o_ref, qa_s, m_s, l_s, acc_s = refs
    else:
        q_ref, sb_ref, kt_ref, v_ref, o_ref, qa_s, m_s, l_s, acc_s = refs
    rows = rep * tq
    qi = pl.program_id(1)
    q0 = qi * tq
    t1 = q0 + lax.broadcasted_iota(jnp.int32, (tq, 1), 0)
    q = jnp.concatenate([q_ref[:, r * HEAD_DIM:(r + 1) * HEAD_DIM] for r in range(rep)], axis=0)
    if moba:
        sg = jnp.dot(q, km_ref[0], preferred_element_type=F32)
        n_blk = (kt_ref.shape[1] * tk) // MOBA_BLOCK
        j = lax.broadcasted_iota(jnp.int32, sg.shape, 1)
        cur = t1 // MOBA_BLOCK
        past = j < cur
        score = jnp.where(j < n_blk, jnp.where(past, sg, NEG), BELOW_NEG)
        sel = _topk_mask(score, j.astype(F32), min(MOBA_TOPK, n_blk))
        bias = jnp.where((sel & past) | (j == cur), 0.0, NEG).astype(BF16)
        qa_s[0] = jnp.concatenate([q, bias], axis=1)
    else:
        for var in range(n_var):
            sb = sb_ref[0][:, var * LANES:(var + 1) * LANES]
            qa_s[var] = jnp.concatenate([q, jnp.concatenate([sb] * rep, axis=0)], axis=1)
    m_s[...] = jnp.full(m_s.shape, NEG, F32)
    l_s[...] = jnp.zeros(l_s.shape, F32)
    acc_s[...] = jnp.zeros(acc_s.shape, F32)
    t = jnp.concatenate([t1] * rep, axis=0)

    def step(kt, causal):
        var = (kt * tk) // keys_per_var if n_var > 1 else 0
        s = jnp.dot(qa_s[var], kt_ref[0, kt], preferred_element_type=F32)
        if causal:
            pos = kt * tk + lax.broadcasted_iota(jnp.int32, (1, tk), 1)
            s = jnp.where(pos <= t, s, NEG)
        m_old = m_s[...]
        m_new = jnp.maximum(m_old, jnp.max(s, axis=-1, keepdims=True))
        alpha = jnp.exp(m_old - m_new)
        p = jnp.exp(s - m_new)
        l_s[...] = alpha * l_s[...] + jnp.sum(p, axis=-1, keepdims=True)
        start = pl.multiple_of(kt * tk, tk)
        acc_s[...] = alpha * acc_s[...] + jnp.dot(p.astype(BF16), v_ref[pl.ds(start, tk), :],
                                                  preferred_element_type=F32)
        m_s[...] = m_new

    n_full = q0 // tk

    def full_step(kt, carry):
        step(kt, False)
        return carry

    lax.fori_loop(0, n_full, full_step, 0)
    for d in range(max(1, tq // tk)):
        step(n_full + d, True)
    o = acc_s[...] / l_s[...]
    for r in range(rep):
        o_ref[:, r * HEAD_DIM:(r + 1) * HEAD_DIM] = o[r * tq:(r + 1) * tq].astype(o_ref.dtype)


def _flash(q, aux, kt_aug, v, *, rep, tq, keys_per_var, moba, out_dtype, name):
    s_len = q.shape[0]
    g, n_kt, kdim, tk = kt_aug.shape
    gw = rep * HEAD_DIM
    rows = rep * tq
    n_var = 1 if moba else aux.shape[-1] // LANES
    if moba:
        aux_spec = pl.BlockSpec((1, HEAD_DIM, LANES), lambda gi, qi: (gi, 0, 0))
    else:
        aux_spec = pl.BlockSpec((1, tq, n_var * LANES), lambda gi, qi: (gi, qi, 0))
    return pl.pallas_call(
        functools.partial(_flash_body, rep=rep, tq=tq, tk=tk, keys_per_var=keys_per_var,
                          n_var=n_var, moba=moba),
        grid=(g, s_len // tq),
        in_specs=[pl.BlockSpec((tq, gw), lambda gi, qi: (qi, gi)),
                  aux_spec,
                  pl.BlockSpec((1, n_kt, kdim, tk), lambda gi, qi: (gi, 0, 0, 0)),
                  pl.BlockSpec((s_len, HEAD_DIM), lambda gi, qi: (0, gi))],
        out_specs=pl.BlockSpec((tq, gw), lambda gi, qi: (qi, gi)),
        out_shape=jax.ShapeDtypeStruct((s_len, g * gw), out_dtype),
        scratch_shapes=[pltpu.VMEM((n_var, rows, 2 * HEAD_DIM), BF16),
                        pltpu.VMEM((rows, 1), F32),
                        pltpu.VMEM((rows, 1), F32),
                        pltpu.VMEM((rows, HEAD_DIM), F32)],
        compiler_params=_cparams(("parallel", "arbitrary")),
        name=name,
    )(q, aux, kt_aug, v)


def _window_body(q_ref, kt_ref, v_ref, o_ref, *, tq):
    qi = pl.program_id(1)
    q0 = qi * tq
    n_tiles = (tq + WINDOW) // LANES
    q = jnp.concatenate([q_ref[:, r * HEAD_DIM:(r + 1) * HEAD_DIM] for r in range(NSA_REP)], axis=0)
    t1 = q0 + lax.broadcasted_iota(jnp.int32, (tq, 1), 0)
    t = jnp.concatenate([t1] * NSA_REP, axis=0)
    first = q0 // LANES - WINDOW // LANES
    lane = lax.broadcasted_iota(jnp.int32, (1, LANES), 1)
    scores, tiles = [], []
    for i in range(n_tiles):
        raw = first + i
        idx = jnp.maximum(raw, 0)
        tiles.append(idx)
        pos = raw * LANES + lane
        ok = (pos <= t) & (pos > t - WINDOW) & (pos >= 0)
        s = jnp.dot(q, kt_ref[0, idx], preferred_element_type=F32)
        scores.append(jnp.where(ok, s, NEG))
    s = jnp.concatenate(scores, axis=1)
    e = jnp.exp(s - jnp.max(s, axis=-1, keepdims=True))
    p = (e / jnp.sum(e, axis=-1, keepdims=True)).astype(BF16)
    o = jnp.zeros((NSA_REP * tq, HEAD_DIM), F32)
    for i in range(n_tiles):
        start = pl.multiple_of(tiles[i] * LANES, LANES)
        o = o + jnp.dot(p[:, i * LANES:(i + 1) * LANES], v_ref[pl.ds(start, LANES), :],
                        preferred_element_type=F32)
    for r in range(NSA_REP):
        o_ref[:, r * HEAD_DIM:(r + 1) * HEAD_DIM] = o[r * tq:(r + 1) * tq]


def _window(q_r, kw_t, v, v_block0, tq=128):
    s_len = q_r.shape[0]
    g, n_kt, _, _ = kw_t.shape
    gw = NSA_REP * HEAD_DIM
    return pl.pallas_call(
        functools.partial(_window_body, tq=tq),
        grid=(g, s_len // tq),
        in_specs=[pl.BlockSpec((tq, gw), lambda gi, qi: (qi, gi)),
                  pl.BlockSpec((1, n_kt, HEAD_DIM, LANES), lambda gi, qi: (gi, 0, 0, 0)),
                  pl.BlockSpec((s_len, HEAD_DIM), lambda gi, qi: (0, v_block0 + gi))],
        out_specs=pl.BlockSpec((tq, gw), lambda gi, qi: (qi, gi)),
        out_shape=jax.ShapeDtypeStruct((s_len, g * gw), F32),
        compiler_params=_cparams(("parallel", "parallel")),
        name="nsa_window",
    )(q_r, kw_t, v)


def _combine_body(oc_ref, os_ref, ow_ref, g_ref, o_ref, *, n_heads):
    g = g_ref[...]
    for hd in range(n_heads):
        sl = slice(hd * HEAD_DIM, (hd + 1) * HEAD_DIM)
        o = (g[:, hd:hd + 1] * oc_ref[:, sl]
             + g[:, n_heads + hd:n_heads + hd + 1] * os_ref[:, sl]
             + g[:, 2 * n_heads + hd:2 * n_heads + hd + 1] * ow_ref[:, sl])
        o_ref[:, sl] = o.astype(o_ref.dtype)


def _combine(o_c, o_s, o_w, gates, tq=512):
    s_len, width = o_c.shape
    tq = min(tq, s_len)
    spec = pl.BlockSpec((tq, width), lambda i: (i, 0))
    return pl.pallas_call(
        functools.partial(_combine_body, n_heads=width // HEAD_DIM),
        grid=(s_len // tq,),
        in_specs=[spec, spec, spec, pl.BlockSpec((tq, gates.shape[1]), lambda i: (i, 0))],
        out_specs=spec,
        out_shape=jax.ShapeDtypeStruct((s_len, width), BF16),
        compiler_params=_cparams(("parallel",)),
        name="nsa_combine",
    )(o_c, o_s, o_w, gates)


def _sgu_body(zu_ref, zv_ref, gain_ref, w_ref, bt_ref, o_ref, *, tm):
    gd = zu_ref.shape[1] // SGU_GROUPS
    row = lax.broadcasted_iota(jnp.int32, (SGU_CHUNK, SGU_CHUNK), 0)
    col = lax.broadcasted_iota(jnp.int32, (SGU_CHUNK, SGU_CHUNK), 1)
    bt = bt_ref[...]
    for g in range(SGU_GROUPS):
        sl = slice(g * gd, (g + 1) * gd)
        v = _gelu(zv_ref[:, sl])
        v = (v * lax.rsqrt(jnp.mean(v * v, axis=-1, keepdims=True) + NORM_EPS) * gain_ref[:, sl]).astype(BF16)
        w = jnp.where(col <= row, w_ref[g], 0.0).astype(BF16)
        for c in range(tm // SGU_CHUNK):
            rs = slice(c * SGU_CHUNK, (c + 1) * SGU_CHUNK)
            mixed = jnp.dot(w, v[rs], preferred_element_type=F32) + bt[:, g:g + 1]
            o_ref[rs, sl] = (_gelu(zu_ref[rs, sl]) * mixed).astype(o_ref.dtype)


def _sgu(z_uv, gain, w_s, b_s, tm=512):
    s_len = z_uv.shape[0]
    width = z_uv.shape[1] // 2
    tm = min(tm, s_len)
    return pl.pallas_call(
        functools.partial(_sgu_body, tm=tm),
        grid=(s_len // tm,),
        in_specs=[pl.BlockSpec((tm, width), lambda i: (i, 0)),
                  pl.BlockSpec((tm, width), lambda i: (i, 1)),
                  pl.BlockSpec((1, width), lambda i: (0, 0)),
                  pl.BlockSpec((SGU_GROUPS, SGU_CHUNK, SGU_CHUNK), lambda i: (0, 0, 0)),
                  pl.BlockSpec((SGU_CHUNK, SGU_GROUPS), lambda i: (0, 0))],
        out_specs=pl.BlockSpec((tm, width), lambda i: (i, 0)),
        out_shape=jax.ShapeDtypeStruct((s_len, width), BF16),
        compiler_params=_cparams(("parallel",)),
        name="sgu",
    )(z_uv, z_uv, gain.reshape(1, width), w_s, b_s.T)


def _merge_body(oa_ref, ob_ref, oc_ref, pa_ref, pb_ref, pc_ref, ga_ref, gb_ref, gc_ref, y_ref):
    y = ga_ref[...] * jnp.dot(oa_ref[...], pa_ref[...], preferred_element_type=F32)
    y = y + gb_ref[...] * jnp.dot(ob_ref[...], pb_ref[...], preferred_element_type=F32)
    y = y + gc_ref[...] * jnp.dot(oc_ref[...], pc_ref[...], preferred_element_type=F32)
    y_ref[...] = y.astype(y_ref.dtype)


def _merge(o_a, o_b, o_c, p_a, p_b, p_c, gm, tm=512, tn=1024):
    s_len = o_a.shape[0]
    d = p_a.shape[1]
    tm, tn = min(tm, s_len), _tile(d, tn)
    nj = d // tn

    def rows(w):
        return pl.BlockSpec((tm, w), lambda i, j: (i, 0))

    def cols(kdim):
        return pl.BlockSpec((kdim, tn), lambda i, j: (0, j))

    def gate(off):
        return pl.BlockSpec((tm, tn), lambda i, j: (i, off * nj + j))

    return pl.pallas_call(
        _merge_body,
        grid=(s_len // tm, nj),
        in_specs=[rows(o_a.shape[1]), rows(o_b.shape[1]), rows(o_c.shape[1]),
                  cols(p_a.shape[0]), cols(p_b.shape[0]), cols(p_c.shape[0]),
                  gate(0), gate(1), gate(2)],
        out_specs=pl.BlockSpec((tm, tn), lambda i, j: (i, j)),
        out_shape=jax.ShapeDtypeStruct((s_len, d), BF16),
        compiler_params=_cparams(("parallel", "parallel")),
        name="gated_merge",
    )(o_a, o_b, o_c, p_a, p_b, p_c, gm, gm, gm)


def _keys_t(k, n_heads, tk, onehot_t=None):
    s_len = k.shape[0]
    kt = k.reshape(s_len // tk, tk, n_heads, HEAD_DIM).transpose(2, 0, 3, 1)
    if onehot_t is not None:
        kt = jnp.concatenate([kt, jnp.broadcast_to(onehot_t[None], (n_heads,) + onehot_t.shape)], axis=2)
    return kt


def _onehot_t(s_len, block, tk):
    key = np.arange(s_len)
    oh = ((key // block) % LANES)[None, :] == np.arange(LANES)[:, None]
    return jnp.asarray(oh.reshape(LANES, s_len // tk, tk).transpose(1, 0, 2), dtype=BF16)


def _overlap(n_pad, n_slc_pad):
    i = np.arange(n_pad)[:, None]
    j = np.arange(n_slc_pad)[None, :]
    ov = (i * CMP_STRIDE <= j * SLC_LEN + SLC_LEN - 1) & (i * CMP_STRIDE + CMP_LEN - 1 >= j * SLC_LEN)
    return jnp.asarray(ov, dtype=BF16)


def _layer(x, cos, sin, p):
    s_len, d_model = x.shape
    scale = HEAD_DIM ** -0.5
    w_in = p["w_in"]
    nsa_w = p["proj_a"].shape[0]
    sgu_w = p["proj_b"].shape[0]
    moba_w = p["proj_c"].shape[0]
    n_heads = nsa_w // HEAD_DIM
    n_groups = n_heads // NSA_REP
    kv_w = n_groups * HEAD_DIM
    moba_heads = moba_w // HEAD_DIM
    sizes = (nsa_w, kv_w, kv_w, kv_w, kv_w, kv_w, kv_w, 3 * n_heads, sgu_w, sgu_w,
             moba_w, moba_w, moba_w, d_model, d_model, d_model)
    offs = np.concatenate([[0], np.cumsum(sizes)])

    def seg(a, b):
        return w_in[:, offs[a]:offs[b]].astype(BF16)

    def tile_gain(gain, reps):
        return jnp.tile(gain, reps).reshape(1, reps * HEAD_DIM)

    h = _rmsnorm(x, p["norm_mix"])
    rope_extras = (cos, sin)
    rope_specs = (_spec_rope, _spec_rope)

    q_c, q_r = _matmul(
        h, seg(0, 1), functools.partial(_ep_q, scale=scale),
        [jax.ShapeDtypeStruct((s_len, nsa_w), BF16)] * 2, [_spec_tile, _spec_tile],
        extras=(tile_gain(p["nsa_q_norm"], n_heads),) + rope_extras,
        extra_specs=(_spec_col,) + rope_specs, name="proj_nsa_q")
    (kcvc,) = _matmul(h, seg(1, 3), _ep_cast, [jax.ShapeDtypeStruct((s_len, 2 * kv_w), F32)],
                      [_spec_tile], name="proj_nsa_cmp_kv")
    (kskw,) = _matmul(
        h, jnp.concatenate([seg(3, 4), seg(5, 6)], axis=1), functools.partial(_ep_krot, block_mean=False),
        [jax.ShapeDtypeStruct((s_len, 2 * kv_w), BF16)], [_spec_tile],
        extras=(jnp.concatenate([tile_gain(p["nsa_ks_norm"], n_groups),
                                 tile_gain(p["nsa_kw_norm"], n_groups)], axis=1),) + rope_extras,
        extra_specs=(_spec_col,) + rope_specs, name="proj_nsa_k")
    (vsvw,) = _matmul(h, jnp.concatenate([seg(4, 5), seg(6, 7)], axis=1), _ep_cast,
                      [jax.ShapeDtypeStruct((s_len, 2 * kv_w), BF16)], [_spec_tile], name="proj_nsa_v")
    n_gate = 3 * n_heads
    w_gate = jnp.pad(seg(7, 8), ((0, 0), (0, LANES - n_gate)))
    b_gate = jnp.pad(p["nsa_gate_b"], (0, LANES - n_gate)).reshape(1, LANES)
    (gates,) = _matmul(h, w_gate, _ep_sigmoid_bias, [jax.ShapeDtypeStruct((s_len, LANES), F32)],
                       [_spec_tile], extras=(b_gate,), extra_specs=(_spec_col,), name="proj_nsa_gates")

    n_chunk = s_len // CMP_STRIDE
    chunks = kcvc.reshape(n_chunk, CMP_STRIDE, 2, n_groups, HEAD_DIM).transpose(2, 3, 0, 1, 4)
    chunks = chunks.reshape(2, n_groups, n_chunk, CMP_STRIDE * HEAD_DIM)
    kc = _compress(chunks[0], p["phi_pe_k"], p["phi_w1_k"], p["phi_w2_k"], p["nsa_kc_norm"], True)
    vc = _compress(chunks[1], p["phi_pe_v"], p["phi_w1_v"], p["phi_w2_v"], p["nsa_kc_norm"], False)

    n_slc = s_len // SLC_LEN
    n_slc_pad = -(-n_slc // LANES) * LANES
    o_cmp, sel_bias = _nsa_cmp(q_c, jnp.swapaxes(kc, 1, 2), vc, _overlap(n_chunk, n_slc_pad), n_slc)
    tk = min(512, s_len)
    ks_t = _keys_t(kskw[:, :kv_w], n_groups, tk, _onehot_t(s_len, SLC_LEN, tk))
    o_slc = _flash(q_r, sel_bias, ks_t, vsvw, rep=NSA_REP, tq=min(256, s_len),
                   keys_per_var=LANES * SLC_LEN, moba=False, out_dtype=F32, name="nsa_selected")
    kw_t = _keys_t(kskw[:, kv_w:], n_groups, LANES)
    o_win = _window(q_r, kw_t, vsvw, n_groups)
    o_a = _combine(o_cmp, o_slc, o_win, gates)

    (z_uv,) = _matmul(h, seg(8, 10), _ep_cast, [jax.ShapeDtypeStruct((s_len, 2 * sgu_w), F32)],
                      [_spec_tile], name="proj_sgu")
    o_b = _sgu(z_uv, p["sgu_norm"], p["sgu_w"], p["sgu_b"])

    (mq,) = _matmul(
        h, seg(10, 11), functools.partial(_ep_qrot, scale=scale),
        [jax.ShapeDtypeStruct((s_len, moba_w), BF16)], [_spec_tile],
        extras=(tile_gain(p["moba_q_norm"], moba_heads),) + rope_extras,
        extra_specs=(_spec_col,) + rope_specs, name="proj_moba_q")
    tm_k = min(1024, s_len)
    mk, mk_mean = _matmul(
        h, seg(11, 12), functools.partial(_ep_krot, block_mean=True),
        [jax.ShapeDtypeStruct((s_len, moba_w), BF16),
         jax.ShapeDtypeStruct((s_len // tm_k, tm_k // MOBA_BLOCK, moba_w), F32)],
        [_spec_tile, _spec_blockmean],
        extras=(tile_gain(p["moba_k_norm"], moba_heads),) + rope_extras,
        extra_specs=(_spec_col,) + rope_specs, tm=tm_k, name="proj_moba_k")
    (mv,) = _matmul(h, seg(12, 13), _ep_cast, [jax.ShapeDtypeStruct((s_len, moba_w), BF16)],
                    [_spec_tile], name="proj_moba_v")
    n_blk = s_len // MOBA_BLOCK
    km_t = mk_mean.reshape(n_blk, moba_heads, HEAD_DIM).transpose(1, 2, 0)
    km_t = jnp.pad(km_t, ((0, 0), (0, 0), (0, LANES - n_blk))).astype(BF16)
    mk_t = _keys_t(mk, moba_heads, tk, _onehot_t(s_len, MOBA_BLOCK, tk))
    o_c = _flash(mq, km_t, mk_t, mv, rep=1, tq=min(512, s_len), keys_per_var=LANES * MOBA_BLOCK,
                 moba=True, out_dtype=BF16, name="moba")

    (gm,) = _matmul(h, seg(13, 16), _ep_sigmoid, [jax.ShapeDtypeStruct((s_len, 3 * d_model), F32)],
                    [_spec_tile], name="proj_merge_gates")
    y = _merge(o_a, o_b, o_c, p["proj_a"].astype(BF16), p["proj_b"].astype(BF16),
               p["proj_c"].astype(BF16), gm)
    (x,) = _matmul(y, p["w_out"].astype(BF16), _ep_residual, [jax.ShapeDtypeStruct((s_len, d_model), F32)],
                   [_spec_tile], extras=(x,), extra_specs=(_spec_tile,), name="out_proj")

    h2 = _rmsnorm(x, p["norm_mlp"])
    (hid,) = _matmul(h2, p["mlp_w1"].astype(BF16), _ep_relu2,
                     [jax.ShapeDtypeStruct((s_len, p["mlp_w1"].shape[1]), BF16)], [_spec_tile], name="mlp_up")
    (x,) = _matmul(hid, p["mlp_w2"].astype(BF16), _ep_residual, [jax.ShapeDtypeStruct((s_len, d_model), F32)],
                   [_spec_tile], extras=(x,), extra_specs=(_spec_tile,), name="mlp_down")
    return x


_LAYER_PARAMS = ("norm_mix", "norm_mlp", "w_in", "nsa_gate_b", "nsa_q_norm", "nsa_kc_norm", "nsa_ks_norm",
                 "nsa_kw_norm", "phi_pe_k", "phi_w1_k", "phi_w2_k", "phi_pe_v", "phi_w1_v", "phi_w2_v",
                 "sgu_norm", "sgu_w", "sgu_b", "moba_q_norm", "moba_k_norm", "proj_a", "proj_b", "proj_c",
                 "w_out", "mlp_w1", "mlp_w2")


def kernel(x, positions, norm_mix, norm_mlp, w_in, nsa_gate_b, nsa_q_norm, nsa_kc_norm, nsa_ks_norm, nsa_kw_norm, phi_pe_k, phi_w1_k, phi_w2_k, phi_pe_v, phi_w1_v, phi_w2_v, sgu_norm, sgu_w, sgu_b, moba_q_norm, moba_k_norm, proj_a, proj_b, proj_c, w_out, mlp_w1, mlp_w2):
    stacked = dict(zip(_LAYER_PARAMS, (norm_mix, norm_mlp, w_in, nsa_gate_b, nsa_q_norm, nsa_kc_norm,
                                       nsa_ks_norm, nsa_kw_norm, phi_pe_k, phi_w1_k, phi_w2_k, phi_pe_v,
                                       phi_w1_v, phi_w2_v, sgu_norm, sgu_w, sgu_b, moba_q_norm, moba_k_norm,
                                       proj_a, proj_b, proj_c, w_out, mlp_w1, mlp_w2)))
    depth = w_in.shape[0]
    inv = ROPE_THETA ** (-jnp.arange(0, HEAD_DIM, 2, dtype=F32) / HEAD_DIM)
    outs = []
    for b in range(x.shape[0]):
        ang = positions[b].astype(F32)[:, None] * inv
        cos = jnp.concatenate([jnp.cos(ang), jnp.cos(ang)], axis=-1)
        sin = jnp.concatenate([-jnp.sin(ang), jnp.sin(ang)], axis=-1)
        xb = x[b]
        for l in range(depth):
            xb = _layer(xb, cos, sin, {k: v[l] for k, v in stacked.items()})
        outs.append(xb)
    return jnp.stack(outs)
```

```python
import functools
import math

import jax
import jax.numpy as jnp
import numpy as np
from jax import lax
from jax.experimental import pallas as pl
from jax.experimental.pallas import tpu as pltpu

F32 = jnp.float32
BF16 = jnp.bfloat16

HEAD_DIM = 128
LANES = 128
ROPE_THETA = 10000.0
NORM_EPS = 1e-6
NEG = -1e30
FORCED = 1e9
BELOW_NEG = -3e38

NSA_REP = 4
CMP_LEN = 32
CMP_STRIDE = 16
SLC_LEN = 64
SLC_TOPK = 16
WINDOW = 512
SGU_GROUPS = 8
SGU_CHUNK = 128
MOBA_BLOCK = 256
MOBA_TOPK = 3

MIB = 1024 * 1024
VMEM_LIMIT = 52 * MIB


def _cparams(sem, vmem=VMEM_LIMIT):
    return pltpu.CompilerParams(dimension_semantics=sem, vmem_limit_bytes=vmem)


def _tile(n, pref):
    if n <= pref:
        return n
    t = (pref // LANES) * LANES
    while t >= LANES:
        if n % t == 0:
            return t
        t -= LANES
    raise ValueError(f"no 128-multiple tile divides {n}")


def _gelu(x):
    c = math.sqrt(2.0 / math.pi)
    return 0.5 * x * (1.0 + jnp.tanh(c * (x + 0.044715 * (x * x * x))))


def _sigmoid(x):
    return 1.0 / (1.0 + jnp.exp(-x))


def _head_norm(x, gain):
    return x * lax.rsqrt(jnp.mean(x * x, axis=-1, keepdims=True) + NORM_EPS) * gain


def _rope(x, cos, sin_signed):
    return x * cos + pltpu.roll(x, HEAD_DIM // 2, 1) * sin_signed


def _rmsnorm_body(x_ref, g_ref, o_ref):
    x = x_ref[...]
    y = x * lax.rsqrt(jnp.mean(x * x, axis=-1, keepdims=True) + NORM_EPS)
    o_ref[...] = (y * g_ref[...]).astype(o_ref.dtype)


def _rmsnorm(x, gain, tm=256):
    m, d = x.shape
    tm = min(tm, m)
    return pl.pallas_call(
        _rmsnorm_body,
        grid=(m // tm,),
        in_specs=[pl.BlockSpec((tm, d), lambda i: (i, 0)),
                  pl.BlockSpec((1, d), lambda i: (0, 0))],
        out_specs=pl.BlockSpec((tm, d), lambda i: (i, 0)),
        out_shape=jax.ShapeDtypeStruct((m, d), BF16),
        compiler_params=_cparams(("parallel",)),
        name="rmsnorm",
    )(x, gain.reshape(1, d))


def _mm_body(*refs, n_extra, n_out, nk, epilogue):
    a_ref, b_ref = refs[0], refs[1]
    extra = refs[2:2 + n_extra]
    outs = refs[2 + n_extra:2 + n_extra + n_out]
    if nk == 1:
        epilogue(jnp.dot(a_ref[...], b_ref[...], preferred_element_type=F32), extra, outs)
        return
    acc_ref = refs[-1]
    k = pl.program_id(2)

    @pl.when(k == 0)
    def _():
        acc_ref[...] = jnp.dot(a_ref[...], b_ref[...], preferred_element_type=F32)

    @pl.when(k > 0)
    def _():
        acc_ref[...] += jnp.dot(a_ref[...], b_ref[...], preferred_element_type=F32)

    @pl.when(k == nk - 1)
    def _():
        epilogue(acc_ref[...], extra, outs)


def _matmul(a, b, epilogue, out_shapes, out_specs, extras=(), extra_specs=(),
            tm=1024, tn=1024, tk=2048, name="matmul"):
    m, kdim = a.shape
    n = b.shape[1]
    tm, tn, tk = min(tm, m), _tile(n, tn), _tile(kdim, tk)
    nk = kdim // tk
    body = functools.partial(_mm_body, n_extra=len(extras), n_out=len(out_shapes), nk=nk,
                             epilogue=functools.partial(epilogue, tm=tm, tn=tn))
    scratch = [] if nk == 1 else [pltpu.VMEM((tm, tn), F32)]
    return pl.pallas_call(
        body,
        grid=(m // tm, n // tn, nk),
        in_specs=[pl.BlockSpec((tm, tk), lambda i, j, k: (i, k)),
                  pl.BlockSpec((tk, tn), lambda i, j, k: (k, j))]
                 + [s(tm, tn) for s in extra_specs],
        out_specs=[s(tm, tn) for s in out_specs],
        out_shape=out_shapes,
        scratch_shapes=scratch,
        compiler_params=_cparams(("parallel", "parallel", "arbitrary")),
        name=name,
    )(a, b, *extras)


def _spec_tile(tm, tn):
    return pl.BlockSpec((tm, tn), lambda i, j, k: (i, j))


def _spec_col(tm, tn):
    return pl.BlockSpec((1, tn), lambda i, j, k: (0, j))


def _spec_rope(tm, tn):
    return pl.BlockSpec((tm, HEAD_DIM), lambda i, j, k: (i, 0))


def _spec_blockmean(tm, tn):
    return pl.BlockSpec((1, tm // MOBA_BLOCK, tn), lambda i, j, k: (i, 0, j))


def _ep_cast(acc, extra, outs, *, tm, tn):
    outs[0][...] = acc.astype(outs[0].dtype)


def _ep_sigmoid_bias(acc, extra, outs, *, tm, tn):
    outs[0][...] = _sigmoid(acc + extra[0][...])


def _ep_sigmoid(acc, extra, outs, *, tm, tn):
    outs[0][...] = _sigmoid(acc)


def _ep_relu2(acc, extra, outs, *, tm, tn):
    r = jnp.maximum(acc, 0.0)
    outs[0][...] = (r * r).astype(outs[0].dtype)


def _ep_residual(acc, extra, outs, *, tm, tn):
    outs[0][...] = extra[0][...] + acc


def _ep_q(acc, extra, outs, *, tm, tn, scale):
    gain, cos, sin = extra[0][...], extra[1][...], extra[2][...]
    for hd in range(tn // HEAD_DIM):
        sl = slice(hd * HEAD_DIM, (hd + 1) * HEAD_DIM)
        y = _head_norm(acc[:, sl], gain[:, sl])
        outs[0][:, sl] = (y * scale).astype(BF16)
        outs[1][:, sl] = (_rope(y, cos, sin) * scale).astype(BF16)


def _ep_qrot(acc, extra, outs, *, tm, tn, scale):
    gain, cos, sin = extra[0][...], extra[1][...], extra[2][...]
    for hd in range(tn // HEAD_DIM):
        sl = slice(hd * HEAD_DIM, (hd + 1) * HEAD_DIM)
        y = _head_norm(acc[:, sl], gain[:, sl])
        outs[0][:, sl] = (_rope(y, cos, sin) * scale).astype(BF16)


def _ep_krot(acc, extra, outs, *, tm, tn, block_mean):
    gain, cos, sin = extra[0][...], extra[1][...], extra[2][...]
    for hd in range(tn // HEAD_DIM):
        sl = slice(hd * HEAD_DIM, (hd + 1) * HEAD_DIM)
        y = _rope(_head_norm(acc[:, sl], gain[:, sl]), cos, sin)
        outs[0][:, sl] = y.astype(BF16)
        if block_mean:
            for blk in range(tm // MOBA_BLOCK):
                rows = y[blk * MOBA_BLOCK:(blk + 1) * MOBA_BLOCK]
                outs[1][0, blk:blk + 1, sl] = jnp.mean(rows, axis=0, keepdims=True)


def _compress_body(a_ref, pe_ref, w1_ref, w2_ref, g_ref, o_ref, *, norm):
    half = CMP_STRIDE * HEAD_DIM
    a = a_ref[0]
    n_chunk = a.shape[0]
    x1 = (a + pe_ref[:, :half]).astype(BF16)
    x2 = (a + pe_ref[:, half:]).astype(BF16)
    p1 = jnp.dot(x1, w1_ref[:half, :], preferred_element_type=F32)
    p2 = jnp.dot(x2, w1_ref[half:, :], preferred_element_type=F32)
    h = _gelu(p1 + pltpu.roll(p2, n_chunk - 1, 0))
    o = jnp.dot(h.astype(BF16), w2_ref[...], preferred_element_type=F32)
    if norm:
        o = _head_norm(o, g_ref[...])
    o_ref[0] = o.astype(o_ref.dtype)


def _compress(a, pe, w1, w2, gain, norm):
    g, n_chunk, width = a.shape
    hidden = w1.shape[1]
    return pl.pallas_call(
        functools.partial(_compress_body, norm=norm),
        grid=(g,),
        in_specs=[pl.BlockSpec((1, n_chunk, width), lambda i: (i, 0, 0)),
                  pl.BlockSpec((1, 2 * width), lambda i: (0, 0)),
                  pl.BlockSpec((2 * width, hidden), lambda i: (0, 0)),
                  pl.BlockSpec((hidden, HEAD_DIM), lambda i: (0, 0)),
                  pl.BlockSpec((1, HEAD_DIM), lambda i: (0, 0))],
        out_specs=pl.BlockSpec((1, n_chunk, HEAD_DIM), lambda i: (i, 0, 0)),
        out_shape=jax.ShapeDtypeStruct((g, n_chunk, HEAD_DIM), BF16),
        compiler_params=_cparams(("parallel",)),
        name="nsa_compress",
    )(a, pe.reshape(1, 2 * width), w1.astype(BF16), w2.astype(BF16), gain.reshape(1, HEAD_DIM))


def _topk_mask(score, index_f, k, axis):
    n = score.shape[axis]
    sel = jnp.zeros(score.shape, dtype=jnp.bool_)
    for _ in range(k):
        m = jnp.max(score, axis=axis, keepdims=True)
        first = jnp.min(jnp.where(score == m, index_f, float(n)), axis=axis, keepdims=True)
        hit = index_f == first
        sel = jnp.logical_or(sel, hit)
        score = jnp.where(hit, BELOW_NEG, score)
    return sel


def _nsa_cmp_body(q_ref, kt_ref, v_ref, ovt_ref, o_ref, sbt_ref, *, tq, n_slc):
    qi = pl.program_id(1)
    q0 = qi * tq
    q = jnp.concatenate([q_ref[:, r * HEAD_DIM:(r + 1) * HEAD_DIM] for r in range(NSA_REP)], axis=0)
    s = jnp.dot(q, kt_ref[0], preferred_element_type=F32)
    n_pad = s.shape[-1]
    t1 = q0 + lax.broadcasted_iota(jnp.int32, (tq, 1), 0)
    t = jnp.concatenate([t1] * NSA_REP, axis=0)
    cmp_end = lax.broadcasted_iota(jnp.int32, (1, n_pad), 1) * CMP_STRIDE + (CMP_LEN - 1)
    s = jnp.where(cmp_end <= t, s, NEG)
    m = jnp.max(s, axis=-1, keepdims=True)
    e = jnp.exp2(s - m)
    inv = jnp.where(m > 0.5 * NEG, 1.0 / jnp.sum(e, axis=-1, keepdims=True), 0.0)
    p = e * inv
    o = jnp.dot(p.astype(BF16), v_ref[0], preferred_element_type=F32)
    for r in range(NSA_REP):
        o_ref[:, r * HEAD_DIM:(r + 1) * HEAD_DIM] = o[r * tq:(r + 1) * tq]
    ps = p[0:tq]
    for r in range(1, NSA_REP):
        ps = ps + p[r * tq:(r + 1) * tq]
    ps_hi = ps.astype(BF16)
    ps_lo = (ps - ps_hi.astype(F32)).astype(BF16)
    nt = (((1,), (1,)), ((), ()))
    imp = (lax.dot_general(ovt_ref[...], ps_hi, nt, preferred_element_type=F32)
           + lax.dot_general(ovt_ref[...], ps_lo, nt, preferred_element_type=F32))
    j = lax.broadcasted_iota(jnp.int32, imp.shape, 0)
    cur = (q0 + lax.broadcasted_iota(jnp.int32, (1, tq), 1)) // SLC_LEN
    forced = (j == 0) | (j == cur) | (j == cur - 1)
    allowed = j <= cur
    score = jnp.where(allowed & jnp.logical_not(forced), imp, NEG)
    score = jnp.where(j < n_slc, score, BELOW_NEG)
    sel = _topk_mask(score, j.astype(F32), min(SLC_TOPK, n_slc) - 3, 0)
    sbt_ref[0] = jnp.where((sel | forced) & allowed, 0.0, NEG).astype(BF16)


def _nsa_cmp(q_c, kc_t, vc, overlap_t, n_slc, tq=256):
    s_len = q_c.shape[0]
    tq = min(tq, s_len)
    g, _, n_pad = kc_t.shape
    n_slc_pad = overlap_t.shape[0]
    gw = NSA_REP * HEAD_DIM
    return pl.pallas_call(
        functools.partial(_nsa_cmp_body, tq=tq, n_slc=n_slc),
        grid=(g, s_len // tq),
        in_specs=[pl.BlockSpec((tq, gw), lambda gi, qi: (qi, gi)),
                  pl.BlockSpec((1, HEAD_DIM, n_pad), lambda gi, qi: (gi, 0, 0)),
                  pl.BlockSpec((1, n_pad, HEAD_DIM), lambda gi, qi: (gi, 0, 0)),
                  pl.BlockSpec((n_slc_pad, n_pad), lambda gi, qi: (0, 0))],
        out_specs=[pl.BlockSpec((tq, gw), lambda gi, qi: (qi, gi)),
                   pl.BlockSpec((1, n_slc_pad, tq), lambda gi, qi: (gi, 0, qi))],
        out_shape=[jax.ShapeDtypeStruct((s_len, g * gw), F32),
                   jax.ShapeDtypeStruct((g, n_slc_pad, s_len), BF16)],
        compiler_params=_cparams(("parallel", "parallel")),
        name="nsa_cmp_select",
    )(q_c, kc_t, vc, overlap_t)


def _flash_body(*refs, rep, tq, tk, keys_per_var, n_var, moba, n_split):
    if moba:
        q_ref, km_ref, kt_ref, v_ref, o_ref, qa_s, m_s, acc_s, s_s = refs
    else:
        q_ref, sb_ref, kt_ref, v_ref, o_ref, qa_s, m_s, acc_s, s_s = refs
    rows = rep * tq
    qi = pl.program_id(1)
    q0 = qi * tq
    t1 = q0 + lax.broadcasted_iota(jnp.int32, (tq, 1), 0)
    q = jnp.concatenate([q_ref[:, r * HEAD_DIM:(r + 1) * HEAD_DIM] for r in range(rep)], axis=0)
    if moba:
        sg = jnp.dot(q, km_ref[0], preferred_element_type=F32)
        n_blk = (kt_ref.shape[1] * tk) // MOBA_BLOCK
        j = lax.broadcasted_iota(jnp.int32, sg.shape, 1)
        cur = t1 // MOBA_BLOCK
        past = j < cur
        score = jnp.where(j < n_blk, jnp.where(past, sg, NEG), BELOW_NEG)
        sel = _topk_mask(score, j.astype(F32), min(MOBA_TOPK, n_blk), 1)
        bias = jnp.where((sel & past) | (j == cur), 0.0, NEG).astype(BF16)
        qa_s[0] = jnp.concatenate([q, bias], axis=1)
    else:
        for var in range(n_var):
            sb = sb_ref[0][:, var * LANES:(var + 1) * LANES]
            qa_s[var] = jnp.concatenate([q, jnp.concatenate([sb] * rep, axis=0)], axis=1)
    m_s[...] = jnp.full(m_s.shape, NEG, F32)
    acc_s[...] = jnp.zeros(acc_s.shape, F32)
    t = jnp.concatenate([t1] * rep, axis=0)
    ones = jnp.ones((tk, HEAD_DIM), BF16)

    chunk = rows // n_split

    def scores(kt, c):
        var = (kt * tk) // keys_per_var if n_var > 1 else 0
        return jnp.dot(qa_s[var, c * chunk:(c + 1) * chunk, :], kt_ref[0, kt], preferred_element_type=F32)

    def step(kt, causal, prefetch):
        start = pl.multiple_of(kt * tk, tk)
        v_aug = jnp.concatenate([v_ref[pl.ds(start, tk), :], ones], axis=1)
        for c in range(n_split):
            rs = slice(c * chunk, (c + 1) * chunk)
            s = s_s[rs, :]
            if prefetch:
                s_s[rs, :] = scores(kt + 1, c)
            if causal:
                pos = kt * tk + lax.broadcasted_iota(jnp.int32, (1, tk), 1)
                s = jnp.where(pos <= t[rs], s, NEG)
            m_old = m_s[rs, :]
            m_new = jnp.maximum(m_old, jnp.max(s, axis=-1, keepdims=True))
            alpha = jnp.exp2(m_old - m_new)
            p = jnp.concatenate([jnp.exp2(s[:, b * LANES:(b + 1) * LANES] - m_new).astype(BF16)
                                 for b in range(tk // LANES)], axis=1)
            pv = jnp.dot(p, v_aug, preferred_element_type=F32)
            acc_s[rs, :] = jnp.concatenate([alpha, alpha], axis=1) * acc_s[rs, :] + pv
            m_s[rs, :] = m_new

    n_full = q0 // tk
    for c in range(n_split):
        s_s[c * chunk:(c + 1) * chunk, :] = scores(0, c)

    def full_step(kt, carry):
        step(kt, False, True)
        return carry

    lax.fori_loop(0, n_full, full_step, 0)
    step(n_full, True, False)
    acc = acc_s[...]
    o = acc[:, :HEAD_DIM] / acc[:, HEAD_DIM:]
    for r in range(rep):
        o_ref[:, r * HEAD_DIM:(r + 1) * HEAD_DIM] = o[r * tq:(r + 1) * tq].astype(o_ref.dtype)


def _flash(q, aux, kt_aug, v, *, rep, tq, keys_per_var, moba, out_dtype, name, n_split=2):
    s_len = q.shape[0]
    g, n_kt, kdim, tk = kt_aug.shape
    gw = rep * HEAD_DIM
    rows = rep * tq
    n_var = 1 if moba else aux.shape[-1] // LANES
    assert tk % tq == 0 and rows % n_split == 0, (tq, tk, rows, n_split)
    if moba:
        aux_spec = pl.BlockSpec((1, HEAD_DIM, LANES), lambda gi, qi: (gi, 0, 0))
    else:
        aux_spec = pl.BlockSpec((1, tq, n_var * LANES), lambda gi, qi: (gi, qi, 0))
    return pl.pallas_call(
        functools.partial(_flash_body, rep=rep, tq=tq, tk=tk, keys_per_var=keys_per_var,
                          n_var=n_var, moba=moba, n_split=n_split),
        grid=(g, s_len // tq),
        in_specs=[pl.BlockSpec((tq, gw), lambda gi, qi: (qi, gi)),
                  aux_spec,
                  pl.BlockSpec((1, n_kt, kdim, tk), lambda gi, qi: (gi, 0, 0, 0)),
                  pl.BlockSpec((s_len, HEAD_DIM), lambda gi, qi: (0, gi))],
        out_specs=pl.BlockSpec((tq, gw), lambda gi, qi: (qi, gi)),
        out_shape=jax.ShapeDtypeStruct((s_len, g * gw), out_dtype),
        scratch_shapes=[pltpu.VMEM((n_var, rows, 2 * HEAD_DIM), BF16),
                        pltpu.VMEM((rows, LANES), F32),
                        pltpu.VMEM((rows, 2 * HEAD_DIM), F32),
                        pltpu.VMEM((rows, tk), F32)],
        compiler_params=_cparams(("parallel", "arbitrary")),
        name=name,
    )(q, aux, kt_aug, v)


def _window_body(q_ref, kt_ref, v_ref, o_ref, *, tq):
    qi = pl.program_id(1)
    q0 = qi * tq
    n_tiles = (tq + WINDOW) // LANES
    q = jnp.concatenate([q_ref[:, r * HEAD_DIM:(r + 1) * HEAD_DIM] for r in range(NSA_REP)], axis=0)
    t1 = q0 + lax.broadcasted_iota(jnp.int32, (tq, 1), 0)
    t = jnp.concatenate([t1] * NSA_REP, axis=0)
    first = q0 // LANES - WINDOW // LANES
    lane = lax.broadcasted_iota(jnp.int32, (1, LANES), 1)
    scores, tiles = [], []
    for i in range(n_tiles):
        raw = first + i
        idx = jnp.maximum(raw, 0)
        tiles.append(idx)
        pos = raw * LANES + lane
        ok = (pos <= t) & (pos > t - WINDOW) & (pos >= 0)
        s = jnp.dot(q, kt_ref[0, idx], preferred_element_type=F32)
        scores.append(jnp.where(ok, s, NEG))
    s = jnp.concatenate(scores, axis=1)
    e = jnp.exp2(s - jnp.max(s, axis=-1, keepdims=True))
    p = (e / jnp.sum(e, axis=-1, keepdims=True)).astype(BF16)
    o = jnp.zeros((NSA_REP * tq, HEAD_DIM), F32)
    for i in range(n_tiles):
        start = pl.multiple_of(tiles[i] * LANES, LANES)
        o = o + jnp.dot(p[:, i * LANES:(i + 1) * LANES], v_ref[pl.ds(start, LANES), :],
                        preferred_element_type=F32)
    for r in range(NSA_REP):
        o_ref[:, r * HEAD_DIM:(r + 1) * HEAD_DIM] = o[r * tq:(r + 1) * tq]


def _window(q_r, kw_t, v, v_block0, tq=128):
    s_len = q_r.shape[0]
    g, n_kt, _, _ = kw_t.shape
    gw = NSA_REP * HEAD_DIM
    return pl.pallas_call(
        functools.partial(_window_body, tq=tq),
        grid=(g, s_len // tq),
        in_specs=[pl.BlockSpec((tq, gw), lambda gi, qi: (qi, gi)),
                  pl.BlockSpec((1, n_kt, HEAD_DIM, LANES), lambda gi, qi: (gi, 0, 0, 0)),
                  pl.BlockSpec((s_len, HEAD_DIM), lambda gi, qi: (0, v_block0 + gi))],
        out_specs=pl.BlockSpec((tq, gw), lambda gi, qi: (qi, gi)),
        out_shape=jax.ShapeDtypeStruct((s_len, g * gw), F32),
        compiler_params=_cparams(("parallel", "parallel")),
        name="nsa_window",
    )(q_r, kw_t, v)


def _combine_body(oc_ref, os_ref, ow_ref, g_ref, o_ref, *, n_heads):
    g = g_ref[...]
    for hd in range(n_heads):
        sl = slice(hd * HEAD_DIM, (hd + 1) * HEAD_DIM)
        o = (g[:, hd:hd + 1] * oc_ref[:, sl]
             + g[:, n_heads + hd:n_heads + hd + 1] * os_ref[:, sl]
             + g[:, 2 * n_heads + hd:2 * n_heads + hd + 1] * ow_ref[:, sl])
        o_ref[:, sl] = o.astype(o_ref.dtype)


def _combine(o_c, o_s, o_w, gates, tq=512):
    s_len, width = o_c.shape
    tq = min(tq, s_len)
    spec = pl.BlockSpec((tq, width), lambda i: (i, 0))
    return pl.pallas_call(
        functools.partial(_combine_body, n_heads=width // HEAD_DIM),
        grid=(s_len // tq,),
        in_specs=[spec, spec, spec, pl.BlockSpec((tq, gates.shape[1]), lambda i: (i, 0))],
        out_specs=spec,
        out_shape=jax.ShapeDtypeStruct((s_len, width), BF16),
        compiler_params=_cparams(("parallel",)),
        name="nsa_combine",
    )(o_c, o_s, o_w, gates)


def _sgu_body(zu_ref, zv_ref, gain_ref, w_ref, bt_ref, o_ref, *, tm):
    gd = zu_ref.shape[1] // SGU_GROUPS
    row = lax.broadcasted_iota(jnp.int32, (SGU_CHUNK, SGU_CHUNK), 0)
    col = lax.broadcasted_iota(jnp.int32, (SGU_CHUNK, SGU_CHUNK), 1)
    bt = bt_ref[...]
    for g in range(SGU_GROUPS):
        sl = slice(g * gd, (g + 1) * gd)
        v = _gelu(zv_ref[:, sl])
        v = (v * lax.rsqrt(jnp.mean(v * v, axis=-1, keepdims=True) + NORM_EPS) * gain_ref[:, sl]).astype(BF16)
        w = jnp.where(col <= row, w_ref[g], 0.0).astype(BF16)
        for c in range(tm // SGU_CHUNK):
            rs = slice(c * SGU_CHUNK, (c + 1) * SGU_CHUNK)
            mixed = jnp.dot(w, v[rs], preferred_element_type=F32) + bt[:, g:g + 1]
            o_ref[rs, sl] = (_gelu(zu_ref[rs, sl]) * mixed).astype(o_ref.dtype)


def _sgu(z_uv, gain, w_s, b_s, tm=512):
    s_len = z_uv.shape[0]
    width = z_uv.shape[1] // 2
    tm = min(tm, s_len)
    return pl.pallas_call(
        functools.partial(_sgu_body, tm=tm),
        grid=(s_len // tm,),
        in_specs=[pl.BlockSpec((tm, width), lambda i: (i, 0)),
                  pl.BlockSpec((tm, width), lambda i: (i, 1)),
                  pl.BlockSpec((1, width), lambda i: (0, 0)),
                  pl.BlockSpec((SGU_GROUPS, SGU_CHUNK, SGU_CHUNK), lambda i: (0, 0, 0)),
                  pl.BlockSpec((SGU_CHUNK, SGU_GROUPS), lambda i: (0, 0))],
        out_specs=pl.BlockSpec((tm, width), lambda i: (i, 0)),
        out_shape=jax.ShapeDtypeStruct((s_len, width), BF16),
        compiler_params=_cparams(("parallel",)),
        name="sgu",
    )(z_uv, z_uv, gain.reshape(1, width), w_s, b_s.T)


def _merge_body(oa_ref, ob_ref, oc_ref, pa_ref, pb_ref, pc_ref, ga_ref, gb_ref, gc_ref, y_ref):
    y = ga_ref[...] * jnp.dot(oa_ref[...], pa_ref[...], preferred_element_type=F32)
    y = y + gb_ref[...] * jnp.dot(ob_ref[...], pb_ref[...], preferred_element_type=F32)
    y = y + gc_ref[...] * jnp.dot(oc_ref[...], pc_ref[...], preferred_element_type=F32)
    y_ref[...] = y.astype(y_ref.dtype)


def _merge(o_a, o_b, o_c, p_a, p_b, p_c, gm, tm=512, tn=1024):
    s_len = o_a.shape[0]
    d = p_a.shape[1]
    tm, tn = min(tm, s_len), _tile(d, tn)
    nj = d // tn

    def rows(w):
        return pl.BlockSpec((tm, w), lambda i, j: (i, 0))

    def cols(kdim):
        return pl.BlockSpec((kdim, tn), lambda i, j: (0, j))

    def gate(off):
        return pl.BlockSpec((tm, tn), lambda i, j: (i, off * nj + j))

    return pl.pallas_call(
        _merge_body,
        grid=(s_len // tm, nj),
        in_specs=[rows(o_a.shape[1]), rows(o_b.shape[1]), rows(o_c.shape[1]),
                  cols(p_a.shape[0]), cols(p_b.shape[0]), cols(p_c.shape[0]),
                  gate(0), gate(1), gate(2)],
        out_specs=pl.BlockSpec((tm, tn), lambda i, j: (i, j)),
        out_shape=jax.ShapeDtypeStruct((s_len, d), BF16),
        compiler_params=_cparams(("parallel", "parallel")),
        name="gated_merge",
    )(o_a, o_b, o_c, p_a, p_b, p_c, gm, gm, gm)


def _keys_t(k, n_heads, tk, onehot_t=None):
    s_len = k.shape[0]
    kt = k.reshape(s_len // tk, tk, n_heads, HEAD_DIM).transpose(2, 0, 3, 1)
    if onehot_t is not None:
        kt = jnp.concatenate([kt, jnp.broadcast_to(onehot_t[None], (n_heads,) + onehot_t.shape)], axis=2)
    return kt


def _onehot_t(s_len, block, tk):
    key = np.arange(s_len)
    oh = ((key // block) % LANES)[None, :] == np.arange(LANES)[:, None]
    return jnp.asarray(oh.reshape(LANES, s_len // tk, tk).transpose(1, 0, 2), dtype=BF16)


def _overlap_t(n_pad, n_slc_pad):
    i = np.arange(n_pad)[None, :]
    j = np.arange(n_slc_pad)[:, None]
    ov = (i * CMP_STRIDE <= j * SLC_LEN + SLC_LEN - 1) & (i * CMP_STRIDE + CMP_LEN - 1 >= j * SLC_LEN)
    return jnp.asarray(ov, dtype=BF16)


def _layer(x, cos, sin, p):
    s_len, d_model = x.shape
    scale = HEAD_DIM ** -0.5 * math.log2(math.e)
    w_in = p["w_in"]
    nsa_w = p["proj_a"].shape[0]
    sgu_w = p["proj_b"].shape[0]
    moba_w = p["proj_c"].shape[0]
    n_heads = nsa_w // HEAD_DIM
    n_groups = n_heads // NSA_REP
    kv_w = n_groups * HEAD_DIM
    moba_heads = moba_w // HEAD_DIM
    sizes = (nsa_w, kv_w, kv_w, kv_w, kv_w, kv_w, kv_w, 3 * n_heads, sgu_w, sgu_w,
             moba_w, moba_w, moba_w, d_model, d_model, d_model)
    offs = np.concatenate([[0], np.cumsum(sizes)])

    def seg(a, b):
        return w_in[:, offs[a]:offs[b]].astype(BF16)

    def tile_gain(gain, reps):
        return jnp.tile(gain, reps).reshape(1, reps * HEAD_DIM)

    h = _rmsnorm(x, p["norm_mix"])
    rope_extras = (cos, sin)
    rope_specs = (_spec_rope, _spec_rope)

    q_c, q_r = _matmul(
        h, seg(0, 1), functools.partial(_ep_q, scale=scale),
        [jax.ShapeDtypeStruct((s_len, nsa_w), BF16)] * 2, [_spec_tile, _spec_tile],
        extras=(tile_gain(p["nsa_q_norm"], n_heads),) + rope_extras,
        extra_specs=(_spec_col,) + rope_specs, name="proj_nsa_q")
    (kcvc,) = _matmul(h, seg(1, 3), _ep_cast, [jax.ShapeDtypeStruct((s_len, 2 * kv_w), F32)],
                      [_spec_tile], name="proj_nsa_cmp_kv")
    (kskw,) = _matmul(
        h, jnp.concatenate([seg(3, 4), seg(5, 6)], axis=1), functools.partial(_ep_krot, block_mean=False),
        [jax.ShapeDtypeStruct((s_len, 2 * kv_w), BF16)], [_spec_tile],
        extras=(jnp.concatenate([tile_gain(p["nsa_ks_norm"], n_groups),
                                 tile_gain(p["nsa_kw_norm"], n_groups)], axis=1),) + rope_extras,
        extra_specs=(_spec_col,) + rope_specs, name="proj_nsa_k")
    (vsvw,) = _matmul(h, jnp.concatenate([seg(4, 5), seg(6, 7)], axis=1), _ep_cast,
                      [jax.ShapeDtypeStruct((s_len, 2 * kv_w), BF16)], [_spec_tile], name="proj_nsa_v")
    n_gate = 3 * n_heads
    w_gate = jnp.pad(seg(7, 8), ((0, 0), (0, LANES - n_gate)))
    b_gate = jnp.pad(p["nsa_gate_b"], (0, LANES - n_gate)).reshape(1, LANES)
    (gates,) = _matmul(h, w_gate, _ep_sigmoid_bias, [jax.ShapeDtypeStruct((s_len, LANES), F32)],
                       [_spec_tile], extras=(b_gate,), extra_specs=(_spec_col,), name="proj_nsa_gates")

    n_chunk = s_len // CMP_STRIDE
    chunks = kcvc.reshape(n_chunk, CMP_STRIDE, 2, n_groups, HEAD_DIM).transpose(2, 3, 0, 1, 4)
    chunks = chunks.reshape(2, n_groups, n_chunk, CMP_STRIDE * HEAD_DIM)
    kc = _compress(chunks[0], p["phi_pe_k"], p["phi_w1_k"], p["phi_w2_k"], p["nsa_kc_norm"], True)
    vc = _compress(chunks[1], p["phi_pe_v"], p["phi_w1_v"], p["phi_w2_v"], p["nsa_kc_norm"], False)

    n_slc = s_len // SLC_LEN
    n_slc_pad = -(-n_slc // LANES) * LANES
    o_cmp, sel_bias_t = _nsa_cmp(q_c, jnp.swapaxes(kc, 1, 2), vc, _overlap_t(n_chunk, n_slc_pad), n_slc)
    sel_bias = jnp.swapaxes(sel_bias_t, 1, 2)
    tk = min(512, s_len)
    ks_t = _keys_t(kskw[:, :kv_w], n_groups, tk, _onehot_t(s_len, SLC_LEN, tk))
    o_slc = _flash(q_r, sel_bias, ks_t, vsvw, rep=NSA_REP, tq=min(256, s_len),
                   keys_per_var=LANES * SLC_LEN, moba=False, out_dtype=F32, name="nsa_selected", n_split=4)
    kw_t = _keys_t(kskw[:, kv_w:], n_groups, LANES)
    o_win = _window(q_r, kw_t, vsvw, n_groups)
    o_a = _combine(o_cmp, o_slc, o_win, gates)

    (z_uv,) = _matmul(h, seg(8, 10), _ep_cast, [jax.ShapeDtypeStruct((s_len, 2 * sgu_w), F32)],
                      [_spec_tile], name="proj_sgu")
    o_b = _sgu(z_uv, p["sgu_norm"], p["sgu_w"], p["sgu_b"])

    (mq,) = _matmul(
        h, seg(10, 11), functools.partial(_ep_qrot, scale=scale),
        [jax.ShapeDtypeStruct((s_len, moba_w), BF16)], [_spec_tile],
        extras=(tile_gain(p["moba_q_norm"], moba_heads),) + rope_extras,
        extra_specs=(_spec_col,) + rope_specs, name="proj_moba_q")
    tm_k = min(1024, s_len)
    mk, mk_mean = _matmul(
        h, seg(11, 12), functools.partial(_ep_krot, block_mean=True),
        [jax.ShapeDtypeStruct((s_len, moba_w), BF16),
         jax.ShapeDtypeStruct((s_len // tm_k, tm_k // MOBA_BLOCK, moba_w), F32)],
        [_spec_tile, _spec_blockmean],
        extras=(tile_gain(p["moba_k_norm"], moba_heads),) + rope_extras,
        extra_specs=(_spec_col,) + rope_specs, tm=tm_k, name="proj_moba_k")
    (mv,) = _matmul(h, seg(12, 13), _ep_cast, [jax.ShapeDtypeStruct((s_len, moba_w), BF16)],
                    [_spec_tile], name="proj_moba_v")
    n_blk = s_len // MOBA_BLOCK
    km_t = mk_mean.reshape(n_blk, moba_heads, HEAD_DIM).transpose(1, 2, 0)
    km_t = jnp.pad(km_t, ((0, 0), (0, 0), (0, LANES - n_blk))).astype(BF16)
    mk_t = _keys_t(mk, moba_heads, tk, _onehot_t(s_len, MOBA_BLOCK, tk))
    o_c = _flash(mq, km_t, mk_t, mv, rep=1, tq=min(512, s_len), keys_per_var=LANES * MOBA_BLOCK,
                 moba=True, out_dtype=BF16, name="moba")

    (gm,) = _matmul(h, seg(13, 16), _ep_sigmoid, [jax.ShapeDtypeStruct((s_len, 3 * d_model), F32)],
                    [_spec_tile], name="proj_merge_gates")
    y = _merge(o_a, o_b, o_c, p["proj_a"].astype(BF16), p["proj_b"].astype(BF16),
               p["proj_c"].astype(BF16), gm)
    (x,) = _matmul(y, p["w_out"].astype(BF16), _ep_residual, [jax.ShapeDtypeStruct((s_len, d_model), F32)],
                   [_spec_tile], extras=(x,), extra_specs=(_spec_tile,), name="out_proj")

    h2 = _rmsnorm(x, p["norm_mlp"])
    (hid,) = _matmul(h2, p["mlp_w1"].astype(BF16), _ep_relu2,
                     [jax.ShapeDtypeStruct((s_len, p["mlp_w1"].shape[1]), BF16)], [_spec_tile], name="mlp_up")
    (x,) = _matmul(hid, p["mlp_w2"].astype(BF16), _ep_residual, [jax.ShapeDtypeStruct((s_len, d_model), F32)],
                   [_spec_tile], extras=(x,), extra_specs=(_spec_tile,), name="mlp_down")
    return x


_LAYER_PARAMS = ("norm_mix", "norm_mlp", "w_in", "nsa_gate_b", "nsa_q_norm", "nsa_kc_norm", "nsa_ks_norm",
                 "nsa_kw_norm", "phi_pe_k", "phi_w1_k", "phi_w2_k", "phi_pe_v", "phi_w1_v", "phi_w2_v",
                 "sgu_norm", "sgu_w", "sgu_b", "moba_q_norm", "moba_k_norm", "proj_a", "proj_b", "proj_c",
                 "w_out", "mlp_w1", "mlp_w2")


def kernel(x, positions, norm_mix, norm_mlp, w_in, nsa_gate_b, nsa_q_norm, nsa_kc_norm, nsa_ks_norm, nsa_kw_norm, phi_pe_k, phi_w1_k, phi_w2_k, phi_pe_v, phi_w1_v, phi_w2_v, sgu_norm, sgu_w, sgu_b, moba_q_norm, moba_k_norm, proj_a, proj_b, proj_c, w_out, mlp_w1, mlp_w2):
    stacked = dict(zip(_LAYER_PARAMS, (norm_mix, norm_mlp, w_in, nsa_gate_b, nsa_q_norm, nsa_kc_norm,
                                       nsa_ks_norm, nsa_kw_norm, phi_pe_k, phi_w1_k, phi_w2_k, phi_pe_v,
                                       phi_w1_v, phi_w2_v, sgu_norm, sgu_w, sgu_b, moba_q_norm, moba_k_norm,
                                       proj_a, proj_b, proj_c, w_out, mlp_w1, mlp_w2)))
    depth = w_in.shape[0]
    inv = ROPE_THETA ** (-jnp.arange(0, HEAD_DIM, 2, dtype=F32) / HEAD_DIM)
    outs = []
    for b in range(x.shape[0]):
        ang = positions[b].astype(F32)[:, None] * inv
        cos = jnp.concatenate([jnp.cos(ang), jnp.cos(ang)], axis=-1)
        sin = jnp.concatenate([-jnp.sin(ang), jnp.sin(ang)], axis=-1)
        xb = x[b]
        for l in range(depth):
            xb = _layer(xb, cos, sin, {k: v[l] for k, v in stacked.items()})
        outs.append(xb)
    return jnp.stack(outs)
```

```python
import functools
import math

import jax
import jax.numpy as jnp
import numpy as np
from jax import lax
from jax.experimental import pallas as pl
from jax.experimental.pallas import tpu as pltpu

F32 = jnp.float32
BF16 = jnp.bfloat16

HEAD_DIM = 128
LANES = 128
ROPE_THETA = 10000.0
NORM_EPS = 1e-6
NEG = -1e30
FORCED = 1e9
BELOW_NEG = -3e38

NSA_REP = 4
CMP_LEN = 32
CMP_STRIDE = 16
SLC_LEN = 64
SLC_TOPK = 16
WINDOW = 512
SGU_GROUPS = 8
SGU_CHUNK = 128
MOBA_BLOCK = 256
MOBA_TOPK = 3

MIB = 1024 * 1024
VMEM_LIMIT = 52 * MIB


def _cparams(sem, vmem=VMEM_LIMIT):
    return pltpu.CompilerParams(dimension_semantics=sem, vmem_limit_bytes=vmem)


def _tile(n, pref):
    if n <= pref:
        return n
    t = (pref // LANES) * LANES
    while t >= LANES:
        if n % t == 0:
            return t
        t -= LANES
    raise ValueError(f"no 128-multiple tile divides {n}")


def _gelu(x):
    c = math.sqrt(2.0 / math.pi)
    return 0.5 * x * (1.0 + jnp.tanh(c * (x + 0.044715 * (x * x * x))))


def _sigmoid(x):
    return 1.0 / (1.0 + jnp.exp(-x))


def _head_norm(x, gain):
    return x * lax.rsqrt(jnp.mean(x * x, axis=-1, keepdims=True) + NORM_EPS) * gain


def _rope(x, cos, sin_signed):
    return x * cos + pltpu.roll(x, HEAD_DIM // 2, 1) * sin_signed


def _rmsnorm_body(x_ref, g_ref, o_ref):
    x = x_ref[...]
    y = x * lax.rsqrt(jnp.mean(x * x, axis=-1, keepdims=True) + NORM_EPS)
    o_ref[...] = (y * g_ref[...]).astype(o_ref.dtype)


def _rmsnorm(x, gain, tm=256):
    m, d = x.shape
    tm = min(tm, m)
    return pl.pallas_call(
        _rmsnorm_body,
        grid=(m // tm,),
        in_specs=[pl.BlockSpec((tm, d), lambda i: (i, 0)),
                  pl.BlockSpec((1, d), lambda i: (0, 0))],
        out_specs=pl.BlockSpec((tm, d), lambda i: (i, 0)),
        out_shape=jax.ShapeDtypeStruct((m, d), BF16),
        compiler_params=_cparams(("parallel",)),
        name="rmsnorm",
    )(x, gain.reshape(1, d))


def _mm_body(*refs, n_extra, n_out, nk, epilogue):
    a_ref, b_ref = refs[0], refs[1]
    extra = refs[2:2 + n_extra]
    outs = refs[2 + n_extra:2 + n_extra + n_out]
    if nk == 1:
        epilogue(jnp.dot(a_ref[...], b_ref[...], preferred_element_type=F32), extra, outs)
        return
    acc_ref = refs[-1]
    k = pl.program_id(2)

    @pl.when(k == 0)
    def _():
        acc_ref[...] = jnp.dot(a_ref[...], b_ref[...], preferred_element_type=F32)

    @pl.when(k > 0)
    def _():
        acc_ref[...] += jnp.dot(a_ref[...], b_ref[...], preferred_element_type=F32)

    @pl.when(k == nk - 1)
    def _():
        epilogue(acc_ref[...], extra, outs)


def _matmul(a, b, epilogue, out_shapes, out_specs, extras=(), extra_specs=(),
            tm=1024, tn=1024, tk=2048, name="matmul"):
    m, kdim = a.shape
    n = b.shape[1]
    tm, tn, tk = min(tm, m), _tile(n, tn), _tile(kdim, tk)
    nk = kdim // tk
    body = functools.partial(_mm_body, n_extra=len(extras), n_out=len(out_shapes), nk=nk,
                             epilogue=functools.partial(epilogue, tm=tm, tn=tn))
    scratch = [] if nk == 1 else [pltpu.VMEM((tm, tn), F32)]
    return pl.pallas_call(
        body,
        grid=(m // tm, n // tn, nk),
        in_specs=[pl.BlockSpec((tm, tk), lambda i, j, k: (i, k)),
                  pl.BlockSpec((tk, tn), lambda i, j, k: (k, j))]
                 + [s(tm, tn) for s in extra_specs],
        out_specs=[s(tm, tn) for s in out_specs],
        out_shape=out_shapes,
        scratch_shapes=scratch,
        compiler_params=_cparams(("parallel", "parallel", "arbitrary")),
        name=name,
    )(a, b, *extras)


def _spec_tile(tm, tn):
    return pl.BlockSpec((tm, tn), lambda i, j, k: (i, j))


def _spec_col(tm, tn):
    return pl.BlockSpec((1, tn), lambda i, j, k: (0, j))


def _spec_rope(tm, tn):
    return pl.BlockSpec((tm, HEAD_DIM), lambda i, j, k: (i, 0))


def _spec_blockmean(tm, tn):
    return pl.BlockSpec((1, tm // MOBA_BLOCK, tn), lambda i, j, k: (i, 0, j))


def _ep_cast(acc, extra, outs, *, tm, tn):
    outs[0][...] = acc.astype(outs[0].dtype)


def _ep_sigmoid_bias(acc, extra, outs, *, tm, tn):
    outs[0][...] = _sigmoid(acc + extra[0][...])


def _ep_sigmoid(acc, extra, outs, *, tm, tn):
    outs[0][...] = _sigmoid(acc)


def _ep_relu2(acc, extra, outs, *, tm, tn):
    r = jnp.maximum(acc, 0.0)
    outs[0][...] = (r * r).astype(outs[0].dtype)


def _ep_residual(acc, extra, outs, *, tm, tn):
    outs[0][...] = extra[0][...] + acc


def _ep_q(acc, extra, outs, *, tm, tn, scale):
    gain, cos, sin = extra[0][...], extra[1][...], extra[2][...]
    for hd in range(tn // HEAD_DIM):
        sl = slice(hd * HEAD_DIM, (hd + 1) * HEAD_DIM)
        y = _head_norm(acc[:, sl], gain[:, sl])
        outs[0][:, sl] = (y * scale).astype(BF16)
        outs[1][:, sl] = (_rope(y, cos, sin) * scale).astype(BF16)


def _ep_qrot(acc, extra, outs, *, tm, tn, scale):
    gain, cos, sin = extra[0][...], extra[1][...], extra[2][...]
    for hd in range(tn // HEAD_DIM):
        sl = slice(hd * HEAD_DIM, (hd + 1) * HEAD_DIM)
        y = _head_norm(acc[:, sl], gain[:, sl])
        outs[0][:, sl] = (_rope(y, cos, sin) * scale).astype(BF16)


def _ep_krot(acc, extra, outs, *, tm, tn, block_mean):
    gain, cos, sin = extra[0][...], extra[1][...], extra[2][...]
    for hd in range(tn // HEAD_DIM):
        sl = slice(hd * HEAD_DIM, (hd + 1) * HEAD_DIM)
        y = _rope(_head_norm(acc[:, sl], gain[:, sl]), cos, sin)
        outs[0][:, sl] = y.astype(BF16)
        if block_mean:
            for blk in range(tm // MOBA_BLOCK):
                rows = y[blk * MOBA_BLOCK:(blk + 1) * MOBA_BLOCK]
                outs[1][0, blk:blk + 1, sl] = jnp.mean(rows, axis=0, keepdims=True)


def _compress_body(a_ref, pe_ref, w1_ref, w2_ref, g_ref, o_ref, *, norm):
    half = CMP_STRIDE * HEAD_DIM
    a = a_ref[0]
    n_chunk = a.shape[0]
    x1 = (a + pe_ref[:, :half]).astype(BF16)
    x2 = (a + pe_ref[:, half:]).astype(BF16)
    p1 = jnp.dot(x1, w1_ref[:half, :], preferred_element_type=F32)
    p2 = jnp.dot(x2, w1_ref[half:, :], preferred_element_type=F32)
    h = _gelu(p1 + pltpu.roll(p2, n_chunk - 1, 0))
    o = jnp.dot(h.astype(BF16), w2_ref[...], preferred_element_type=F32)
    if norm:
        o = _head_norm(o, g_ref[...])
    o_ref[0] = o.astype(o_ref.dtype)


def _compress(a, pe, w1, w2, gain, norm):
    g, n_chunk, width = a.shape
    hidden = w1.shape[1]
    return pl.pallas_call(
        functools.partial(_compress_body, norm=norm),
        grid=(g,),
        in_specs=[pl.BlockSpec((1, n_chunk, width), lambda i: (i, 0, 0)),
                  pl.BlockSpec((1, 2 * width), lambda i: (0, 0)),
                  pl.BlockSpec((2 * width, hidden), lambda i: (0, 0)),
                  pl.BlockSpec((hidden, HEAD_DIM), lambda i: (0, 0)),
                  pl.BlockSpec((1, HEAD_DIM), lambda i: (0, 0))],
        out_specs=pl.BlockSpec((1, n_chunk, HEAD_DIM), lambda i: (i, 0, 0)),
        out_shape=jax.ShapeDtypeStruct((g, n_chunk, HEAD_DIM), BF16),
        compiler_params=_cparams(("parallel",)),
        name="nsa_compress",
    )(a, pe.reshape(1, 2 * width), w1.astype(BF16), w2.astype(BF16), gain.reshape(1, HEAD_DIM))


def _topk_mask(score, index_f, k, axis):
    n = score.shape[axis]
    sel = jnp.zeros(score.shape, dtype=jnp.bool_)
    for _ in range(k):
        m = jnp.max(score, axis=axis, keepdims=True)
        first = jnp.min(jnp.where(score == m, index_f, float(n)), axis=axis, keepdims=True)
        hit = index_f == first
        sel = jnp.logical_or(sel, hit)
        score = jnp.where(hit, BELOW_NEG, score)
    return sel


def _nsa_cmp_body(q_ref, kt_ref, v_ref, ovt_ref, o_ref, sbt_ref, *, tq, n_slc):
    qi = pl.program_id(1)
    q0 = qi * tq
    q = jnp.concatenate([q_ref[:, r * HEAD_DIM:(r + 1) * HEAD_DIM] for r in range(NSA_REP)], axis=0)
    s = jnp.dot(q, kt_ref[0], preferred_element_type=F32)
    n_pad = s.shape[-1]
    t1 = q0 + lax.broadcasted_iota(jnp.int32, (tq, 1), 0)
    t = jnp.concatenate([t1] * NSA_REP, axis=0)
    cmp_end = lax.broadcasted_iota(jnp.int32, (1, n_pad), 1) * CMP_STRIDE + (CMP_LEN - 1)
    s = jnp.where(cmp_end <= t, s, NEG)
    m = jnp.max(s, axis=-1, keepdims=True)
    e = jnp.exp2(s - m)
    inv = jnp.where(m > 0.5 * NEG, 1.0 / jnp.sum(e, axis=-1, keepdims=True), 0.0)
    p = e * inv
    o = jnp.dot(p.astype(BF16), v_ref[0], preferred_element_type=F32)
    for r in range(NSA_REP):
        o_ref[:, r * HEAD_DIM:(r + 1) * HEAD_DIM] = o[r * tq:(r + 1) * tq]
    ps = p[0:tq]
    for r in range(1, NSA_REP):
        ps = ps + p[r * tq:(r + 1) * tq]
    ps_hi = ps.astype(BF16)
    ps_lo = (ps - ps_hi.astype(F32)).astype(BF16)
    nt = (((1,), (1,)), ((), ()))
    imp = (lax.dot_general(ovt_ref[...], ps_hi, nt, preferred_element_type=F32)
           + lax.dot_general(ovt_ref[...], ps_lo, nt, preferred_element_type=F32))
    j = lax.broadcasted_iota(jnp.int32, imp.shape, 0)
    cur = (q0 + lax.broadcasted_iota(jnp.int32, (1, tq), 1)) // SLC_LEN
    forced = (j == 0) | (j == cur) | (j == cur - 1)
    allowed = j <= cur
    score = jnp.where(allowed & jnp.logical_not(forced), imp, NEG)
    score = jnp.where(j < n_slc, score, BELOW_NEG)
    sel = _topk_mask(score, j.astype(F32), min(SLC_TOPK, n_slc) - 3, 0)
    sbt_ref[0] = jnp.where((sel | forced) & allowed, 0.0, NEG).astype(BF16)


def _nsa_cmp(q_c, kc_t, vc, overlap_t, n_slc, tq=256):
    s_len = q_c.shape[0]
    tq = min(tq, s_len)
    g, _, n_pad = kc_t.shape
    n_slc_pad = overlap_t.shape[0]
    gw = NSA_REP * HEAD_DIM
    return pl.pallas_call(
        functools.partial(_nsa_cmp_body, tq=tq, n_slc=n_slc),
        grid=(g, s_len // tq),
        in_specs=[pl.BlockSpec((tq, gw), lambda gi, qi: (qi, gi)),
                  pl.BlockSpec((1, HEAD_DIM, n_pad), lambda gi, qi: (gi, 0, 0)),
                  pl.BlockSpec((1, n_pad, HEAD_DIM), lambda gi, qi: (gi, 0, 0)),
                  pl.BlockSpec((n_slc_pad, n_pad), lambda gi, qi: (0, 0))],
        out_specs=[pl.BlockSpec((tq, gw), lambda gi, qi: (qi, gi)),
                   pl.BlockSpec((1, n_slc_pad, tq), lambda gi, qi: (gi, 0, qi))],
        out_shape=[jax.ShapeDtypeStruct((s_len, g * gw), F32),
                   jax.ShapeDtypeStruct((g, n_slc_pad, s_len), BF16)],
        compiler_params=_cparams(("parallel", "parallel")),
        name="nsa_cmp_select",
    )(q_c, kc_t, vc, overlap_t)


def _flash_body(*refs, rep, tq, tk, keys_per_var, n_var, moba, n_split):
    if moba:
        q_ref, km_ref, kt_ref, v_ref, o_ref, qa_s, m_s, acc_s, s_s = refs
    else:
        q_ref, sb_ref, kt_ref, v_ref, o_ref, qa_s, m_s, acc_s, s_s = refs
    rows = rep * tq
    qi = pl.program_id(1)
    q0 = qi * tq
    t1 = q0 + lax.broadcasted_iota(jnp.int32, (tq, 1), 0)
    q = jnp.concatenate([q_ref[:, r * HEAD_DIM:(r + 1) * HEAD_DIM] for r in range(rep)], axis=0)
    if moba:
        sg = jnp.dot(q, km_ref[0], preferred_element_type=F32)
        n_blk = (kt_ref.shape[1] * tk) // MOBA_BLOCK
        j = lax.broadcasted_iota(jnp.int32, sg.shape, 1)
        cur = t1 // MOBA_BLOCK
        past = j < cur
        score = jnp.where(j < n_blk, jnp.where(past, sg, NEG), BELOW_NEG)
        sel = _topk_mask(score, j.astype(F32), min(MOBA_TOPK, n_blk), 1)
        bias = jnp.where((sel & past) | (j == cur), 0.0, NEG).astype(BF16)
        qa_s[0] = jnp.concatenate([q, bias], axis=1)
    else:
        for var in range(n_var):
            sb = sb_ref[0][:, var * LANES:(var + 1) * LANES]
            qa_s[var] = jnp.concatenate([q, jnp.concatenate([sb] * rep, axis=0)], axis=1)
    m_s[...] = jnp.full(m_s.shape, NEG, F32)
    acc_s[...] = jnp.zeros(acc_s.shape, F32)
    t = jnp.concatenate([t1] * rep, axis=0)
    ones = jnp.ones((tk, HEAD_DIM), BF16)

    chunk = rows // n_split

    def scores(kt, c):
        var = (kt * tk) // keys_per_var if n_var > 1 else 0
        return jnp.dot(qa_s[var, c * chunk:(c + 1) * chunk, :], kt_ref[0, kt], preferred_element_type=F32)

    def step(kt, causal, prefetch):
        start = pl.multiple_of(kt * tk, tk)
        v_aug = jnp.concatenate([v_ref[pl.ds(start, tk), :], ones], axis=1)
        for c in range(n_split):
            rs = slice(c * chunk, (c + 1) * chunk)
            s = s_s[rs, :]
            if prefetch:
                s_s[rs, :] = scores(kt + 1, c)
            if causal:
                pos = kt * tk + lax.broadcasted_iota(jnp.int32, (1, tk), 1)
                s = jnp.where(pos <= t[rs], s, NEG)
            m_old = m_s[rs, :]
            m_new = jnp.maximum(m_old, jnp.max(s, axis=-1, keepdims=True))
            alpha = jnp.exp2(m_old - m_new)
            p = jnp.concatenate([jnp.exp2(s[:, b * LANES:(b + 1) * LANES] - m_new).astype(BF16)
                                 for b in range(tk // LANES)], axis=1)
            pv = jnp.dot(p, v_aug, preferred_element_type=F32)
            acc_s[rs, :] = jnp.concatenate([alpha, alpha], axis=1) * acc_s[rs, :] + pv
            m_s[rs, :] = m_new

    n_full = q0 // tk
    n_diag = max(1, tq // tk)
    for c in range(n_split):
        s_s[c * chunk:(c + 1) * chunk, :] = scores(0, c)

    def full_step(kt, carry):
        step(kt, False, True)
        return carry

    lax.fori_loop(0, n_full, full_step, 0)
    for d in range(n_diag):
        step(n_full + d, True, d < n_diag - 1)
    acc = acc_s[...]
    o = acc[:, :HEAD_DIM] / acc[:, HEAD_DIM:]
    for r in range(rep):
        o_ref[:, r * HEAD_DIM:(r + 1) * HEAD_DIM] = o[r * tq:(r + 1) * tq].astype(o_ref.dtype)


def _flash(q, aux, kt_aug, v, *, rep, tq, keys_per_var, moba, out_dtype, name, n_split=2):
    s_len = q.shape[0]
    g, n_kt, kdim, tk = kt_aug.shape
    gw = rep * HEAD_DIM
    rows = rep * tq
    n_var = 1 if moba else aux.shape[-1] // LANES
    assert (tk % tq == 0 or tq % tk == 0) and rows % n_split == 0, (tq, tk, rows, n_split)
    if moba:
        aux_spec = pl.BlockSpec((1, HEAD_DIM, LANES), lambda gi, qi: (gi, 0, 0))
    else:
        aux_spec = pl.BlockSpec((1, tq, n_var * LANES), lambda gi, qi: (gi, qi, 0))
    return pl.pallas_call(
        functools.partial(_flash_body, rep=rep, tq=tq, tk=tk, keys_per_var=keys_per_var,
                          n_var=n_var, moba=moba, n_split=n_split),
        grid=(g, s_len // tq),
        in_specs=[pl.BlockSpec((tq, gw), lambda gi, qi: (qi, gi)),
                  aux_spec,
                  pl.BlockSpec((1, n_kt, kdim, tk), lambda gi, qi: (gi, 0, 0, 0)),
                  pl.BlockSpec((s_len, HEAD_DIM), lambda gi, qi: (0, gi))],
        out_specs=pl.BlockSpec((tq, gw), lambda gi, qi: (qi, gi)),
        out_shape=jax.ShapeDtypeStruct((s_len, g * gw), out_dtype),
        scratch_shapes=[pltpu.VMEM((n_var, rows, 2 * HEAD_DIM), BF16),
                        pltpu.VMEM((rows, LANES), F32),
                        pltpu.VMEM((rows, 2 * HEAD_DIM), F32),
                        pltpu.VMEM((rows, tk), F32)],
        compiler_params=_cparams(("parallel", "arbitrary")),
        name=name,
    )(q, aux, kt_aug, v)


def _window_body(q_ref, kt_ref, v_ref, o_ref, *, tq):
    qi = pl.program_id(1)
    q0 = qi * tq
    n_past = WINDOW // tq
    q = jnp.concatenate([q_ref[:, r * HEAD_DIM:(r + 1) * HEAD_DIM] for r in range(NSA_REP)], axis=0)
    t1 = q0 + lax.broadcasted_iota(jnp.int32, (tq, 1), 0)
    t = jnp.concatenate([t1] * NSA_REP, axis=0)
    lane = lax.broadcasted_iota(jnp.int32, (1, tq), 1)
    ones = jnp.ones((tq, HEAD_DIM), BF16)
    scores, tiles = [], []
    for i in range(n_past + 1):
        raw = qi - n_past + i
        idx = jnp.maximum(raw, 0)
        tiles.append(idx)
        s = jnp.dot(q, kt_ref[0, idx], preferred_element_type=F32)
        pos = raw * tq + lane
        if i == 0:
            s = jnp.where(pos > t - WINDOW, s, NEG)
        if i == n_past:
            s = jnp.where(pos <= t, s, NEG)
        else:
            s = s + jnp.where(raw >= 0, 0.0, NEG)
        scores.append(s)
    m = jnp.max(scores[0], axis=-1, keepdims=True)
    for s in scores[1:]:
        m = jnp.maximum(m, jnp.max(s, axis=-1, keepdims=True))
    acc = jnp.zeros((NSA_REP * tq, 2 * HEAD_DIM), F32)
    for i, s in enumerate(scores):
        start = pl.multiple_of(tiles[i] * tq, tq)
        v_aug = jnp.concatenate([v_ref[pl.ds(start, tq), :], ones], axis=1)
        acc = acc + jnp.dot(jnp.exp2(s - m).astype(BF16), v_aug, preferred_element_type=F32)
    o = acc[:, :HEAD_DIM] / acc[:, HEAD_DIM:]
    for r in range(NSA_REP):
        o_ref[:, r * HEAD_DIM:(r + 1) * HEAD_DIM] = o[r * tq:(r + 1) * tq]


def _window(q_r, kw_t, v, v_block0):
    s_len = q_r.shape[0]
    g, n_kt, _, tq = kw_t.shape
    assert WINDOW % tq == 0, tq
    gw = NSA_REP * HEAD_DIM
    return pl.pallas_call(
        functools.partial(_window_body, tq=tq),
        grid=(g, s_len // tq),
        in_specs=[pl.BlockSpec((tq, gw), lambda gi, qi: (qi, gi)),
                  pl.BlockSpec((1, n_kt, HEAD_DIM, tq), lambda gi, qi: (gi, 0, 0, 0)),
                  pl.BlockSpec((s_len, HEAD_DIM), lambda gi, qi: (0, v_block0 + gi))],
        out_specs=pl.BlockSpec((tq, gw), lambda gi, qi: (qi, gi)),
        out_shape=jax.ShapeDtypeStruct((s_len, g * gw), F32),
        compiler_params=_cparams(("parallel", "parallel")),
        name="nsa_window",
    )(q_r, kw_t, v)


def _combine_body(oc_ref, os_ref, ow_ref, g_ref, o_ref, *, n_heads):
    g = g_ref[...]
    for hd in range(n_heads):
        sl = slice(hd * HEAD_DIM, (hd + 1) * HEAD_DIM)
        o = (g[:, hd:hd + 1] * oc_ref[:, sl]
             + g[:, n_heads + hd:n_heads + hd + 1] * os_ref[:, sl]
             + g[:, 2 * n_heads + hd:2 * n_heads + hd + 1] * ow_ref[:, sl])
        o_ref[:, sl] = o.astype(o_ref.dtype)


def _combine(o_c, o_s, o_w, gates, tq=512):
    s_len, width = o_c.shape
    tq = min(tq, s_len)
    spec = pl.BlockSpec((tq, width), lambda i: (i, 0))
    return pl.pallas_call(
        functools.partial(_combine_body, n_heads=width // HEAD_DIM),
        grid=(s_len // tq,),
        in_specs=[spec, spec, spec, pl.BlockSpec((tq, gates.shape[1]), lambda i: (i, 0))],
        out_specs=spec,
        out_shape=jax.ShapeDtypeStruct((s_len, width), BF16),
        compiler_params=_cparams(("parallel",)),
        name="nsa_combine",
    )(o_c, o_s, o_w, gates)


def _sgu_body(zu_ref, zv_ref, gain_ref, w_ref, bt_ref, o_ref, *, tm):
    gd = zu_ref.shape[1] // SGU_GROUPS
    row = lax.broadcasted_iota(jnp.int32, (SGU_CHUNK, SGU_CHUNK), 0)
    col = lax.broadcasted_iota(jnp.int32, (SGU_CHUNK, SGU_CHUNK), 1)
    bt = bt_ref[...]
    for g in range(SGU_GROUPS):
        sl = slice(g * gd, (g + 1) * gd)
        v = _gelu(zv_ref[:, sl])
        v = (v * lax.rsqrt(jnp.mean(v * v, axis=-1, keepdims=True) + NORM_EPS) * gain_ref[:, sl]).astype(BF16)
        w = jnp.where(col <= row, w_ref[g], 0.0).astype(BF16)
        for c in range(tm // SGU_CHUNK):
            rs = slice(c * SGU_CHUNK, (c + 1) * SGU_CHUNK)
            mixed = jnp.dot(w, v[rs], preferred_element_type=F32) + bt[:, g:g + 1]
            o_ref[rs, sl] = (_gelu(zu_ref[rs, sl]) * mixed).astype(o_ref.dtype)


def _sgu(z_uv, gain, w_s, b_s, tm=512):
    s_len = z_uv.shape[0]
    width = z_uv.shape[1] // 2
    tm = min(tm, s_len)
    return pl.pallas_call(
        functools.partial(_sgu_body, tm=tm),
        grid=(s_len // tm,),
        in_specs=[pl.BlockSpec((tm, width), lambda i: (i, 0)),
                  pl.BlockSpec((tm, width), lambda i: (i, 1)),
                  pl.BlockSpec((1, width), lambda i: (0, 0)),
                  pl.BlockSpec((SGU_GROUPS, SGU_CHUNK, SGU_CHUNK), lambda i: (0, 0, 0)),
                  pl.BlockSpec((SGU_CHUNK, SGU_GROUPS), lambda i: (0, 0))],
        out_specs=pl.BlockSpec((tm, width), lambda i: (i, 0)),
        out_shape=jax.ShapeDtypeStruct((s_len, width), BF16),
        compiler_params=_cparams(("parallel",)),
        name="sgu",
    )(z_uv, z_uv, gain.reshape(1, width), w_s, b_s.T)


def _merge_body(oa_ref, ob_ref, oc_ref, pa_ref, pb_ref, pc_ref, ga_ref, gb_ref, gc_ref, y_ref):
    y = ga_ref[...] * jnp.dot(oa_ref[...], pa_ref[...], preferred_element_type=F32)
    y = y + gb_ref[...] * jnp.dot(ob_ref[...], pb_ref[...], preferred_element_type=F32)
    y = y + gc_ref[...] * jnp.dot(oc_ref[...], pc_ref[...], preferred_element_type=F32)
    y_ref[...] = y.astype(y_ref.dtype)


def _merge(o_a, o_b, o_c, p_a, p_b, p_c, gm, tm=512, tn=1024):
    s_len = o_a.shape[0]
    d = p_a.shape[1]
    tm, tn = min(tm, s_len), _tile(d, tn)
    nj = d // tn

    def rows(w):
        return pl.BlockSpec((tm, w), lambda i, j: (i, 0))

    def cols(kdim):
        return pl.BlockSpec((kdim, tn), lambda i, j: (0, j))

    def gate(off):
        return pl.BlockSpec((tm, tn), lambda i, j: (i, off * nj + j))

    return pl.pallas_call(
        _merge_body,
        grid=(s_len // tm, nj),
        in_specs=[rows(o_a.shape[1]), rows(o_b.shape[1]), rows(o_c.shape[1]),
                  cols(p_a.shape[0]), cols(p_b.shape[0]), cols(p_c.shape[0]),
                  gate(0), gate(1), gate(2)],
        out_specs=pl.BlockSpec((tm, tn), lambda i, j: (i, j)),
        out_shape=jax.ShapeDtypeStruct((s_len, d), BF16),
        compiler_params=_cparams(("parallel", "parallel")),
        name="gated_merge",
    )(o_a, o_b, o_c, p_a, p_b, p_c, gm, gm, gm)


def _keys_t(k, n_heads, tk, onehot_t=None):
    s_len = k.shape[0]
    kt = k.reshape(s_len // tk, tk, n_heads, HEAD_DIM).transpose(2, 0, 3, 1)
    if onehot_t is not None:
        kt = jnp.concatenate([kt, jnp.broadcast_to(onehot_t[None], (n_heads,) + onehot_t.shape)], axis=2)
    return kt


def _onehot_t(s_len, block, tk):
    key = np.arange(s_len)
    oh = ((key // block) % LANES)[None, :] == np.arange(LANES)[:, None]
    return jnp.asarray(oh.reshape(LANES, s_len // tk, tk).transpose(1, 0, 2), dtype=BF16)


def _overlap_t(n_pad, n_slc_pad):
    i = np.arange(n_pad)[None, :]
    j = np.arange(n_slc_pad)[:, None]
    ov = (i * CMP_STRIDE <= j * SLC_LEN + SLC_LEN - 1) & (i * CMP_STRIDE + CMP_LEN - 1 >= j * SLC_LEN)
    return jnp.asarray(ov, dtype=BF16)


def _layer(x, cos, sin, p):
    s_len, d_model = x.shape
    scale = HEAD_DIM ** -0.5 * math.log2(math.e)
    w_in = p["w_in"]
    nsa_w = p["proj_a"].shape[0]
    sgu_w = p["proj_b"].shape[0]
    moba_w = p["proj_c"].shape[0]
    n_heads = nsa_w // HEAD_DIM
    n_groups = n_heads // NSA_REP
    kv_w = n_groups * HEAD_DIM
    moba_heads = moba_w // HEAD_DIM
    sizes = (nsa_w, kv_w, kv_w, kv_w, kv_w, kv_w, kv_w, 3 * n_heads, sgu_w, sgu_w,
             moba_w, moba_w, moba_w, d_model, d_model, d_model)
    offs = np.concatenate([[0], np.cumsum(sizes)])

    def seg(a, b):
        return w_in[:, offs[a]:offs[b]].astype(BF16)

    def tile_gain(gain, reps):
        return jnp.tile(gain, reps).reshape(1, reps * HEAD_DIM)

    h = _rmsnorm(x, p["norm_mix"])
    rope_extras = (cos, sin)
    rope_specs = (_spec_rope, _spec_rope)

    q_c, q_r = _matmul(
        h, seg(0, 1), functools.partial(_ep_q, scale=scale),
        [jax.ShapeDtypeStruct((s_len, nsa_w), BF16)] * 2, [_spec_tile, _spec_tile],
        extras=(tile_gain(p["nsa_q_norm"], n_heads),) + rope_extras,
        extra_specs=(_spec_col,) + rope_specs, name="proj_nsa_q")
    (kcvc,) = _matmul(h, seg(1, 3), _ep_cast, [jax.ShapeDtypeStruct((s_len, 2 * kv_w), F32)],
                      [_spec_tile], name="proj_nsa_cmp_kv")
    (kskw,) = _matmul(
        h, jnp.concatenate([seg(3, 4), seg(5, 6)], axis=1), functools.partial(_ep_krot, block_mean=False),
        [jax.ShapeDtypeStruct((s_len, 2 * kv_w), BF16)], [_spec_tile],
        extras=(jnp.concatenate([tile_gain(p["nsa_ks_norm"], n_groups),
                                 tile_gain(p["nsa_kw_norm"], n_groups)], axis=1),) + rope_extras,
        extra_specs=(_spec_col,) + rope_specs, name="proj_nsa_k")
    (vsvw,) = _matmul(h, jnp.concatenate([seg(4, 5), seg(6, 7)], axis=1), _ep_cast,
                      [jax.ShapeDtypeStruct((s_len, 2 * kv_w), BF16)], [_spec_tile], name="proj_nsa_v")
    n_gate = 3 * n_heads
    w_gate = jnp.pad(seg(7, 8), ((0, 0), (0, LANES - n_gate)))
    b_gate = jnp.pad(p["nsa_gate_b"], (0, LANES - n_gate)).reshape(1, LANES)
    (gates,) = _matmul(h, w_gate, _ep_sigmoid_bias, [jax.ShapeDtypeStruct((s_len, LANES), F32)],
                       [_spec_tile], extras=(b_gate,), extra_specs=(_spec_col,), name="proj_nsa_gates")

    n_chunk = s_len // CMP_STRIDE
    chunks = kcvc.reshape(n_chunk, CMP_STRIDE, 2, n_groups, HEAD_DIM).transpose(2, 3, 0, 1, 4)
    chunks = chunks.reshape(2, n_groups, n_chunk, CMP_STRIDE * HEAD_DIM)
    kc = _compress(chunks[0], p["phi_pe_k"], p["phi_w1_k"], p["phi_w2_k"], p["nsa_kc_norm"], True)
    vc = _compress(chunks[1], p["phi_pe_v"], p["phi_w1_v"], p["phi_w2_v"], p["nsa_kc_norm"], False)

    n_slc = s_len // SLC_LEN
    n_slc_pad = -(-n_slc // LANES) * LANES
    o_cmp, sel_bias_t = _nsa_cmp(q_c, jnp.swapaxes(kc, 1, 2), vc, _overlap_t(n_chunk, n_slc_pad), n_slc)
    sel_bias = jnp.swapaxes(sel_bias_t, 1, 2)
    tk = min(512, s_len)
    ks_t = _keys_t(kskw[:, :kv_w], n_groups, tk, _onehot_t(s_len, SLC_LEN, tk))
    o_slc = _flash(q_r, sel_bias, ks_t, vsvw, rep=NSA_REP, tq=min(512, s_len),
                   keys_per_var=LANES * SLC_LEN, moba=False, out_dtype=F32, name="nsa_selected", n_split=8)
    kw_t = _keys_t(kskw[:, kv_w:], n_groups, min(256, s_len))
    o_win = _window(q_r, kw_t, vsvw, n_groups)
    o_a = _combine(o_cmp, o_slc, o_win, gates)

    (z_uv,) = _matmul(h, seg(8, 10), _ep_cast, [jax.ShapeDtypeStruct((s_len, 2 * sgu_w), F32)],
                      [_spec_tile], name="proj_sgu")
    o_b = _sgu(z_uv, p["sgu_norm"], p["sgu_w"], p["sgu_b"])

    (mq,) = _matmul(
        h, seg(10, 11), functools.partial(_ep_qrot, scale=scale),
        [jax.ShapeDtypeStruct((s_len, moba_w), BF16)], [_spec_tile],
        extras=(tile_gain(p["moba_q_norm"], moba_heads),) + rope_extras,
        extra_specs=(_spec_col,) + rope_specs, name="proj_moba_q")
    tm_k = min(1024, s_len)
    mk, mk_mean = _matmul(
        h, seg(11, 12), functools.partial(_ep_krot, block_mean=True),
        [jax.ShapeDtypeStruct((s_len, moba_w), BF16),
         jax.ShapeDtypeStruct((s_len // tm_k, tm_k // MOBA_BLOCK, moba_w), F32)],
        [_spec_tile, _spec_blockmean],
        extras=(tile_gain(p["moba_k_norm"], moba_heads),) + rope_extras,
        extra_specs=(_spec_col,) + rope_specs, tm=tm_k, name="proj_moba_k")
    (mv,) = _matmul(h, seg(12, 13), _ep_cast, [jax.ShapeDtypeStruct((s_len, moba_w), BF16)],
                    [_spec_tile], name="proj_moba_v")
    n_blk = s_len // MOBA_BLOCK
    km_t = mk_mean.reshape(n_blk, moba_heads, HEAD_DIM).transpose(1, 2, 0)
    km_t = jnp.pad(km_t, ((0, 0), (0, 0), (0, LANES - n_blk))).astype(BF16)
    mk_t = _keys_t(mk, moba_heads, tk, _onehot_t(s_len, MOBA_BLOCK, tk))
    o_c = _flash(mq, km_t, mk_t, mv, rep=1, tq=min(1024, s_len), keys_per_var=LANES * MOBA_BLOCK,
                 moba=True, out_dtype=BF16, name="moba", n_split=4)

    (gm,) = _matmul(h, seg(13, 16), _ep_sigmoid, [jax.ShapeDtypeStruct((s_len, 3 * d_model), F32)],
                    [_spec_tile], name="proj_merge_gates")
    y = _merge(o_a, o_b, o_c, p["proj_a"].astype(BF16), p["proj_b"].astype(BF16),
               p["proj_c"].astype(BF16), gm)
    (x,) = _matmul(y, p["w_out"].astype(BF16), _ep_residual, [jax.ShapeDtypeStruct((s_len, d_model), F32)],
                   [_spec_tile], extras=(x,), extra_specs=(_spec_tile,), name="out_proj")

    h2 = _rmsnorm(x, p["norm_mlp"])
    (hid,) = _matmul(h2, p["mlp_w1"].astype(BF16), _ep_relu2,
                     [jax.ShapeDtypeStruct((s_len, p["mlp_w1"].shape[1]), BF16)], [_spec_tile], name="mlp_up")
    (x,) = _matmul(hid, p["mlp_w2"].astype(BF16), _ep_residual, [jax.ShapeDtypeStruct((s_len, d_model), F32)],
                   [_spec_tile], extras=(x,), extra_specs=(_spec_tile,), name="mlp_down")
    return x


_LAYER_PARAMS = ("norm_mix", "norm_mlp", "w_in", "nsa_gate_b", "nsa_q_norm", "nsa_kc_norm", "nsa_ks_norm",
                 "nsa_kw_norm", "phi_pe_k", "phi_w1_k", "phi_w2_k", "phi_pe_v", "phi_w1_v", "phi_w2_v",
                 "sgu_norm", "sgu_w", "sgu_b", "moba_q_norm", "moba_k_norm", "proj_a", "proj_b", "proj_c",
                 "w_out", "mlp_w1", "mlp_w2")


def kernel(x, positions, norm_mix, norm_mlp, w_in, nsa_gate_b, nsa_q_norm, nsa_kc_norm, nsa_ks_norm, nsa_kw_norm, phi_pe_k, phi_w1_k, phi_w2_k, phi_pe_v, phi_w1_v, phi_w2_v, sgu_norm, sgu_w, sgu_b, moba_q_norm, moba_k_norm, proj_a, proj_b, proj_c, w_out, mlp_w1, mlp_w2):
    stacked = dict(zip(_LAYER_PARAMS, (norm_mix, norm_mlp, w_in, nsa_gate_b, nsa_q_norm, nsa_kc_norm,
                                       nsa_ks_norm, nsa_kw_norm, phi_pe_k, phi_w1_k, phi_w2_k, phi_pe_v,
                                       phi_w1_v, phi_w2_v, sgu_norm, sgu_w, sgu_b, moba_q_norm, moba_k_norm,
                                       proj_a, proj_b, proj_c, w_out, mlp_w1, mlp_w2)))
    depth = w_in.shape[0]
    inv = ROPE_THETA ** (-jnp.arange(0, HEAD_DIM, 2, dtype=F32) / HEAD_DIM)
    outs = []
    for b in range(x.shape[0]):
        ang = positions[b].astype(F32)[:, None] * inv
        cos = jnp.concatenate([jnp.cos(ang), jnp.cos(ang)], axis=-1)
        sin = jnp.concatenate([-jnp.sin(ang), jnp.sin(ang)], axis=-1)
        xb = x[b]
        for l in range(depth):
            xb = _layer(xb, cos, sin, {k: v[l] for k, v in stacked.items()})
        outs.append(xb)
    return jnp.stack(outs)
```

```python
import functools
import math

import jax
import jax.numpy as jnp
import numpy as np
from jax import lax
from jax.experimental import pallas as pl
from jax.experimental.pallas import tpu as pltpu

F32 = jnp.float32
BF16 = jnp.bfloat16

HEAD_DIM = 128
LANES = 128
ROPE_THETA = 10000.0
NORM_EPS = 1e-6
NEG = -1e30
FORCED = 1e9
BELOW_NEG = -3e38

NSA_REP = 4
CMP_LEN = 32
CMP_STRIDE = 16
SLC_LEN = 64
SLC_TOPK = 16
WINDOW = 512
SGU_GROUPS = 8
SGU_CHUNK = 128
MOBA_BLOCK = 256
MOBA_TOPK = 3

LIGHT_EPILOGUE_TS = 512
HEAVY_EPILOGUE_TS = 1024

MIB = 1024 * 1024
VMEM_LIMIT = 52 * MIB


def _cparams(sem, vmem=VMEM_LIMIT):
    return pltpu.CompilerParams(dimension_semantics=sem, vmem_limit_bytes=vmem)


def _tile(n, pref):
    if n <= pref:
        return n
    t = (pref // LANES) * LANES
    while t >= LANES:
        if n % t == 0:
            return t
        t -= LANES
    raise ValueError(f"no 128-multiple tile divides {n}")


def _gelu(x):
    c = math.sqrt(2.0 / math.pi)
    return 0.5 * x * (1.0 + jnp.tanh(c * (x + 0.044715 * (x * x * x))))


def _sigmoid(x):
    return 1.0 / (1.0 + jnp.exp(-x))


def _head_norm(x, gain):
    return x * lax.rsqrt(jnp.mean(x * x, axis=-1, keepdims=True) + NORM_EPS) * gain


def _rope(x, cos, sin_signed):
    return x * cos + pltpu.roll(x, HEAD_DIM // 2, 1) * sin_signed


def _rmsnorm_body(x_ref, g_ref, o_ref):
    x = x_ref[...]
    y = x * lax.rsqrt(jnp.mean(x * x, axis=-1, keepdims=True) + NORM_EPS)
    o_ref[...] = (y * g_ref[...]).astype(o_ref.dtype)


def _rmsnorm(x, gain, tm=256):
    m, d = x.shape
    tm = min(tm, m)
    return pl.pallas_call(
        _rmsnorm_body,
        grid=(m // tm,),
        in_specs=[pl.BlockSpec((tm, d), lambda i: (i, 0)),
                  pl.BlockSpec((1, d), lambda i: (0, 0))],
        out_specs=pl.BlockSpec((tm, d), lambda i: (i, 0)),
        out_shape=jax.ShapeDtypeStruct((m, d), BF16),
        compiler_params=_cparams(("parallel",)),
        name="rmsnorm",
    )(x, gain.reshape(1, d))


def _mm_body(*refs, n_extra, n_out, nk, tn, ts, epilogue):
    a_ref, b_ref = refs[0], refs[1]
    extra = refs[2:2 + n_extra]
    outs = refs[2 + n_extra:2 + n_extra + n_out]
    acc_ref = refs[-1] if nk > 1 else None

    def finish():
        for c0 in range(0, tn, ts):
            part = jnp.dot(a_ref[...], b_ref[:, c0:c0 + ts], preferred_element_type=F32)
            if nk > 1:
                part = part + acc_ref[:, c0:c0 + ts]
            epilogue(part, extra, outs, slice(c0, c0 + ts))

    if nk == 1:
        finish()
        return
    k = pl.program_id(2)
    split_finish = ts < tn

    @pl.when(k == 0)
    def _():
        acc_ref[...] = jnp.dot(a_ref[...], b_ref[...], preferred_element_type=F32)

    @pl.when((k > 0) & (k < nk - 1) if split_finish else k > 0)
    def _():
        acc_ref[...] += jnp.dot(a_ref[...], b_ref[...], preferred_element_type=F32)

    @pl.when(k == nk - 1)
    def _():
        if split_finish:
            finish()
        else:
            epilogue(acc_ref[...], extra, outs, slice(0, tn))


def _matmul(a, b, epilogue, out_shapes, out_specs, extras=(), extra_specs=(),
            tm=1024, tn=1024, tk=2048, ts=LIGHT_EPILOGUE_TS, name="matmul"):
    m, kdim = a.shape
    n = b.shape[1]
    tm, tn, tk = min(tm, m), _tile(n, tn), _tile(kdim, tk)
    ts = min(ts, tn)
    nk = kdim // tk
    body = functools.partial(_mm_body, n_extra=len(extras), n_out=len(out_shapes), nk=nk, tn=tn, ts=ts,
                             epilogue=epilogue)
    scratch = [] if nk == 1 else [pltpu.VMEM((tm, tn), F32)]
    return pl.pallas_call(
        body,
        grid=(m // tm, n // tn, nk),
        in_specs=[pl.BlockSpec((tm, tk), lambda i, j, k: (i, k)),
                  pl.BlockSpec((tk, tn), lambda i, j, k: (k, j))]
                 + [s(tm, tn) for s in extra_specs],
        out_specs=[s(tm, tn) for s in out_specs],
        out_shape=out_shapes,
        scratch_shapes=scratch,
        compiler_params=_cparams(("parallel", "parallel", "arbitrary")),
        name=name,
    )(a, b, *extras)


def _spec_tile(tm, tn):
    return pl.BlockSpec((tm, tn), lambda i, j, k: (i, j))


def _spec_col(tm, tn):
    return pl.BlockSpec((1, tn), lambda i, j, k: (0, j))


def _spec_rope(tm, tn):
    return pl.BlockSpec((tm, HEAD_DIM), lambda i, j, k: (i, 0))


def _spec_blockmean(tm, tn):
    return pl.BlockSpec((1, tm // MOBA_BLOCK, tn), lambda i, j, k: (i, 0, j))


def _ep_cast(acc, extra, outs, cols):
    outs[0][:, cols] = acc.astype(outs[0].dtype)


def _ep_sigmoid_bias(acc, extra, outs, cols):
    outs[0][:, cols] = _sigmoid(acc + extra[0][:, cols])


def _ep_sigmoid(acc, extra, outs, cols):
    outs[0][:, cols] = _sigmoid(acc)


def _ep_relu2(acc, extra, outs, cols):
    r = jnp.maximum(acc, 0.0)
    outs[0][:, cols] = (r * r).astype(outs[0].dtype)


def _ep_residual(acc, extra, outs, cols):
    outs[0][:, cols] = extra[0][:, cols] + acc


def _heads(cols):
    return [(slice(c - cols.start, c - cols.start + HEAD_DIM), slice(c, c + HEAD_DIM))
            for c in range(cols.start, cols.stop, HEAD_DIM)]


def _ep_q(acc, extra, outs, cols, *, scale):
    cos, sin = extra[1][...], extra[2][...]
    for a_sl, t_sl in _heads(cols):
        y = _head_norm(acc[:, a_sl], extra[0][:, t_sl])
        outs[0][:, t_sl] = (y * scale).astype(BF16)
        outs[1][:, t_sl] = (_rope(y, cos, sin) * scale).astype(BF16)


def _ep_qrot(acc, extra, outs, cols, *, scale):
    cos, sin = extra[1][...], extra[2][...]
    for a_sl, t_sl in _heads(cols):
        y = _head_norm(acc[:, a_sl], extra[0][:, t_sl])
        outs[0][:, t_sl] = (_rope(y, cos, sin) * scale).astype(BF16)


def _ep_krot(acc, extra, outs, cols, *, block_mean):
    cos, sin = extra[1][...], extra[2][...]
    for a_sl, t_sl in _heads(cols):
        y = _rope(_head_norm(acc[:, a_sl], extra[0][:, t_sl]), cos, sin)
        outs[0][:, t_sl] = y.astype(BF16)
        if block_mean:
            for blk in range(acc.shape[0] // MOBA_BLOCK):
                rows = y[blk * MOBA_BLOCK:(blk + 1) * MOBA_BLOCK]
                outs[1][0, blk:blk + 1, t_sl] = jnp.mean(rows, axis=0, keepdims=True)


def _compress_body(a_ref, pe_ref, w1_ref, w2_ref, g_ref, o_ref, *, norm):
    half = CMP_STRIDE * HEAD_DIM
    a = a_ref[0]
    n_chunk = a.shape[0]
    x1 = (a + pe_ref[:, :half]).astype(BF16)
    x2 = (a + pe_ref[:, half:]).astype(BF16)
    p1 = jnp.dot(x1, w1_ref[:half, :], preferred_element_type=F32)
    p2 = jnp.dot(x2, w1_ref[half:, :], preferred_element_type=F32)
    h = _gelu(p1 + pltpu.roll(p2, n_chunk - 1, 0))
    o = jnp.dot(h.astype(BF16), w2_ref[...], preferred_element_type=F32)
    if norm:
        o = _head_norm(o, g_ref[...])
    o_ref[0] = o.astype(o_ref.dtype)


def _compress(a, pe, w1, w2, gain, norm):
    g, n_chunk, width = a.shape
    hidden = w1.shape[1]
    return pl.pallas_call(
        functools.partial(_compress_body, norm=norm),
        grid=(g,),
        in_specs=[pl.BlockSpec((1, n_chunk, width), lambda i: (i, 0, 0)),
                  pl.BlockSpec((1, 2 * width), lambda i: (0, 0)),
                  pl.BlockSpec((2 * width, hidden), lambda i: (0, 0)),
                  pl.BlockSpec((hidden, HEAD_DIM), lambda i: (0, 0)),
                  pl.BlockSpec((1, HEAD_DIM), lambda i: (0, 0))],
        out_specs=pl.BlockSpec((1, n_chunk, HEAD_DIM), lambda i: (i, 0, 0)),
        out_shape=jax.ShapeDtypeStruct((g, n_chunk, HEAD_DIM), BF16),
        compiler_params=_cparams(("parallel",)),
        name="nsa_compress",
    )(a, pe.reshape(1, 2 * width), w1.astype(BF16), w2.astype(BF16), gain.reshape(1, HEAD_DIM))


def _topk_mask(score, index_f, k, axis):
    n = score.shape[axis]
    sel = jnp.zeros(score.shape, dtype=jnp.bool_)
    for _ in range(k):
        m = jnp.max(score, axis=axis, keepdims=True)
        first = jnp.min(jnp.where(score == m, index_f, float(n)), axis=axis, keepdims=True)
        hit = index_f == first
        sel = jnp.logical_or(sel, hit)
        score = jnp.where(hit, BELOW_NEG, score)
    return sel


def _nsa_cmp_body(q_ref, kt_ref, v_ref, ovt_ref, o_ref, sbt_ref, imp_s, *, tq, n_slc, cw):
    qi = pl.program_id(1)
    q0 = qi * tq
    n_pad = kt_ref.shape[-1]
    q = jnp.concatenate([q_ref[:, r * HEAD_DIM:(r + 1) * HEAD_DIM] for r in range(NSA_REP)], axis=0)
    t1 = q0 + lax.broadcasted_iota(jnp.int32, (tq, 1), 0)
    t = jnp.concatenate([t1] * NSA_REP, axis=0)

    def attend(width):
        s = jnp.dot(q, kt_ref[0, :, :width], preferred_element_type=F32)
        cmp_end = lax.broadcasted_iota(jnp.int32, (1, width), 1) * CMP_STRIDE + (CMP_LEN - 1)
        s = jnp.where(cmp_end <= t, s, NEG)
        m = jnp.max(s, axis=-1, keepdims=True)
        e = jnp.exp2(s - m)
        inv = jnp.where(m > 0.5 * NEG, 1.0 / jnp.sum(e, axis=-1, keepdims=True), 0.0)
        p = e * inv
        o = jnp.dot(p.astype(BF16), v_ref[0, :width, :], preferred_element_type=F32)
        for r in range(NSA_REP):
            o_ref[:, r * HEAD_DIM:(r + 1) * HEAD_DIM] = o[r * tq:(r + 1) * tq]
        ps = p[0:tq]
        for r in range(1, NSA_REP):
            ps = ps + p[r * tq:(r + 1) * tq]
        ps_hi = ps.astype(BF16)
        ps_lo = (ps - ps_hi.astype(F32)).astype(BF16)
        nt = (((1,), (1,)), ((), ()))
        ovt = ovt_ref[:, :width]
        imp_s[...] = (lax.dot_general(ovt, ps_hi, nt, preferred_element_type=F32)
                      + lax.dot_general(ovt, ps_lo, nt, preferred_element_type=F32))

    n_chunks = ((q0 + tq) // CMP_STRIDE - 1 + cw - 1) // cw
    for v in range(1, n_pad // cw + 1):
        pl.when(n_chunks == v)(functools.partial(attend, v * cw))

    imp = imp_s[...]
    j = lax.broadcasted_iota(jnp.int32, imp.shape, 0)
    cur = (q0 + lax.broadcasted_iota(jnp.int32, (1, tq), 1)) // SLC_LEN
    forced = (j == 0) | (j == cur) | (j == cur - 1)
    allowed = j <= cur
    score = jnp.where(allowed & jnp.logical_not(forced), imp, NEG)
    score = jnp.where(j < n_slc, score, BELOW_NEG)
    sel = _topk_mask(score, j.astype(F32), min(SLC_TOPK, n_slc) - 3, 0)
    sbt_ref[0] = jnp.where((sel | forced) & allowed, 0.0, NEG).astype(BF16)


def _nsa_cmp(q_c, kc_t, vc, overlap_t, n_slc, tq=256):
    s_len = q_c.shape[0]
    tq = min(tq, s_len)
    g, _, n_pad = kc_t.shape
    n_slc_pad = overlap_t.shape[0]
    gw = NSA_REP * HEAD_DIM
    cw = min(256, n_pad)
    assert n_pad % cw == 0, (n_pad, cw)
    return pl.pallas_call(
        functools.partial(_nsa_cmp_body, tq=tq, n_slc=n_slc, cw=cw),
        grid=(g, s_len // tq),
        in_specs=[pl.BlockSpec((tq, gw), lambda gi, qi: (qi, gi)),
                  pl.BlockSpec((1, HEAD_DIM, n_pad), lambda gi, qi: (gi, 0, 0)),
                  pl.BlockSpec((1, n_pad, HEAD_DIM), lambda gi, qi: (gi, 0, 0)),
                  pl.BlockSpec((n_slc_pad, n_pad), lambda gi, qi: (0, 0))],
        out_specs=[pl.BlockSpec((tq, gw), lambda gi, qi: (qi, gi)),
                   pl.BlockSpec((1, n_slc_pad, tq), lambda gi, qi: (gi, 0, qi))],
        out_shape=[jax.ShapeDtypeStruct((s_len, g * gw), F32),
                   jax.ShapeDtypeStruct((g, n_slc_pad, s_len), BF16)],
        scratch_shapes=[pltpu.VMEM((n_slc_pad, tq), F32)],
        compiler_params=_cparams(("parallel", "parallel")),
        name="nsa_cmp_select",
    )(q_c, kc_t, vc, overlap_t)


def _flash_body(q_ref, sb_ref, kt_ref, v_ref, o_ref, qa_s, m_s, acc_s, s_s, *,
                rep, tq, tk, keys_per_var, n_var, n_split):
    rows = rep * tq
    qi = pl.program_id(1)
    q0 = qi * tq
    t1 = q0 + lax.broadcasted_iota(jnp.int32, (tq, 1), 0)
    q = jnp.concatenate([q_ref[:, r * HEAD_DIM:(r + 1) * HEAD_DIM] for r in range(rep)], axis=0)
    for var in range(n_var):
        sb = sb_ref[0][:, var * LANES:(var + 1) * LANES]
        qa_s[var] = jnp.concatenate([q, jnp.concatenate([sb] * rep, axis=0)], axis=1)
    m_s[...] = jnp.full(m_s.shape, NEG, F32)
    acc_s[...] = jnp.zeros(acc_s.shape, F32)
    t = jnp.concatenate([t1] * rep, axis=0)
    ones = jnp.ones((tk, HEAD_DIM), BF16)

    chunk = rows // n_split

    def scores(kt, c):
        var = (kt * tk) // keys_per_var if n_var > 1 else 0
        return jnp.dot(qa_s[var, c * chunk:(c + 1) * chunk, :], kt_ref[0, kt], preferred_element_type=F32)

    def step(kt, causal, prefetch):
        start = pl.multiple_of(kt * tk, tk)
        v_aug = jnp.concatenate([v_ref[pl.ds(start, tk), :], ones], axis=1)
        for c in range(n_split):
            rs = slice(c * chunk, (c + 1) * chunk)
            s = s_s[rs, :]
            if prefetch:
                s_s[rs, :] = scores(kt + 1, c)
            if causal:
                pos = kt * tk + lax.broadcasted_iota(jnp.int32, (1, tk), 1)
                s = jnp.where(pos <= t[rs], s, NEG)
            m_old = m_s[rs, :]
            m_new = jnp.maximum(m_old, jnp.max(s, axis=-1, keepdims=True))
            alpha = jnp.exp2(m_old - m_new)
            p = jnp.concatenate([jnp.exp2(s[:, b * LANES:(b + 1) * LANES] - m_new).astype(BF16)
                                 for b in range(tk // LANES)], axis=1)
            pv = jnp.dot(p, v_aug, preferred_element_type=F32)
            acc_s[rs, :] = jnp.concatenate([alpha, alpha], axis=1) * acc_s[rs, :] + pv
            m_s[rs, :] = m_new

    n_full = q0 // tk
    n_diag = max(1, tq // tk)
    for c in range(n_split):
        s_s[c * chunk:(c + 1) * chunk, :] = scores(0, c)

    def full_step(kt, carry):
        step(kt, False, True)
        return carry

    lax.fori_loop(0, n_full, full_step, 0)
    for d in range(n_diag):
        step(n_full + d, True, d < n_diag - 1)
    acc = acc_s[...]
    o = acc[:, :HEAD_DIM] / acc[:, HEAD_DIM:]
    for r in range(rep):
        o_ref[:, r * HEAD_DIM:(r + 1) * HEAD_DIM] = o[r * tq:(r + 1) * tq].astype(o_ref.dtype)


def _flash(q, bias, kt_aug, v, *, rep, tq, keys_per_var, out_dtype, name, n_split):
    s_len = q.shape[0]
    g, n_kt, kdim, tk = kt_aug.shape
    gw = rep * HEAD_DIM
    rows = rep * tq
    n_var = bias.shape[-1] // LANES
    assert (tk % tq == 0 or tq % tk == 0) and rows % n_split == 0, (tq, tk, rows, n_split)
    return pl.pallas_call(
        functools.partial(_flash_body, rep=rep, tq=tq, tk=tk, keys_per_var=keys_per_var,
                          n_var=n_var, n_split=n_split),
        grid=(g, s_len // tq),
        in_specs=[pl.BlockSpec((tq, gw), lambda gi, qi: (qi, gi)),
                  pl.BlockSpec((1, tq, n_var * LANES), lambda gi, qi: (gi, qi, 0)),
                  pl.BlockSpec((1, n_kt, kdim, tk), lambda gi, qi: (gi, 0, 0, 0)),
                  pl.BlockSpec((s_len, HEAD_DIM), lambda gi, qi: (0, gi))],
        out_specs=pl.BlockSpec((tq, gw), lambda gi, qi: (qi, gi)),
        out_shape=jax.ShapeDtypeStruct((s_len, g * gw), out_dtype),
        scratch_shapes=[pltpu.VMEM((n_var, rows, 2 * HEAD_DIM), BF16),
                        pltpu.VMEM((rows, LANES), F32),
                        pltpu.VMEM((rows, 2 * HEAD_DIM), F32),
                        pltpu.VMEM((rows, tk), F32)],
        compiler_params=_cparams(("parallel", "arbitrary")),
        name=name,
    )(q, bias, kt_aug, v)


def _moba_select_body(q_ref, km_ref, bt_ref, *, tq, n_blk):
    q0 = pl.program_id(1) * tq
    nt = (((1,), (1,)), ((), ()))
    sg = lax.dot_general(km_ref[0], q_ref[...], nt, preferred_element_type=F32)
    j = lax.broadcasted_iota(jnp.int32, sg.shape, 0)
    cur = (q0 + lax.broadcasted_iota(jnp.int32, (1, tq), 1)) // MOBA_BLOCK
    past = j < cur
    score = jnp.where(j < n_blk, jnp.where(past, sg, NEG), BELOW_NEG)
    sel = _topk_mask(score, j.astype(F32), min(MOBA_TOPK, n_blk), 0)
    bt_ref[0] = jnp.where((sel & past) | (j == cur), 0.0, NEG).astype(BF16)


def _moba_select(q, k_mean, n_blk, tq=512):
    s_len = q.shape[0]
    tq = min(tq, s_len)
    n_heads = k_mean.shape[0]
    return pl.pallas_call(
        functools.partial(_moba_select_body, tq=tq, n_blk=n_blk),
        grid=(n_heads, s_len // tq),
        in_specs=[pl.BlockSpec((tq, HEAD_DIM), lambda hi, qi: (qi, hi)),
                  pl.BlockSpec((1, LANES, HEAD_DIM), lambda hi, qi: (hi, 0, 0))],
        out_specs=pl.BlockSpec((1, LANES, tq), lambda hi, qi: (hi, 0, qi)),
        out_shape=jax.ShapeDtypeStruct((n_heads, LANES, s_len), BF16),
        compiler_params=_cparams(("parallel", "parallel")),
        name="moba_select",
    )(q, k_mean)


def _window_body(q_ref, kt_ref, v_ref, o_ref, *, tq):
    qi = pl.program_id(1)
    q0 = qi * tq
    n_past = WINDOW // tq
    q = jnp.concatenate([q_ref[:, r * HEAD_DIM:(r + 1) * HEAD_DIM] for r in range(NSA_REP)], axis=0)
    t1 = q0 + lax.broadcasted_iota(jnp.int32, (tq, 1), 0)
    t = jnp.concatenate([t1] * NSA_REP, axis=0)
    lane = lax.broadcasted_iota(jnp.int32, (1, tq), 1)
    ones = jnp.ones((tq, HEAD_DIM), BF16)
    scores, tiles = [], []
    for i in range(n_past + 1):
        raw = qi - n_past + i
        idx = jnp.maximum(raw, 0)
        tiles.append(idx)
        s = jnp.dot(q, kt_ref[0, idx], preferred_element_type=F32)
        pos = raw * tq + lane
        if i == 0:
            s = jnp.where(pos > t - WINDOW, s, NEG)
        if i == n_past:
            s = jnp.where(pos <= t, s, NEG)
        else:
            s = s + jnp.where(raw >= 0, 0.0, NEG)
        scores.append(s)
    m = jnp.max(scores[0], axis=-1, keepdims=True)
    for s in scores[1:]:
        m = jnp.maximum(m, jnp.max(s, axis=-1, keepdims=True))
    acc = jnp.zeros((NSA_REP * tq, 2 * HEAD_DIM), F32)
    for i, s in enumerate(scores):
        start = pl.multiple_of(tiles[i] * tq, tq)
        v_aug = jnp.concatenate([v_ref[pl.ds(start, tq), :], ones], axis=1)
        acc = acc + jnp.dot(jnp.exp2(s - m).astype(BF16), v_aug, preferred_element_type=F32)
    o = acc[:, :HEAD_DIM] / acc[:, HEAD_DIM:]
    for r in range(NSA_REP):
        o_ref[:, r * HEAD_DIM:(r + 1) * HEAD_DIM] = o[r * tq:(r + 1) * tq]


def _window(q_r, kw_t, v, v_block0):
    s_len = q_r.shape[0]
    g, n_kt, _, tq = kw_t.shape
    assert WINDOW % tq == 0, tq
    gw = NSA_REP * HEAD_DIM
    return pl.pallas_call(
        functools.partial(_window_body, tq=tq),
        grid=(g, s_len // tq),
        in_specs=[pl.BlockSpec((tq, gw), lambda gi, qi: (qi, gi)),
                  pl.BlockSpec((1, n_kt, HEAD_DIM, tq), lambda gi, qi: (gi, 0, 0, 0)),
                  pl.BlockSpec((s_len, HEAD_DIM), lambda gi, qi: (0, v_block0 + gi))],
        out_specs=pl.BlockSpec((tq, gw), lambda gi, qi: (qi, gi)),
        out_shape=jax.ShapeDtypeStruct((s_len, g * gw), F32),
        compiler_params=_cparams(("parallel", "parallel")),
        name="nsa_window",
    )(q_r, kw_t, v)


def _combine_body(oc_ref, os_ref, ow_ref, g_ref, o_ref, *, n_heads):
    g = g_ref[...]
    for hd in range(n_heads):
        sl = slice(hd * HEAD_DIM, (hd + 1) * HEAD_DIM)
        o = (g[:, hd:hd + 1] * oc_ref[:, sl]
             + g[:, n_heads + hd:n_heads + hd + 1] * os_ref[:, sl]
             + g[:, 2 * n_heads + hd:2 * n_heads + hd + 1] * ow_ref[:, sl])
        o_ref[:, sl] = o.astype(o_ref.dtype)


def _combine(o_c, o_s, o_w, gates, tq=512):
    s_len, width = o_c.shape
    tq = min(tq, s_len)
    spec = pl.BlockSpec((tq, width), lambda i: (i, 0))
    return pl.pallas_call(
        functools.partial(_combine_body, n_heads=width // HEAD_DIM),
        grid=(s_len // tq,),
        in_specs=[spec, spec, spec, pl.BlockSpec((tq, gates.shape[1]), lambda i: (i, 0))],
        out_specs=spec,
        out_shape=jax.ShapeDtypeStruct((s_len, width), BF16),
        compiler_params=_cparams(("parallel",)),
        name="nsa_combine",
    )(o_c, o_s, o_w, gates)


def _sgu_body(zu_ref, zv_ref, gain_ref, w_ref, bt_ref, o_ref, *, tm):
    gd = zu_ref.shape[1] // SGU_GROUPS
    row = lax.broadcasted_iota(jnp.int32, (SGU_CHUNK, SGU_CHUNK), 0)
    col = lax.broadcasted_iota(jnp.int32, (SGU_CHUNK, SGU_CHUNK), 1)
    bt = bt_ref[...]
    for g in range(SGU_GROUPS):
        sl = slice(g * gd, (g + 1) * gd)
        v = _gelu(zv_ref[:, sl])
        v = (v * lax.rsqrt(jnp.mean(v * v, axis=-1, keepdims=True) + NORM_EPS) * gain_ref[:, sl]).astype(BF16)
        w = jnp.where(col <= row, w_ref[g], 0.0).astype(BF16)
        for c in range(tm // SGU_CHUNK):
            rs = slice(c * SGU_CHUNK, (c + 1) * SGU_CHUNK)
            mixed = jnp.dot(w, v[rs], preferred_element_type=F32) + bt[:, g:g + 1]
            o_ref[rs, sl] = (_gelu(zu_ref[rs, sl]) * mixed).astype(o_ref.dtype)


def _sgu(z_uv, gain, w_s, b_s, tm=512):
    s_len = z_uv.shape[0]
    width = z_uv.shape[1] // 2
    tm = min(tm, s_len)
    return pl.pallas_call(
        functools.partial(_sgu_body, tm=tm),
        grid=(s_len // tm,),
        in_specs=[pl.BlockSpec((tm, width), lambda i: (i, 0)),
                  pl.BlockSpec((tm, width), lambda i: (i, 1)),
                  pl.BlockSpec((1, width), lambda i: (0, 0)),
                  pl.BlockSpec((SGU_GROUPS, SGU_CHUNK, SGU_CHUNK), lambda i: (0, 0, 0)),
                  pl.BlockSpec((SGU_CHUNK, SGU_GROUPS), lambda i: (0, 0))],
        out_specs=pl.BlockSpec((tm, width), lambda i: (i, 0)),
        out_shape=jax.ShapeDtypeStruct((s_len, width), BF16),
        compiler_params=_cparams(("parallel",)),
        name="sgu",
    )(z_uv, z_uv, gain.reshape(1, width), w_s, b_s.T)


def _merge_body(oa_ref, ob_ref, oc_ref, pa_ref, pb_ref, pc_ref, ga_ref, gb_ref, gc_ref, y_ref):
    y = ga_ref[...] * jnp.dot(oa_ref[...], pa_ref[...], preferred_element_type=F32)
    y = y + gb_ref[...] * jnp.dot(ob_ref[...], pb_ref[...], preferred_element_type=F32)
    y = y + gc_ref[...] * jnp.dot(oc_ref[...], pc_ref[...], preferred_element_type=F32)
    y_ref[...] = y.astype(y_ref.dtype)


def _merge(o_a, o_b, o_c, p_a, p_b, p_c, gm, tm=512, tn=1024):
    s_len = o_a.shape[0]
    d = p_a.shape[1]
    tm, tn = min(tm, s_len), _tile(d, tn)
    nj = d // tn

    def rows(w):
        return pl.BlockSpec((tm, w), lambda i, j: (i, 0))

    def cols(kdim):
        return pl.BlockSpec((kdim, tn), lambda i, j: (0, j))

    def gate(off):
        return pl.BlockSpec((tm, tn), lambda i, j: (i, off * nj + j))

    return pl.pallas_call(
        _merge_body,
        grid=(s_len // tm, nj),
        in_specs=[rows(o_a.shape[1]), rows(o_b.shape[1]), rows(o_c.shape[1]),
                  cols(p_a.shape[0]), cols(p_b.shape[0]), cols(p_c.shape[0]),
                  gate(0), gate(1), gate(2)],
        out_specs=pl.BlockSpec((tm, tn), lambda i, j: (i, j)),
        out_shape=jax.ShapeDtypeStruct((s_len, d), BF16),
        compiler_params=_cparams(("parallel", "parallel")),
        name="gated_merge",
    )(o_a, o_b, o_c, p_a, p_b, p_c, gm, gm, gm)


def _keys_t(k, n_heads, tk, onehot_t=None):
    s_len = k.shape[0]
    kt = k.reshape(s_len // tk, tk, n_heads, HEAD_DIM).transpose(2, 0, 3, 1)
    if onehot_t is not None:
        kt = jnp.concatenate([kt, jnp.broadcast_to(onehot_t[None], (n_heads,) + onehot_t.shape)], axis=2)
    return kt


def _onehot_t(s_len, block, tk):
    key = np.arange(s_len)
    oh = ((key // block) % LANES)[None, :] == np.arange(LANES)[:, None]
    return jnp.asarray(oh.reshape(LANES, s_len // tk, tk).transpose(1, 0, 2), dtype=BF16)


def _overlap_t(n_pad, n_slc_pad):
    i = np.arange(n_pad)[None, :]
    j = np.arange(n_slc_pad)[:, None]
    ov = (i * CMP_STRIDE <= j * SLC_LEN + SLC_LEN - 1) & (i * CMP_STRIDE + CMP_LEN - 1 >= j * SLC_LEN)
    return jnp.asarray(ov, dtype=BF16)


def _layer(x, cos, sin, p):
    s_len, d_model = x.shape
    scale = HEAD_DIM ** -0.5 * math.log2(math.e)
    w_in = p["w_in"]
    nsa_w = p["proj_a"].shape[0]
    sgu_w = p["proj_b"].shape[0]
    moba_w = p["proj_c"].shape[0]
    n_heads = nsa_w // HEAD_DIM
    n_groups = n_heads // NSA_REP
    kv_w = n_groups * HEAD_DIM
    moba_heads = moba_w // HEAD_DIM
    sizes = (nsa_w, kv_w, kv_w, kv_w, kv_w, kv_w, kv_w, 3 * n_heads, sgu_w, sgu_w,
             moba_w, moba_w, moba_w, d_model, d_model, d_model)
    offs = np.concatenate([[0], np.cumsum(sizes)])

    def seg(a, b):
        return w_in[:, offs[a]:offs[b]].astype(BF16)

    def tile_gain(gain, reps):
        return jnp.tile(gain, reps).reshape(1, reps * HEAD_DIM)

    h = _rmsnorm(x, p["norm_mix"])
    rope_extras = (cos, sin)
    rope_specs = (_spec_rope, _spec_rope)

    q_c, q_r = _matmul(
        h, seg(0, 1), functools.partial(_ep_q, scale=scale),
        [jax.ShapeDtypeStruct((s_len, nsa_w), BF16)] * 2, [_spec_tile, _spec_tile],
        extras=(tile_gain(p["nsa_q_norm"], n_heads),) + rope_extras,
        extra_specs=(_spec_col,) + rope_specs, ts=HEAVY_EPILOGUE_TS, name="proj_nsa_q")
    (kcvc,) = _matmul(h, seg(1, 3), _ep_cast, [jax.ShapeDtypeStruct((s_len, 2 * kv_w), F32)],
                      [_spec_tile], name="proj_nsa_cmp_kv")
    (kskw,) = _matmul(
        h, jnp.concatenate([seg(3, 4), seg(5, 6)], axis=1), functools.partial(_ep_krot, block_mean=False),
        [jax.ShapeDtypeStruct((s_len, 2 * kv_w), BF16)], [_spec_tile],
        extras=(jnp.concatenate([tile_gain(p["nsa_ks_norm"], n_groups),
                                 tile_gain(p["nsa_kw_norm"], n_groups)], axis=1),) + rope_extras,
        extra_specs=(_spec_col,) + rope_specs, ts=HEAVY_EPILOGUE_TS, name="proj_nsa_k")
    (vsvw,) = _matmul(h, jnp.concatenate([seg(4, 5), seg(6, 7)], axis=1), _ep_cast,
                      [jax.ShapeDtypeStruct((s_len, 2 * kv_w), BF16)], [_spec_tile], name="proj_nsa_v")
    n_gate = 3 * n_heads
    w_gate = jnp.pad(seg(7, 8), ((0, 0), (0, LANES - n_gate)))
    b_gate = jnp.pad(p["nsa_gate_b"], (0, LANES - n_gate)).reshape(1, LANES)
    (gates,) = _matmul(h, w_gate, _ep_sigmoid_bias, [jax.ShapeDtypeStruct((s_len, LANES), F32)],
                       [_spec_tile], extras=(b_gate,), extra_specs=(_spec_col,), name="proj_nsa_gates")

    n_chunk = s_len // CMP_STRIDE
    chunks = kcvc.reshape(n_chunk, CMP_STRIDE, 2, n_groups, HEAD_DIM).transpose(2, 3, 0, 1, 4)
    chunks = chunks.reshape(2, n_groups, n_chunk, CMP_STRIDE * HEAD_DIM)
    kc = _compress(chunks[0], p["phi_pe_k"], p["phi_w1_k"], p["phi_w2_k"], p["nsa_kc_norm"], True)
    vc = _compress(chunks[1], p["phi_pe_v"], p["phi_w1_v"], p["phi_w2_v"], p["nsa_kc_norm"], False)

    n_slc = s_len // SLC_LEN
    n_slc_pad = -(-n_slc // LANES) * LANES
    o_cmp, sel_bias_t = _nsa_cmp(q_c, jnp.swapaxes(kc, 1, 2), vc, _overlap_t(n_chunk, n_slc_pad), n_slc)
    sel_bias = jnp.swapaxes(sel_bias_t, 1, 2)
    tk = min(512, s_len)
    ks_t = _keys_t(kskw[:, :kv_w], n_groups, tk, _onehot_t(s_len, SLC_LEN, tk))
    o_slc = _flash(q_r, sel_bias, ks_t, vsvw, rep=NSA_REP, tq=min(512, s_len),
                   keys_per_var=LANES * SLC_LEN, out_dtype=F32, name="nsa_selected", n_split=8)
    kw_t = _keys_t(kskw[:, kv_w:], n_groups, min(256, s_len))
    o_win = _window(q_r, kw_t, vsvw, n_groups)
    o_a = _combine(o_cmp, o_slc, o_win, gates)

    (z_uv,) = _matmul(h, seg(8, 10), _ep_cast, [jax.ShapeDtypeStruct((s_len, 2 * sgu_w), F32)],
                      [_spec_tile], name="proj_sgu")
    o_b = _sgu(z_uv, p["sgu_norm"], p["sgu_w"], p["sgu_b"])

    (mq,) = _matmul(
        h, seg(10, 11), functools.partial(_ep_qrot, scale=scale),
        [jax.ShapeDtypeStruct((s_len, moba_w), BF16)], [_spec_tile],
        extras=(tile_gain(p["moba_q_norm"], moba_heads),) + rope_extras,
        extra_specs=(_spec_col,) + rope_specs, ts=HEAVY_EPILOGUE_TS, name="proj_moba_q")
    tm_k = min(1024, s_len)
    mk, mk_mean = _matmul(
        h, seg(11, 12), functools.partial(_ep_krot, block_mean=True),
        [jax.ShapeDtypeStruct((s_len, moba_w), BF16),
         jax.ShapeDtypeStruct((s_len // tm_k, tm_k // MOBA_BLOCK, moba_w), F32)],
        [_spec_tile, _spec_blockmean],
        extras=(tile_gain(p["moba_k_norm"], moba_heads),) + rope_extras,
        extra_specs=(_spec_col,) + rope_specs, tm=tm_k, ts=HEAVY_EPILOGUE_TS, name="proj_moba_k")
    (mv,) = _matmul(h, seg(12, 13), _ep_cast, [jax.ShapeDtypeStruct((s_len, moba_w), BF16)],
                    [_spec_tile], name="proj_moba_v")
    n_blk = s_len // MOBA_BLOCK
    assert n_blk <= LANES, n_blk
    k_mean = mk_mean.reshape(n_blk, moba_heads, HEAD_DIM).transpose(1, 0, 2)
    k_mean = jnp.pad(k_mean, ((0, 0), (0, LANES - n_blk), (0, 0))).astype(BF16)
    moba_bias = jnp.swapaxes(_moba_select(mq, k_mean, n_blk), 1, 2)
    mk_t = _keys_t(mk, moba_heads, tk, _onehot_t(s_len, MOBA_BLOCK, tk))
    o_c = _flash(mq, moba_bias, mk_t, mv, rep=1, tq=min(1024, s_len), keys_per_var=LANES * MOBA_BLOCK,
                 out_dtype=BF16, name="moba", n_split=4)

    (gm,) = _matmul(h, seg(13, 16), _ep_sigmoid, [jax.ShapeDtypeStruct((s_len, 3 * d_model), F32)],
                    [_spec_tile], name="proj_merge_gates")
    y = _merge(o_a, o_b, o_c, p["proj_a"].astype(BF16), p["proj_b"].astype(BF16),
               p["proj_c"].astype(BF16), gm)
    (x,) = _matmul(y, p["w_out"].astype(BF16), _ep_residual, [jax.ShapeDtypeStruct((s_len, d_model), F32)],
                   [_spec_tile], extras=(x,), extra_specs=(_spec_tile,), name="out_proj")

    h2 = _rmsnorm(x, p["norm_mlp"])
    (hid,) = _matmul(h2, p["mlp_w1"].astype(BF16), _ep_relu2,
                     [jax.ShapeDtypeStruct((s_len, p["mlp_w1"].shape[1]), BF16)], [_spec_tile], name="mlp_up")
    (x,) = _matmul(hid, p["mlp_w2"].astype(BF16), _ep_residual, [jax.ShapeDtypeStruct((s_len, d_model), F32)],
                   [_spec_tile], extras=(x,), extra_specs=(_spec_tile,), name="mlp_down")
    return x


_LAYER_PARAMS = ("norm_mix", "norm_mlp", "w_in", "nsa_gate_b", "nsa_q_norm", "nsa_kc_norm", "nsa_ks_norm",
                 "nsa_kw_norm", "phi_pe_k", "phi_w1_k", "phi_w2_k", "phi_pe_v", "phi_w1_v", "phi_w2_v",
                 "sgu_norm", "sgu_w", "sgu_b", "moba_q_norm", "moba_k_norm", "proj_a", "proj_b", "proj_c",
                 "w_out", "mlp_w1", "mlp_w2")


def kernel(x, positions, norm_mix, norm_mlp, w_in, nsa_gate_b, nsa_q_norm, nsa_kc_norm, nsa_ks_norm, nsa_kw_norm, phi_pe_k, phi_w1_k, phi_w2_k, phi_pe_v, phi_w1_v, phi_w2_v, sgu_norm, sgu_w, sgu_b, moba_q_norm, moba_k_norm, proj_a, proj_b, proj_c, w_out, mlp_w1, mlp_w2):
    stacked = dict(zip(_LAYER_PARAMS, (norm_mix, norm_mlp, w_in, nsa_gate_b, nsa_q_norm, nsa_kc_norm,
                                       nsa_ks_norm, nsa_kw_norm, phi_pe_k, phi_w1_k, phi_w2_k, phi_pe_v,
                                       phi_w1_v, phi_w2_v, sgu_norm, sgu_w, sgu_b, moba_q_norm, moba_k_norm,
                                       proj_a, proj_b, proj_c, w_out, mlp_w1, mlp_w2)))
    depth = w_in.shape[0]
    inv = ROPE_THETA ** (-jnp.arange(0, HEAD_DIM, 2, dtype=F32) / HEAD_DIM)
    outs = []
    for b in range(x.shape[0]):
        ang = positions[b].astype(F32)[:, None] * inv
        cos = jnp.concatenate([jnp.cos(ang), jnp.cos(ang)], axis=-1)
        sin = jnp.concatenate([-jnp.sin(ang), jnp.sin(ang)], axis=-1)
        xb = x[b]
        for l in range(depth):
            xb = _layer(xb, cos, sin, {k: v[l] for k, v in stacked.items()})
        outs.append(xb)
    return jnp.stack(outs)
```

```python
import functools
import math

import jax
import jax.numpy as jnp
import numpy as np
from jax import lax
from jax.experimental import pallas as pl
from jax.experimental.pallas import tpu as pltpu

F32 = jnp.float32
BF16 = jnp.bfloat16

HEAD_DIM = 128
LANES = 128
ROPE_THETA = 10000.0
NORM_EPS = 1e-6
NEG = -1e30
FORCED = 1e9
BELOW_NEG = -3e38

NSA_REP = 4
CMP_LEN = 32
CMP_STRIDE = 16
SLC_LEN = 64
SLC_TOPK = 16
WINDOW = 512
SGU_GROUPS = 8
SGU_CHUNK = 128
MOBA_BLOCK = 256
MOBA_TOPK = 3

LIGHT_EPILOGUE_TS = 512
HEAVY_EPILOGUE_TS = 1024

MIB = 1024 * 1024
VMEM_LIMIT = 52 * MIB


def _cparams(sem, vmem=VMEM_LIMIT):
    return pltpu.CompilerParams(dimension_semantics=sem, vmem_limit_bytes=vmem)


def _tile(n, pref):
    if n <= pref:
        return n
    t = (pref // LANES) * LANES
    while t >= LANES:
        if n % t == 0:
            return t
        t -= LANES
    raise ValueError(f"no 128-multiple tile divides {n}")


def _gelu(x):
    c = math.sqrt(2.0 / math.pi)
    return 0.5 * x * (1.0 + jnp.tanh(c * (x + 0.044715 * (x * x * x))))


def _sigmoid(x):
    return 1.0 / (1.0 + jnp.exp(-x))


def _head_norm(x, gain):
    return x * lax.rsqrt(jnp.mean(x * x, axis=-1, keepdims=True) + NORM_EPS) * gain


def _rope(x, cos, sin_signed):
    return x * cos + pltpu.roll(x, HEAD_DIM // 2, 1) * sin_signed


def _rmsnorm_body(x_ref, g_ref, o_ref):
    x = x_ref[...]
    y = x * lax.rsqrt(jnp.mean(x * x, axis=-1, keepdims=True) + NORM_EPS)
    o_ref[...] = (y * g_ref[...]).astype(o_ref.dtype)


def _rmsnorm(x, gain, tm=256):
    m, d = x.shape
    tm = min(tm, m)
    return pl.pallas_call(
        _rmsnorm_body,
        grid=(m // tm,),
        in_specs=[pl.BlockSpec((tm, d), lambda i: (i, 0)),
                  pl.BlockSpec((1, d), lambda i: (0, 0))],
        out_specs=pl.BlockSpec((tm, d), lambda i: (i, 0)),
        out_shape=jax.ShapeDtypeStruct((m, d), BF16),
        compiler_params=_cparams(("parallel",)),
        name="rmsnorm",
    )(x, gain.reshape(1, d))


def _mm_body(*refs, n_extra, n_out, nk, tn, ts, epilogue):
    a_ref, b_ref = refs[0], refs[1]
    extra = refs[2:2 + n_extra]
    outs = refs[2 + n_extra:2 + n_extra + n_out]
    acc_ref = refs[-1] if nk > 1 else None

    def finish():
        for c0 in range(0, tn, ts):
            part = jnp.dot(a_ref[...], b_ref[:, c0:c0 + ts], preferred_element_type=F32)
            if nk > 1:
                part = part + acc_ref[:, c0:c0 + ts]
            epilogue(part, extra, outs, slice(c0, c0 + ts))

    if nk == 1:
        finish()
        return
    k = pl.program_id(2)
    split_finish = ts < tn

    @pl.when(k == 0)
    def _():
        acc_ref[...] = jnp.dot(a_ref[...], b_ref[...], preferred_element_type=F32)

    @pl.when((k > 0) & (k < nk - 1) if split_finish else k > 0)
    def _():
        acc_ref[...] += jnp.dot(a_ref[...], b_ref[...], preferred_element_type=F32)

    @pl.when(k == nk - 1)
    def _():
        if split_finish:
            finish()
        else:
            epilogue(acc_ref[...], extra, outs, slice(0, tn))


def _matmul(a, b, epilogue, out_shapes, out_specs, extras=(), extra_specs=(),
            tm=1024, tn=1024, tk=2048, ts=LIGHT_EPILOGUE_TS, b_layer=0, name="matmul"):
    m, kdim = a.shape
    n = b.shape[-1]
    tm, tn, tk = min(tm, m), _tile(n, tn), _tile(kdim, tk)
    ts = min(ts, tn)
    if b.ndim == 3:
        b_spec = pl.BlockSpec((None, tk, tn), lambda i, j, k: (b_layer, k, j))
    else:
        b_spec = pl.BlockSpec((tk, tn), lambda i, j, k: (k, j))
    nk = kdim // tk
    body = functools.partial(_mm_body, n_extra=len(extras), n_out=len(out_shapes), nk=nk, tn=tn, ts=ts,
                             epilogue=epilogue)
    scratch = [] if nk == 1 else [pltpu.VMEM((tm, tn), F32)]
    return pl.pallas_call(
        body,
        grid=(m // tm, n // tn, nk),
        in_specs=[pl.BlockSpec((tm, tk), lambda i, j, k: (i, k)),
                  b_spec]
                 + [s(tm, tn) for s in extra_specs],
        out_specs=[s(tm, tn) for s in out_specs],
        out_shape=out_shapes,
        scratch_shapes=scratch,
        compiler_params=_cparams(("parallel", "parallel", "arbitrary")),
        name=name,
    )(a, b, *extras)


def _spec_tile(tm, tn):
    return pl.BlockSpec((tm, tn), lambda i, j, k: (i, j))


def _spec_col(tm, tn):
    return pl.BlockSpec((1, tn), lambda i, j, k: (0, j))


def _spec_rope(tm, tn):
    return pl.BlockSpec((tm, HEAD_DIM), lambda i, j, k: (i, 0))


def _spec_blockmean(tm, tn):
    return pl.BlockSpec((1, tm // MOBA_BLOCK, tn), lambda i, j, k: (i, 0, j))


def _ep_cast(acc, extra, outs, cols):
    outs[0][:, cols] = acc.astype(outs[0].dtype)


def _ep_sigmoid_bias(acc, extra, outs, cols):
    outs[0][:, cols] = _sigmoid(acc + extra[0][:, cols])


def _ep_sigmoid(acc, extra, outs, cols):
    outs[0][:, cols] = _sigmoid(acc)


def _ep_relu2(acc, extra, outs, cols):
    r = jnp.maximum(acc, 0.0)
    outs[0][:, cols] = (r * r).astype(outs[0].dtype)


def _ep_residual(acc, extra, outs, cols):
    outs[0][:, cols] = extra[0][:, cols] + acc


def _heads(cols):
    return [(slice(c - cols.start, c - cols.start + HEAD_DIM), slice(c, c + HEAD_DIM))
            for c in range(cols.start, cols.stop, HEAD_DIM)]


def _ep_q(acc, extra, outs, cols, *, scale):
    cos, sin = extra[1][...], extra[2][...]
    for a_sl, t_sl in _heads(cols):
        y = _head_norm(acc[:, a_sl], extra[0][:, t_sl])
        outs[0][:, t_sl] = (y * scale).astype(BF16)
        outs[1][:, t_sl] = (_rope(y, cos, sin) * scale).astype(BF16)


def _ep_qrot(acc, extra, outs, cols, *, scale):
    cos, sin = extra[1][...], extra[2][...]
    for a_sl, t_sl in _heads(cols):
        y = _head_norm(acc[:, a_sl], extra[0][:, t_sl])
        outs[0][:, t_sl] = (_rope(y, cos, sin) * scale).astype(BF16)


def _ep_krot(acc, extra, outs, cols, *, block_mean):
    cos, sin = extra[1][...], extra[2][...]
    for a_sl, t_sl in _heads(cols):
        y = _rope(_head_norm(acc[:, a_sl], extra[0][:, t_sl]), cos, sin)
        outs[0][:, t_sl] = y.astype(BF16)
        if block_mean:
            for blk in range(acc.shape[0] // MOBA_BLOCK):
                rows = y[blk * MOBA_BLOCK:(blk + 1) * MOBA_BLOCK]
                outs[1][0, blk:blk + 1, t_sl] = jnp.mean(rows, axis=0, keepdims=True)


def _compress_body(a_ref, pe_ref, w1_ref, w2_ref, g_ref, o_ref, *, norm):
    half = CMP_STRIDE * HEAD_DIM
    a = a_ref[0]
    n_chunk = a.shape[0]
    x1 = (a + pe_ref[:, :half]).astype(BF16)
    x2 = (a + pe_ref[:, half:]).astype(BF16)
    p1 = jnp.dot(x1, w1_ref[:half, :], preferred_element_type=F32)
    p2 = jnp.dot(x2, w1_ref[half:, :], preferred_element_type=F32)
    h = _gelu(p1 + pltpu.roll(p2, n_chunk - 1, 0))
    o = jnp.dot(h.astype(BF16), w2_ref[...], preferred_element_type=F32)
    if norm:
        o = _head_norm(o, g_ref[...])
    o_ref[0] = o.astype(o_ref.dtype)


def _compress(a, pe, w1, w2, gain, norm):
    g, n_chunk, width = a.shape
    hidden = w1.shape[1]
    return pl.pallas_call(
        functools.partial(_compress_body, norm=norm),
        grid=(g,),
        in_specs=[pl.BlockSpec((1, n_chunk, width), lambda i: (i, 0, 0)),
                  pl.BlockSpec((1, 2 * width), lambda i: (0, 0)),
                  pl.BlockSpec((2 * width, hidden), lambda i: (0, 0)),
                  pl.BlockSpec((hidden, HEAD_DIM), lambda i: (0, 0)),
                  pl.BlockSpec((1, HEAD_DIM), lambda i: (0, 0))],
        out_specs=pl.BlockSpec((1, n_chunk, HEAD_DIM), lambda i: (i, 0, 0)),
        out_shape=jax.ShapeDtypeStruct((g, n_chunk, HEAD_DIM), BF16),
        compiler_params=_cparams(("parallel",)),
        name="nsa_compress",
    )(a, pe.reshape(1, 2 * width), w1.astype(BF16), w2.astype(BF16), gain.reshape(1, HEAD_DIM))


def _topk_mask(score, index_f, k, axis):
    n = score.shape[axis]
    sel = jnp.zeros(score.shape, dtype=jnp.bool_)
    for _ in range(k):
        m = jnp.max(score, axis=axis, keepdims=True)
        first = jnp.min(jnp.where(score == m, index_f, float(n)), axis=axis, keepdims=True)
        hit = index_f == first
        sel = jnp.logical_or(sel, hit)
        score = jnp.where(hit, BELOW_NEG, score)
    return sel


def _nsa_cmp_body(q_ref, kt_ref, v_ref, ovt_ref, o_ref, sbt_ref, imp_s, *, tq, n_slc, cw):
    qi = pl.program_id(1)
    q0 = qi * tq
    n_pad = kt_ref.shape[-1]
    q = jnp.concatenate([q_ref[:, r * HEAD_DIM:(r + 1) * HEAD_DIM] for r in range(NSA_REP)], axis=0)
    t1 = q0 + lax.broadcasted_iota(jnp.int32, (tq, 1), 0)
    t = jnp.concatenate([t1] * NSA_REP, axis=0)

    def attend(width):
        s = jnp.dot(q, kt_ref[0, :, :width], preferred_element_type=F32)
        cmp_end = lax.broadcasted_iota(jnp.int32, (1, width), 1) * CMP_STRIDE + (CMP_LEN - 1)
        s = jnp.where(cmp_end <= t, s, NEG)
        m = jnp.max(s, axis=-1, keepdims=True)
        e = jnp.exp2(s - m)
        inv = jnp.where(m > 0.5 * NEG, 1.0 / jnp.sum(e, axis=-1, keepdims=True), 0.0)
        p = e * inv
        o = jnp.dot(p.astype(BF16), v_ref[0, :width, :], preferred_element_type=F32)
        for r in range(NSA_REP):
            o_ref[:, r * HEAD_DIM:(r + 1) * HEAD_DIM] = o[r * tq:(r + 1) * tq]
        ps = p[0:tq]
        for r in range(1, NSA_REP):
            ps = ps + p[r * tq:(r + 1) * tq]
        ps_hi = ps.astype(BF16)
        ps_lo = (ps - ps_hi.astype(F32)).astype(BF16)
        nt = (((1,), (1,)), ((), ()))
        ovt = ovt_ref[:, :width]
        imp_s[...] = (lax.dot_general(ovt, ps_hi, nt, preferred_element_type=F32)
                      + lax.dot_general(ovt, ps_lo, nt, preferred_element_type=F32))

    n_chunks = ((q0 + tq) // CMP_STRIDE - 1 + cw - 1) // cw
    for v in range(1, n_pad // cw + 1):
        pl.when(n_chunks == v)(functools.partial(attend, v * cw))

    imp = imp_s[...]
    j = lax.broadcasted_iota(jnp.int32, imp.shape, 0)
    cur = (q0 + lax.broadcasted_iota(jnp.int32, (1, tq), 1)) // SLC_LEN
    forced = (j == 0) | (j == cur) | (j == cur - 1)
    allowed = j <= cur
    score = jnp.where(allowed & jnp.logical_not(forced), imp, NEG)
    score = jnp.where(j < n_slc, score, BELOW_NEG)
    sel = _topk_mask(score, j.astype(F32), min(SLC_TOPK, n_slc) - 3, 0)
    sbt_ref[0] = jnp.where((sel | forced) & allowed, 0.0, NEG).astype(BF16)


def _nsa_cmp(q_c, kc_t, vc, overlap_t, n_slc, tq=256):
    s_len = q_c.shape[0]
    tq = min(tq, s_len)
    g, _, n_pad = kc_t.shape
    n_slc_pad = overlap_t.shape[0]
    gw = NSA_REP * HEAD_DIM
    cw = min(256, n_pad)
    assert n_pad % cw == 0, (n_pad, cw)
    return pl.pallas_call(
        functools.partial(_nsa_cmp_body, tq=tq, n_slc=n_slc, cw=cw),
        grid=(g, s_len // tq),
        in_specs=[pl.BlockSpec((tq, gw), lambda gi, qi: (qi, gi)),
                  pl.BlockSpec((1, HEAD_DIM, n_pad), lambda gi, qi: (gi, 0, 0)),
                  pl.BlockSpec((1, n_pad, HEAD_DIM), lambda gi, qi: (gi, 0, 0)),
                  pl.BlockSpec((n_slc_pad, n_pad), lambda gi, qi: (0, 0))],
        out_specs=[pl.BlockSpec((tq, gw), lambda gi, qi: (qi, gi)),
                   pl.BlockSpec((1, n_slc_pad, tq), lambda gi, qi: (gi, 0, qi))],
        out_shape=[jax.ShapeDtypeStruct((s_len, g * gw), F32),
                   jax.ShapeDtypeStruct((g, n_slc_pad, s_len), BF16)],
        scratch_shapes=[pltpu.VMEM((n_slc_pad, tq), F32)],
        compiler_params=_cparams(("parallel", "parallel")),
        name="nsa_cmp_select",
    )(q_c, kc_t, vc, overlap_t)


def _flash_body(q_ref, sb_ref, kt_ref, v_ref, o_ref, qa_s, m_s, acc_s, s_s, *,
                rep, tq, tk, keys_per_var, n_var, n_split):
    rows = rep * tq
    qi = pl.program_id(1)
    q0 = qi * tq
    t1 = q0 + lax.broadcasted_iota(jnp.int32, (tq, 1), 0)
    q = jnp.concatenate([q_ref[:, r * HEAD_DIM:(r + 1) * HEAD_DIM] for r in range(rep)], axis=0)
    for var in range(n_var):
        sb = sb_ref[0][:, var * LANES:(var + 1) * LANES]
        qa_s[var] = jnp.concatenate([q, jnp.concatenate([sb] * rep, axis=0)], axis=1)
    m_s[...] = jnp.full(m_s.shape, NEG, F32)
    acc_s[...] = jnp.zeros(acc_s.shape, F32)
    t = jnp.concatenate([t1] * rep, axis=0)
    ones = jnp.ones((tk, HEAD_DIM), BF16)

    chunk = rows // n_split

    def scores(kt, c):
        var = (kt * tk) // keys_per_var if n_var > 1 else 0
        return jnp.dot(qa_s[var, c * chunk:(c + 1) * chunk, :], kt_ref[0, kt], preferred_element_type=F32)

    def step(kt, causal, prefetch):
        start = pl.multiple_of(kt * tk, tk)
        v_aug = jnp.concatenate([v_ref[pl.ds(start, tk), :], ones], axis=1)
        for c in range(n_split):
            rs = slice(c * chunk, (c + 1) * chunk)
            s = s_s[rs, :]
            if prefetch:
                s_s[rs, :] = scores(kt + 1, c)
            if causal:
                pos = kt * tk + lax.broadcasted_iota(jnp.int32, (1, tk), 1)
                s = jnp.where(pos <= t[rs], s, NEG)
            m_old = m_s[rs, :]
            m_new = jnp.maximum(m_old, jnp.max(s, axis=-1, keepdims=True))
            alpha = jnp.exp2(m_old - m_new)
            p = jnp.concatenate([jnp.exp2(s[:, b * LANES:(b + 1) * LANES] - m_new).astype(BF16)
                                 for b in range(tk // LANES)], axis=1)
            pv = jnp.dot(p, v_aug, preferred_element_type=F32)
            acc_s[rs, :] = jnp.concatenate([alpha, alpha], axis=1) * acc_s[rs, :] + pv
            m_s[rs, :] = m_new

    n_full = q0 // tk
    n_diag = max(1, tq // tk)
    for c in range(n_split):
        s_s[c * chunk:(c + 1) * chunk, :] = scores(0, c)

    def full_step(kt, carry):
        step(kt, False, True)
        return carry

    lax.fori_loop(0, n_full, full_step, 0)
    for d in range(n_diag):
        step(n_full + d, True, d < n_diag - 1)
    acc = acc_s[...]
    o = acc[:, :HEAD_DIM] / acc[:, HEAD_DIM:]
    for r in range(rep):
        o_ref[:, r * HEAD_DIM:(r + 1) * HEAD_DIM] = o[r * tq:(r + 1) * tq].astype(o_ref.dtype)


def _flash(q, bias, kt_aug, v, *, rep, tq, keys_per_var, out_dtype, name, n_split):
    s_len = q.shape[0]
    g, n_kt, kdim, tk = kt_aug.shape
    gw = rep * HEAD_DIM
    rows = rep * tq
    n_var = bias.shape[-1] // LANES
    assert (tk % tq == 0 or tq % tk == 0) and rows % n_split == 0, (tq, tk, rows, n_split)
    return pl.pallas_call(
        functools.partial(_flash_body, rep=rep, tq=tq, tk=tk, keys_per_var=keys_per_var,
                          n_var=n_var, n_split=n_split),
        grid=(g, s_len // tq),
        in_specs=[pl.BlockSpec((tq, gw), lambda gi, qi: (qi, gi)),
                  pl.BlockSpec((1, tq, n_var * LANES), lambda gi, qi: (gi, qi, 0)),
                  pl.BlockSpec((1, n_kt, kdim, tk), lambda gi, qi: (gi, 0, 0, 0)),
                  pl.BlockSpec((s_len, HEAD_DIM), lambda gi, qi: (0, gi))],
        out_specs=pl.BlockSpec((tq, gw), lambda gi, qi: (qi, gi)),
        out_shape=jax.ShapeDtypeStruct((s_len, g * gw), out_dtype),
        scratch_shapes=[pltpu.VMEM((n_var, rows, 2 * HEAD_DIM), BF16),
                        pltpu.VMEM((rows, LANES), F32),
                        pltpu.VMEM((rows, 2 * HEAD_DIM), F32),
                        pltpu.VMEM((rows, tk), F32)],
        compiler_params=_cparams(("parallel", "arbitrary")),
        name=name,
    )(q, bias, kt_aug, v)


def _moba_select_body(q_ref, km_ref, bt_ref, *, tq, n_blk):
    q0 = pl.program_id(1) * tq
    nt = (((1,), (1,)), ((), ()))
    sg = lax.dot_general(km_ref[0], q_ref[...], nt, preferred_element_type=F32)
    j = lax.broadcasted_iota(jnp.int32, sg.shape, 0)
    cur = (q0 + lax.broadcasted_iota(jnp.int32, (1, tq), 1)) // MOBA_BLOCK
    past = j < cur
    score = jnp.where(j < n_blk, jnp.where(past, sg, NEG), BELOW_NEG)
    sel = _topk_mask(score, j.astype(F32), min(MOBA_TOPK, n_blk), 0)
    bt_ref[0] = jnp.where((sel & past) | (j == cur), 0.0, NEG).astype(BF16)


def _moba_select(q, k_mean, n_blk, tq=512):
    s_len = q.shape[0]
    tq = min(tq, s_len)
    n_heads = k_mean.shape[0]
    return pl.pallas_call(
        functools.partial(_moba_select_body, tq=tq, n_blk=n_blk),
        grid=(n_heads, s_len // tq),
        in_specs=[pl.BlockSpec((tq, HEAD_DIM), lambda hi, qi: (qi, hi)),
                  pl.BlockSpec((1, LANES, HEAD_DIM), lambda hi, qi: (hi, 0, 0))],
        out_specs=pl.BlockSpec((1, LANES, tq), lambda hi, qi: (hi, 0, qi)),
        out_shape=jax.ShapeDtypeStruct((n_heads, LANES, s_len), BF16),
        compiler_params=_cparams(("parallel", "parallel")),
        name="moba_select",
    )(q, k_mean)


def _window_body(q_ref, kt_ref, v_ref, o_ref, *, tq):
    qi = pl.program_id(1)
    q0 = qi * tq
    n_past = WINDOW // tq
    q = jnp.concatenate([q_ref[:, r * HEAD_DIM:(r + 1) * HEAD_DIM] for r in range(NSA_REP)], axis=0)
    t1 = q0 + lax.broadcasted_iota(jnp.int32, (tq, 1), 0)
    t = jnp.concatenate([t1] * NSA_REP, axis=0)
    lane = lax.broadcasted_iota(jnp.int32, (1, tq), 1)
    ones = jnp.ones((tq, HEAD_DIM), BF16)
    scores, tiles = [], []
    for i in range(n_past + 1):
        raw = qi - n_past + i
        idx = jnp.maximum(raw, 0)
        tiles.append(idx)
        s = jnp.dot(q, kt_ref[0, idx], preferred_element_type=F32)
        pos = raw * tq + lane
        if i == 0:
            s = jnp.where(pos > t - WINDOW, s, NEG)
        if i == n_past:
            s = jnp.where(pos <= t, s, NEG)
        else:
            s = s + jnp.where(raw >= 0, 0.0, NEG)
        scores.append(s)
    m = jnp.max(scores[0], axis=-1, keepdims=True)
    for s in scores[1:]:
        m = jnp.maximum(m, jnp.max(s, axis=-1, keepdims=True))
    acc = jnp.zeros((NSA_REP * tq, 2 * HEAD_DIM), F32)
    for i, s in enumerate(scores):
        start = pl.multiple_of(tiles[i] * tq, tq)
        v_aug = jnp.concatenate([v_ref[pl.ds(start, tq), :], ones], axis=1)
        acc = acc + jnp.dot(jnp.exp2(s - m).astype(BF16), v_aug, preferred_element_type=F32)
    o = acc[:, :HEAD_DIM] / acc[:, HEAD_DIM:]
    for r in range(NSA_REP):
        o_ref[:, r * HEAD_DIM:(r + 1) * HEAD_DIM] = o[r * tq:(r + 1) * tq]


def _window(q_r, kw_t, v, v_block0):
    s_len = q_r.shape[0]
    g, n_kt, _, tq = kw_t.shape
    assert WINDOW % tq == 0, tq
    gw = NSA_REP * HEAD_DIM
    return pl.pallas_call(
        functools.partial(_window_body, tq=tq),
        grid=(g, s_len // tq),
        in_specs=[pl.BlockSpec((tq, gw), lambda gi, qi: (qi, gi)),
                  pl.BlockSpec((1, n_kt, HEAD_DIM, tq), lambda gi, qi: (gi, 0, 0, 0)),
                  pl.BlockSpec((s_len, HEAD_DIM), lambda gi, qi: (0, v_block0 + gi))],
        out_specs=pl.BlockSpec((tq, gw), lambda gi, qi: (qi, gi)),
        out_shape=jax.ShapeDtypeStruct((s_len, g * gw), F32),
        compiler_params=_cparams(("parallel", "parallel")),
        name="nsa_window",
    )(q_r, kw_t, v)


def _combine_body(oc_ref, os_ref, ow_ref, g_ref, o_ref, *, n_heads):
    g = g_ref[...]
    for hd in range(n_heads):
        sl = slice(hd * HEAD_DIM, (hd + 1) * HEAD_DIM)
        o = (g[:, hd:hd + 1] * oc_ref[:, sl]
             + g[:, n_heads + hd:n_heads + hd + 1] * os_ref[:, sl]
             + g[:, 2 * n_heads + hd:2 * n_heads + hd + 1] * ow_ref[:, sl])
        o_ref[:, sl] = o.astype(o_ref.dtype)


def _combine(o_c, o_s, o_w, gates, tq=512):
    s_len, width = o_c.shape
    tq = min(tq, s_len)
    spec = pl.BlockSpec((tq, width), lambda i: (i, 0))
    return pl.pallas_call(
        functools.partial(_combine_body, n_heads=width // HEAD_DIM),
        grid=(s_len // tq,),
        in_specs=[spec, spec, spec, pl.BlockSpec((tq, gates.shape[1]), lambda i: (i, 0))],
        out_specs=spec,
        out_shape=jax.ShapeDtypeStruct((s_len, width), BF16),
        compiler_params=_cparams(("parallel",)),
        name="nsa_combine",
    )(o_c, o_s, o_w, gates)


def _sgu_body(zu_ref, zv_ref, gain_ref, w_ref, bt_ref, o_ref, *, tm):
    gd = zu_ref.shape[1] // SGU_GROUPS
    row = lax.broadcasted_iota(jnp.int32, (SGU_CHUNK, SGU_CHUNK), 0)
    col = lax.broadcasted_iota(jnp.int32, (SGU_CHUNK, SGU_CHUNK), 1)
    bt = bt_ref[...]
    for g in range(SGU_GROUPS):
        sl = slice(g * gd, (g + 1) * gd)
        v = _gelu(zv_ref[:, sl])
        v = (v * lax.rsqrt(jnp.mean(v * v, axis=-1, keepdims=True) + NORM_EPS) * gain_ref[:, sl]).astype(BF16)
        w = jnp.where(col <= row, w_ref[g], 0.0).astype(BF16)
        for c in range(tm // SGU_CHUNK):
            rs = slice(c * SGU_CHUNK, (c + 1) * SGU_CHUNK)
            mixed = jnp.dot(w, v[rs], preferred_element_type=F32) + bt[:, g:g + 1]
            o_ref[rs, sl] = (_gelu(zu_ref[rs, sl]) * mixed).astype(o_ref.dtype)


def _sgu(z_uv, gain, w_s, b_s, tm=512):
    s_len = z_uv.shape[0]
    width = z_uv.shape[1] // 2
    tm = min(tm, s_len)
    return pl.pallas_call(
        functools.partial(_sgu_body, tm=tm),
        grid=(s_len // tm,),
        in_specs=[pl.BlockSpec((tm, width), lambda i: (i, 0)),
                  pl.BlockSpec((tm, width), lambda i: (i, 1)),
                  pl.BlockSpec((1, width), lambda i: (0, 0)),
                  pl.BlockSpec((SGU_GROUPS, SGU_CHUNK, SGU_CHUNK), lambda i: (0, 0, 0)),
                  pl.BlockSpec((SGU_CHUNK, SGU_GROUPS), lambda i: (0, 0))],
        out_specs=pl.BlockSpec((tm, width), lambda i: (i, 0)),
        out_shape=jax.ShapeDtypeStruct((s_len, width), BF16),
        compiler_params=_cparams(("parallel",)),
        name="sgu",
    )(z_uv, z_uv, gain.reshape(1, width), w_s, b_s.T)


def _merge_body(oa_ref, ob_ref, oc_ref, pa_ref, pb_ref, pc_ref, ga_ref, gb_ref, gc_ref, y_ref):
    y = ga_ref[...] * jnp.dot(oa_ref[...], pa_ref[...], preferred_element_type=F32)
    y = y + gb_ref[...] * jnp.dot(ob_ref[...], pb_ref[...], preferred_element_type=F32)
    y = y + gc_ref[...] * jnp.dot(oc_ref[...], pc_ref[...], preferred_element_type=F32)
    y_ref[...] = y.astype(y_ref.dtype)


def _merge(o_a, o_b, o_c, p_a, p_b, p_c, layer, gm, tm=512, tn=1024):
    s_len = o_a.shape[0]
    d = p_a.shape[2]
    tm, tn = min(tm, s_len), _tile(d, tn)
    nj = d // tn

    def rows(w):
        return pl.BlockSpec((tm, w), lambda i, j: (i, 0))

    def cols(kdim):
        return pl.BlockSpec((None, kdim, tn), lambda i, j: (layer, 0, j))

    def gate(off):
        return pl.BlockSpec((tm, tn), lambda i, j: (i, off * nj + j))

    return pl.pallas_call(
        _merge_body,
        grid=(s_len // tm, nj),
        in_specs=[rows(o_a.shape[1]), rows(o_b.shape[1]), rows(o_c.shape[1]),
                  cols(p_a.shape[1]), cols(p_b.shape[1]), cols(p_c.shape[1]),
                  gate(0), gate(1), gate(2)],
        out_specs=pl.BlockSpec((tm, tn), lambda i, j: (i, j)),
        out_shape=jax.ShapeDtypeStruct((s_len, d), BF16),
        compiler_params=_cparams(("parallel", "parallel")),
        name="gated_merge",
    )(o_a, o_b, o_c, p_a, p_b, p_c, gm, gm, gm)


def _keys_t(k, n_heads, tk, onehot_t=None):
    s_len = k.shape[0]
    kt = k.reshape(s_len // tk, tk, n_heads, HEAD_DIM).transpose(2, 0, 3, 1)
    if onehot_t is not None:
        kt = jnp.concatenate([kt, jnp.broadcast_to(onehot_t[None], (n_heads,) + onehot_t.shape)], axis=2)
    return kt


def _onehot_t(s_len, block, tk):
    key = np.arange(s_len)
    oh = ((key // block) % LANES)[None, :] == np.arange(LANES)[:, None]
    return jnp.asarray(oh.reshape(LANES, s_len // tk, tk).transpose(1, 0, 2), dtype=BF16)


def _overlap_t(n_pad, n_slc_pad):
    i = np.arange(n_pad)[None, :]
    j = np.arange(n_slc_pad)[:, None]
    ov = (i * CMP_STRIDE <= j * SLC_LEN + SLC_LEN - 1) & (i * CMP_STRIDE + CMP_LEN - 1 >= j * SLC_LEN)
    return jnp.asarray(ov, dtype=BF16)


def _layer(x, cos, sin, p, big, layer):
    s_len, d_model = x.shape
    scale = HEAD_DIM ** -0.5 * math.log2(math.e)
    w_in = p["w_in"]
    nsa_w = big["proj_a"].shape[1]
    sgu_w = big["proj_b"].shape[1]
    moba_w = big["proj_c"].shape[1]
    n_heads = nsa_w // HEAD_DIM
    n_groups = n_heads // NSA_REP
    kv_w = n_groups * HEAD_DIM
    moba_heads = moba_w // HEAD_DIM
    sizes = (nsa_w, kv_w, kv_w, kv_w, kv_w, kv_w, kv_w, 3 * n_heads, sgu_w, sgu_w,
             moba_w, moba_w, moba_w, d_model, d_model, d_model)
    offs = np.concatenate([[0], np.cumsum(sizes)])

    def seg(a, b):
        return w_in[:, offs[a]:offs[b]].astype(BF16)

    def tile_gain(gain, reps):
        return jnp.tile(gain, reps).reshape(1, reps * HEAD_DIM)

    h = _rmsnorm(x, p["norm_mix"])
    rope_extras = (cos, sin)
    rope_specs = (_spec_rope, _spec_rope)

    q_c, q_r = _matmul(
        h, seg(0, 1), functools.partial(_ep_q, scale=scale),
        [jax.ShapeDtypeStruct((s_len, nsa_w), BF16)] * 2, [_spec_tile, _spec_tile],
        extras=(tile_gain(p["nsa_q_norm"], n_heads),) + rope_extras,
        extra_specs=(_spec_col,) + rope_specs, ts=HEAVY_EPILOGUE_TS, name="proj_nsa_q")
    (kcvc,) = _matmul(h, seg(1, 3), _ep_cast, [jax.ShapeDtypeStruct((s_len, 2 * kv_w), F32)],
                      [_spec_tile], name="proj_nsa_cmp_kv")
    (kskw,) = _matmul(
        h, jnp.concatenate([seg(3, 4), seg(5, 6)], axis=1), functools.partial(_ep_krot, block_mean=False),
        [jax.ShapeDtypeStruct((s_len, 2 * kv_w), BF16)], [_spec_tile],
        extras=(jnp.concatenate([tile_gain(p["nsa_ks_norm"], n_groups),
                                 tile_gain(p["nsa_kw_norm"], n_groups)], axis=1),) + rope_extras,
        extra_specs=(_spec_col,) + rope_specs, ts=HEAVY_EPILOGUE_TS, name="proj_nsa_k")
    (vsvw,) = _matmul(h, jnp.concatenate([seg(4, 5), seg(6, 7)], axis=1), _ep_cast,
                      [jax.ShapeDtypeStruct((s_len, 2 * kv_w), BF16)], [_spec_tile], name="proj_nsa_v")
    n_gate = 3 * n_heads
    w_gate = jnp.pad(seg(7, 8), ((0, 0), (0, LANES - n_gate)))
    b_gate = jnp.pad(p["nsa_gate_b"], (0, LANES - n_gate)).reshape(1, LANES)
    (gates,) = _matmul(h, w_gate, _ep_sigmoid_bias, [jax.ShapeDtypeStruct((s_len, LANES), F32)],
                       [_spec_tile], extras=(b_gate,), extra_specs=(_spec_col,), name="proj_nsa_gates")

    n_chunk = s_len // CMP_STRIDE
    chunks = kcvc.reshape(n_chunk, CMP_STRIDE, 2, n_groups, HEAD_DIM).transpose(2, 3, 0, 1, 4)
    chunks = chunks.reshape(2, n_groups, n_chunk, CMP_STRIDE * HEAD_DIM)
    kc = _compress(chunks[0], p["phi_pe_k"], p["phi_w1_k"], p["phi_w2_k"], p["nsa_kc_norm"], True)
    vc = _compress(chunks[1], p["phi_pe_v"], p["phi_w1_v"], p["phi_w2_v"], p["nsa_kc_norm"], False)

    n_slc = s_len // SLC_LEN
    n_slc_pad = -(-n_slc // LANES) * LANES
    o_cmp, sel_bias_t = _nsa_cmp(q_c, jnp.swapaxes(kc, 1, 2), vc, _overlap_t(n_chunk, n_slc_pad), n_slc)
    sel_bias = jnp.swapaxes(sel_bias_t, 1, 2)
    tk = min(512, s_len)
    ks_t = _keys_t(kskw[:, :kv_w], n_groups, tk, _onehot_t(s_len, SLC_LEN, tk))
    o_slc = _flash(q_r, sel_bias, ks_t, vsvw, rep=NSA_REP, tq=min(512, s_len),
                   keys_per_var=LANES * SLC_LEN, out_dtype=F32, name="nsa_selected", n_split=8)
    kw_t = _keys_t(kskw[:, kv_w:], n_groups, min(256, s_len))
    o_win = _window(q_r, kw_t, vsvw, n_groups)
    o_a = _combine(o_cmp, o_slc, o_win, gates)

    (z_uv,) = _matmul(h, seg(8, 10), _ep_cast, [jax.ShapeDtypeStruct((s_len, 2 * sgu_w), F32)],
                      [_spec_tile], name="proj_sgu")
    o_b = _sgu(z_uv, p["sgu_norm"], p["sgu_w"], p["sgu_b"])

    (mq,) = _matmul(
        h, seg(10, 11), functools.partial(_ep_qrot, scale=scale),
        [jax.ShapeDtypeStruct((s_len, moba_w), BF16)], [_spec_tile],
        extras=(tile_gain(p["moba_q_norm"], moba_heads),) + rope_extras,
        extra_specs=(_spec_col,) + rope_specs, ts=HEAVY_EPILOGUE_TS, name="proj_moba_q")
    tm_k = min(1024, s_len)
    mk, mk_mean = _matmul(
        h, seg(11, 12), functools.partial(_ep_krot, block_mean=True),
        [jax.ShapeDtypeStruct((s_len, moba_w), BF16),
         jax.ShapeDtypeStruct((s_len // tm_k, tm_k // MOBA_BLOCK, moba_w), F32)],
        [_spec_tile, _spec_blockmean],
        extras=(tile_gain(p["moba_k_norm"], moba_heads),) + rope_extras,
        extra_specs=(_spec_col,) + rope_specs, tm=tm_k, ts=HEAVY_EPILOGUE_TS, name="proj_moba_k")
    (mv,) = _matmul(h, seg(12, 13), _ep_cast, [jax.ShapeDtypeStruct((s_len, moba_w), BF16)],
                    [_spec_tile], name="proj_moba_v")
    n_blk = s_len // MOBA_BLOCK
    assert n_blk <= LANES, n_blk
    k_mean = mk_mean.reshape(n_blk, moba_heads, HEAD_DIM).transpose(1, 0, 2)
    k_mean = jnp.pad(k_mean, ((0, 0), (0, LANES - n_blk), (0, 0))).astype(BF16)
    moba_bias = jnp.swapaxes(_moba_select(mq, k_mean, n_blk), 1, 2)
    mk_t = _keys_t(mk, moba_heads, tk, _onehot_t(s_len, MOBA_BLOCK, tk))
    o_c = _flash(mq, moba_bias, mk_t, mv, rep=1, tq=min(1024, s_len), keys_per_var=LANES * MOBA_BLOCK,
                 out_dtype=BF16, name="moba", n_split=4)

    (gm,) = _matmul(h, seg(13, 16), _ep_sigmoid, [jax.ShapeDtypeStruct((s_len, 3 * d_model), F32)],
                    [_spec_tile], name="proj_merge_gates")
    y = _merge(o_a, o_b, o_c, big["proj_a"], big["proj_b"], big["proj_c"], layer, gm)
    (x,) = _matmul(y, big["w_out"], _ep_residual, [jax.ShapeDtypeStruct((s_len, d_model), F32)],
                   [_spec_tile], extras=(x,), extra_specs=(_spec_tile,), b_layer=layer, name="out_proj")

    h2 = _rmsnorm(x, p["norm_mlp"])
    (hid,) = _matmul(h2, big["mlp_w1"], _ep_relu2,
                     [jax.ShapeDtypeStruct((s_len, big["mlp_w1"].shape[2]), BF16)], [_spec_tile],
                     b_layer=layer, name="mlp_up")
    (x,) = _matmul(hid, big["mlp_w2"], _ep_residual, [jax.ShapeDtypeStruct((s_len, d_model), F32)],
                   [_spec_tile], extras=(x,), extra_specs=(_spec_tile,), b_layer=layer, name="mlp_down")
    return x


_LAYER_PARAMS = ("norm_mix", "norm_mlp", "w_in", "nsa_gate_b", "nsa_q_norm", "nsa_kc_norm", "nsa_ks_norm",
                 "nsa_kw_norm", "phi_pe_k", "phi_w1_k", "phi_w2_k", "phi_pe_v", "phi_w1_v", "phi_w2_v",
                 "sgu_norm", "sgu_w", "sgu_b", "moba_q_norm", "moba_k_norm", "proj_a", "proj_b", "proj_c",
                 "w_out", "mlp_w1", "mlp_w2")
_BIG_PARAMS = ("proj_a", "proj_b", "proj_c", "w_out", "mlp_w1", "mlp_w2")


def kernel(x, positions, norm_mix, norm_mlp, w_in, nsa_gate_b, nsa_q_norm, nsa_kc_norm, nsa_ks_norm, nsa_kw_norm, phi_pe_k, phi_w1_k, phi_w2_k, phi_pe_v, phi_w1_v, phi_w2_v, sgu_norm, sgu_w, sgu_b, moba_q_norm, moba_k_norm, proj_a, proj_b, proj_c, w_out, mlp_w1, mlp_w2):
    stacked = dict(zip(_LAYER_PARAMS, (norm_mix, norm_mlp, w_in, nsa_gate_b, nsa_q_norm, nsa_kc_norm,
                                       nsa_ks_norm, nsa_kw_norm, phi_pe_k, phi_w1_k, phi_w2_k, phi_pe_v,
                                       phi_w1_v, phi_w2_v, sgu_norm, sgu_w, sgu_b, moba_q_norm, moba_k_norm,
                                       proj_a, proj_b, proj_c, w_out, mlp_w1, mlp_w2)))
    depth = w_in.shape[0]
    big = {k: stacked.pop(k).astype(BF16) for k in _BIG_PARAMS}
    inv = ROPE_THETA ** (-jnp.arange(0, HEAD_DIM, 2, dtype=F32) / HEAD_DIM)
    outs = []
    for b in range(x.shape[0]):
        ang = positions[b].astype(F32)[:, None] * inv
        cos = jnp.concatenate([jnp.cos(ang), jnp.cos(ang)], axis=-1)
        sin = jnp.concatenate([-jnp.sin(ang), jnp.sin(ang)], axis=-1)
        xb = x[b]
        for l in range(depth):
            xb = _layer(xb, cos, sin, {k: v[l] for k, v in stacked.items()}, big, l)
        outs.append(xb)
    return jnp.stack(outs)
```

```python
import functools
import math

import jax
import jax.numpy as jnp
import numpy as np
from jax import lax
from jax.experimental import pallas as pl
from jax.experimental.pallas import tpu as pltpu

F32 = jnp.float32
BF16 = jnp.bfloat16

HEAD_DIM = 128
LANES = 128
ROPE_THETA = 10000.0
NORM_EPS = 1e-6
NEG = -1e30
FORCED = 1e9
BELOW_NEG = -3e38

NSA_REP = 4
CMP_LEN = 32
CMP_STRIDE = 16
SLC_LEN = 64
SLC_TOPK = 16
WINDOW = 512
SGU_GROUPS = 8
SGU_CHUNK = 128
MOBA_BLOCK = 256
MOBA_TOPK = 3

LIGHT_EPILOGUE_TS = 512
HEAVY_EPILOGUE_TS = 1024

MIB = 1024 * 1024
VMEM_LIMIT = 52 * MIB


def _cparams(sem, vmem=VMEM_LIMIT):
    return pltpu.CompilerParams(dimension_semantics=sem, vmem_limit_bytes=vmem)


def _tile(n, pref):
    if n <= pref:
        return n
    t = (pref // LANES) * LANES
    while t >= LANES:
        if n % t == 0:
            return t
        t -= LANES
    raise ValueError(f"no 128-multiple tile divides {n}")


def _gelu(x):
    c = math.sqrt(2.0 / math.pi)
    return 0.5 * x * (1.0 + jnp.tanh(c * (x + 0.044715 * (x * x * x))))


def _sigmoid(x):
    return 1.0 / (1.0 + jnp.exp(-x))


def _head_norm(x, gain):
    return x * lax.rsqrt(jnp.mean(x * x, axis=-1, keepdims=True) + NORM_EPS) * gain


def _rope(x, cos, sin_signed):
    return x * cos + pltpu.roll(x, HEAD_DIM // 2, 1) * sin_signed


def _rmsnorm_body(x_ref, g_ref, o_ref):
    x = x_ref[...]
    y = x * lax.rsqrt(jnp.mean(x * x, axis=-1, keepdims=True) + NORM_EPS)
    o_ref[...] = (y * g_ref[...]).astype(o_ref.dtype)


def _rmsnorm(x, gain, tm=256):
    m, d = x.shape
    tm = min(tm, m)
    return pl.pallas_call(
        _rmsnorm_body,
        grid=(m // tm,),
        in_specs=[pl.BlockSpec((tm, d), lambda i: (i, 0)),
                  pl.BlockSpec((1, d), lambda i: (0, 0))],
        out_specs=pl.BlockSpec((tm, d), lambda i: (i, 0)),
        out_shape=jax.ShapeDtypeStruct((m, d), BF16),
        compiler_params=_cparams(("parallel",)),
        name="rmsnorm",
    )(x, gain.reshape(1, d))


def _mm_body(*refs, n_extra, n_out, nk, tn, ts, epilogue):
    a_ref, b_ref = refs[0], refs[1]
    extra = refs[2:2 + n_extra]
    outs = refs[2 + n_extra:2 + n_extra + n_out]
    acc_ref = refs[-1] if nk > 1 else None

    def finish():
        for c0 in range(0, tn, ts):
            part = jnp.dot(a_ref[...], b_ref[:, c0:c0 + ts], preferred_element_type=F32)
            if nk > 1:
                part = part + acc_ref[:, c0:c0 + ts]
            epilogue(part, extra, outs, slice(c0, c0 + ts))

    if nk == 1:
        finish()
        return
    k = pl.program_id(2)
    split_finish = ts < tn

    @pl.when(k == 0)
    def _():
        acc_ref[...] = jnp.dot(a_ref[...], b_ref[...], preferred_element_type=F32)

    @pl.when((k > 0) & (k < nk - 1) if split_finish else k > 0)
    def _():
        acc_ref[...] += jnp.dot(a_ref[...], b_ref[...], preferred_element_type=F32)

    @pl.when(k == nk - 1)
    def _():
        if split_finish:
            finish()
        else:
            epilogue(acc_ref[...], extra, outs, slice(0, tn))


def _matmul(a, b, epilogue, out_shapes, out_specs, extras=(), extra_specs=(),
            tm=1024, tn=1024, tk=2048, ts=LIGHT_EPILOGUE_TS, b_layer=0, name="matmul"):
    m, kdim = a.shape
    n = b.shape[-1]
    tm, tn, tk = min(tm, m), _tile(n, tn), _tile(kdim, tk)
    ts = min(ts, tn)
    if b.ndim == 3:
        b_spec = pl.BlockSpec((None, tk, tn), lambda i, j, k: (b_layer, k, j))
    else:
        b_spec = pl.BlockSpec((tk, tn), lambda i, j, k: (k, j))
    nk = kdim // tk
    body = functools.partial(_mm_body, n_extra=len(extras), n_out=len(out_shapes), nk=nk, tn=tn, ts=ts,
                             epilogue=epilogue)
    scratch = [] if nk == 1 else [pltpu.VMEM((tm, tn), F32)]
    return pl.pallas_call(
        body,
        grid=(m // tm, n // tn, nk),
        in_specs=[pl.BlockSpec((tm, tk), lambda i, j, k: (i, k)),
                  b_spec]
                 + [s(tm, tn) for s in extra_specs],
        out_specs=[s(tm, tn) for s in out_specs],
        out_shape=out_shapes,
        scratch_shapes=scratch,
        compiler_params=_cparams(("parallel", "parallel", "arbitrary")),
        name=name,
    )(a, b, *extras)


def _spec_tile(tm, tn):
    return pl.BlockSpec((tm, tn), lambda i, j, k: (i, j))


def _spec_col(tm, tn):
    return pl.BlockSpec((1, tn), lambda i, j, k: (0, j))


def _spec_rope(tm, tn):
    return pl.BlockSpec((tm, HEAD_DIM), lambda i, j, k: (i, 0))


def _spec_blockmean(tm, tn):
    return pl.BlockSpec((1, tm // MOBA_BLOCK, tn), lambda i, j, k: (i, 0, j))


def _ep_cast(acc, extra, outs, cols):
    outs[0][:, cols] = acc.astype(outs[0].dtype)


def _ep_sigmoid_bias(acc, extra, outs, cols):
    outs[0][:, cols] = _sigmoid(acc + extra[0][:, cols])


def _ep_sigmoid(acc, extra, outs, cols):
    outs[0][:, cols] = _sigmoid(acc)


def _ep_relu2(acc, extra, outs, cols):
    r = jnp.maximum(acc, 0.0)
    outs[0][:, cols] = (r * r).astype(outs[0].dtype)


def _ep_residual(acc, extra, outs, cols):
    outs[0][:, cols] = extra[0][:, cols] + acc


def _heads(cols):
    return [(slice(c - cols.start, c - cols.start + HEAD_DIM), slice(c, c + HEAD_DIM))
            for c in range(cols.start, cols.stop, HEAD_DIM)]


def _ep_q(acc, extra, outs, cols, *, scale):
    cos, sin = extra[1][...], extra[2][...]
    for a_sl, t_sl in _heads(cols):
        y = _head_norm(acc[:, a_sl], extra[0][:, t_sl])
        outs[0][:, t_sl] = (y * scale).astype(BF16)
        outs[1][:, t_sl] = (_rope(y, cos, sin) * scale).astype(BF16)


def _ep_qrot(acc, extra, outs, cols, *, scale):
    cos, sin = extra[1][...], extra[2][...]
    for a_sl, t_sl in _heads(cols):
        y = _head_norm(acc[:, a_sl], extra[0][:, t_sl])
        outs[0][:, t_sl] = (_rope(y, cos, sin) * scale).astype(BF16)


def _ep_krot(acc, extra, outs, cols, *, block_mean):
    cos, sin = extra[1][...], extra[2][...]
    for a_sl, t_sl in _heads(cols):
        y = _rope(_head_norm(acc[:, a_sl], extra[0][:, t_sl]), cos, sin)
        outs[0][:, t_sl] = y.astype(BF16)
        if block_mean:
            for blk in range(acc.shape[0] // MOBA_BLOCK):
                rows = y[blk * MOBA_BLOCK:(blk + 1) * MOBA_BLOCK]
                outs[1][0, blk:blk + 1, t_sl] = jnp.mean(rows, axis=0, keepdims=True)


def _compress_body(a_ref, pe_ref, w1_ref, w2_ref, g_ref, o_ref, *, norm):
    half = CMP_STRIDE * HEAD_DIM
    a = a_ref[0]
    n_chunk = a.shape[0]
    x1 = (a + pe_ref[:, :half]).astype(BF16)
    x2 = (a + pe_ref[:, half:]).astype(BF16)
    p1 = jnp.dot(x1, w1_ref[:half, :], preferred_element_type=F32)
    p2 = jnp.dot(x2, w1_ref[half:, :], preferred_element_type=F32)
    h = _gelu(p1 + pltpu.roll(p2, n_chunk - 1, 0))
    o = jnp.dot(h.astype(BF16), w2_ref[...], preferred_element_type=F32)
    if norm:
        o = _head_norm(o, g_ref[...])
    o_ref[0] = o.astype(o_ref.dtype)


def _compress(a, pe, w1, w2, gain, norm):
    g, n_chunk, width = a.shape
    hidden = w1.shape[1]
    return pl.pallas_call(
        functools.partial(_compress_body, norm=norm),
        grid=(g,),
        in_specs=[pl.BlockSpec((1, n_chunk, width), lambda i: (i, 0, 0)),
                  pl.BlockSpec((1, 2 * width), lambda i: (0, 0)),
                  pl.BlockSpec((2 * width, hidden), lambda i: (0, 0)),
                  pl.BlockSpec((hidden, HEAD_DIM), lambda i: (0, 0)),
                  pl.BlockSpec((1, HEAD_DIM), lambda i: (0, 0))],
        out_specs=pl.BlockSpec((1, n_chunk, HEAD_DIM), lambda i: (i, 0, 0)),
        out_shape=jax.ShapeDtypeStruct((g, n_chunk, HEAD_DIM), BF16),
        compiler_params=_cparams(("parallel",)),
        name="nsa_compress",
    )(a, pe.reshape(1, 2 * width), w1.astype(BF16), w2.astype(BF16), gain.reshape(1, HEAD_DIM))


def _topk_mask(score, index_f, k, axis):
    n = score.shape[axis]
    sel = jnp.zeros(score.shape, dtype=jnp.bool_)
    for _ in range(k):
        m = jnp.max(score, axis=axis, keepdims=True)
        first = jnp.min(jnp.where(score == m, index_f, float(n)), axis=axis, keepdims=True)
        hit = index_f == first
        sel = jnp.logical_or(sel, hit)
        score = jnp.where(hit, BELOW_NEG, score)
    return sel


def _nsa_cmp_body(q_ref, kt_ref, v_ref, ovt_ref, o_ref, sbt_ref, *, tq, n_slc, cw):
    qi = pl.program_id(1)
    q0 = qi * tq
    n_pad = kt_ref.shape[-1]
    q = jnp.concatenate([q_ref[:, r * HEAD_DIM:(r + 1) * HEAD_DIM] for r in range(NSA_REP)], axis=0)
    t1 = q0 + lax.broadcasted_iota(jnp.int32, (tq, 1), 0)
    t = jnp.concatenate([t1] * NSA_REP, axis=0)

    def attend(width):
        s = jnp.dot(q, kt_ref[0, :, :width], preferred_element_type=F32)
        cmp_end = lax.broadcasted_iota(jnp.int32, (1, width), 1) * CMP_STRIDE + (CMP_LEN - 1)
        s = jnp.where(cmp_end <= t, s, NEG)
        m = jnp.max(s, axis=-1, keepdims=True)
        e = jnp.exp2(s - m)
        inv = jnp.where(m > 0.5 * NEG, 1.0 / jnp.sum(e, axis=-1, keepdims=True), 0.0)
        p = e * inv
        o = jnp.dot(p.astype(BF16), v_ref[0, :width, :], preferred_element_type=F32)
        for r in range(NSA_REP):
            o_ref[:, r * HEAD_DIM:(r + 1) * HEAD_DIM] = o[r * tq:(r + 1) * tq]
        ps = p[0:tq]
        for r in range(1, NSA_REP):
            ps = ps + p[r * tq:(r + 1) * tq]
        ps_hi = ps.astype(BF16)
        ps_lo = (ps - ps_hi.astype(F32)).astype(BF16)
        nt = (((1,), (1,)), ((), ()))
        n_rows = width * CMP_STRIDE // SLC_LEN
        ovt = ovt_ref[:n_rows, :width]
        imp = (lax.dot_general(ovt, ps_hi, nt, preferred_element_type=F32)
               + lax.dot_general(ovt, ps_lo, nt, preferred_element_type=F32))
        j = lax.broadcasted_iota(jnp.int32, imp.shape, 0)
        cur = (q0 + lax.broadcasted_iota(jnp.int32, (1, tq), 1)) // SLC_LEN
        forced = (j == 0) | (j == cur) | (j == cur - 1)
        allowed = j <= cur
        score = jnp.where(allowed & jnp.logical_not(forced), imp, NEG)
        score = jnp.where(j < n_slc, score, BELOW_NEG)
        sel = _topk_mask(score, j.astype(F32), min(SLC_TOPK, n_slc) - 3, 0)
        sbt_ref[0, :n_rows, :] = jnp.where((sel | forced) & allowed, 0.0, NEG).astype(BF16)
        if n_rows < sbt_ref.shape[1]:
            sbt_ref[0, n_rows:, :] = jnp.full((sbt_ref.shape[1] - n_rows, tq), NEG, BF16)

    n_chunks = ((q0 + tq) // CMP_STRIDE - 1 + cw - 1) // cw
    for v in range(1, n_pad // cw + 1):
        pl.when(n_chunks == v)(functools.partial(attend, v * cw))


def _nsa_cmp(q_c, kc_t, vc, overlap_t, n_slc, tq=256):
    s_len = q_c.shape[0]
    tq = min(tq, s_len)
    g, _, n_pad = kc_t.shape
    n_slc_pad = overlap_t.shape[0]
    gw = NSA_REP * HEAD_DIM
    cw = min(256, n_pad)
    assert n_pad % cw == 0, (n_pad, cw)
    return pl.pallas_call(
        functools.partial(_nsa_cmp_body, tq=tq, n_slc=n_slc, cw=cw),
        grid=(g, s_len // tq),
        in_specs=[pl.BlockSpec((tq, gw), lambda gi, qi: (qi, gi)),
                  pl.BlockSpec((1, HEAD_DIM, n_pad), lambda gi, qi: (gi, 0, 0)),
                  pl.BlockSpec((1, n_pad, HEAD_DIM), lambda gi, qi: (gi, 0, 0)),
                  pl.BlockSpec((n_slc_pad, n_pad), lambda gi, qi: (0, 0))],
        out_specs=[pl.BlockSpec((tq, gw), lambda gi, qi: (qi, gi)),
                   pl.BlockSpec((1, n_slc_pad, tq), lambda gi, qi: (gi, 0, qi))],
        out_shape=[jax.ShapeDtypeStruct((s_len, g * gw), F32),
                   jax.ShapeDtypeStruct((g, n_slc_pad, s_len), BF16)],
        compiler_params=_cparams(("parallel", "parallel")),
        name="nsa_cmp_select",
    )(q_c, kc_t, vc, overlap_t)


def _flash_body(q_ref, sb_ref, kt_ref, v_ref, o_ref, qa_s, m_s, acc_s, s_s, *,
                rep, tq, tk, keys_per_var, n_var, n_split):
    rows = rep * tq
    qi = pl.program_id(1)
    q0 = qi * tq
    t1 = q0 + lax.broadcasted_iota(jnp.int32, (tq, 1), 0)
    q = jnp.concatenate([q_ref[:, r * HEAD_DIM:(r + 1) * HEAD_DIM] for r in range(rep)], axis=0)
    for var in range(n_var):
        sb = sb_ref[0][:, var * LANES:(var + 1) * LANES]
        qa_s[var] = jnp.concatenate([q, jnp.concatenate([sb] * rep, axis=0)], axis=1)
    m_s[...] = jnp.full(m_s.shape, NEG, F32)
    acc_s[...] = jnp.zeros(acc_s.shape, F32)
    t = jnp.concatenate([t1] * rep, axis=0)
    ones = jnp.ones((tk, HEAD_DIM), BF16)

    chunk = rows // n_split

    def scores(kt, c):
        var = (kt * tk) // keys_per_var if n_var > 1 else 0
        return jnp.dot(qa_s[var, c * chunk:(c + 1) * chunk, :], kt_ref[0, kt], preferred_element_type=F32)

    def step(kt, causal, prefetch):
        start = pl.multiple_of(kt * tk, tk)
        v_aug = jnp.concatenate([v_ref[pl.ds(start, tk), :], ones], axis=1)
        for c in range(n_split):
            rs = slice(c * chunk, (c + 1) * chunk)
            s = s_s[rs, :]
            if prefetch:
                s_s[rs, :] = scores(kt + 1, c)
            if causal:
                pos = kt * tk + lax.broadcasted_iota(jnp.int32, (1, tk), 1)
                s = jnp.where(pos <= t[rs], s, NEG)
            m_old = m_s[rs, :]
            m_new = jnp.maximum(m_old, jnp.max(s, axis=-1, keepdims=True))
            alpha = jnp.exp2(m_old - m_new)
            p = jnp.concatenate([jnp.exp2(s[:, b * LANES:(b + 1) * LANES] - m_new).astype(BF16)
                                 for b in range(tk // LANES)], axis=1)
            pv = jnp.dot(p, v_aug, preferred_element_type=F32)
            acc_s[rs, :] = jnp.concatenate([alpha, alpha], axis=1) * acc_s[rs, :] + pv
            m_s[rs, :] = m_new

    n_full = q0 // tk
    n_diag = max(1, tq // tk)
    for c in range(n_split):
        s_s[c * chunk:(c + 1) * chunk, :] = scores(0, c)

    def full_step(kt, carry):
        step(kt, False, True)
        return carry

    lax.fori_loop(0, n_full, full_step, 0)
    for d in range(n_diag):
        step(n_full + d, True, d < n_diag - 1)
    acc = acc_s[...]
    o = acc[:, :HEAD_DIM] / acc[:, HEAD_DIM:]
    for r in range(rep):
        o_ref[:, r * HEAD_DIM:(r + 1) * HEAD_DIM] = o[r * tq:(r + 1) * tq].astype(o_ref.dtype)


def _flash(q, bias, kt_aug, v, *, rep, tq, keys_per_var, out_dtype, name, n_split):
    s_len = q.shape[0]
    g, n_kt, kdim, tk = kt_aug.shape
    gw = rep * HEAD_DIM
    rows = rep * tq
    n_var = bias.shape[-1] // LANES
    assert (tk % tq == 0 or tq % tk == 0) and rows % n_split == 0, (tq, tk, rows, n_split)
    return pl.pallas_call(
        functools.partial(_flash_body, rep=rep, tq=tq, tk=tk, keys_per_var=keys_per_var,
                          n_var=n_var, n_split=n_split),
        grid=(g, s_len // tq),
        in_specs=[pl.BlockSpec((tq, gw), lambda gi, qi: (qi, gi)),
                  pl.BlockSpec((1, tq, n_var * LANES), lambda gi, qi: (gi, qi, 0)),
                  pl.BlockSpec((1, n_kt, kdim, tk), lambda gi, qi: (gi, 0, 0, 0)),
                  pl.BlockSpec((s_len, HEAD_DIM), lambda gi, qi: (0, gi))],
        out_specs=pl.BlockSpec((tq, gw), lambda gi, qi: (qi, gi)),
        out_shape=jax.ShapeDtypeStruct((s_len, g * gw), out_dtype),
        scratch_shapes=[pltpu.VMEM((n_var, rows, 2 * HEAD_DIM), BF16),
                        pltpu.VMEM((rows, LANES), F32),
                        pltpu.VMEM((rows, 2 * HEAD_DIM), F32),
                        pltpu.VMEM((rows, tk), F32)],
        compiler_params=_cparams(("parallel", "arbitrary")),
        name=name,
    )(q, bias, kt_aug, v)


def _moba_select_body(q_ref, km_ref, bt_ref, *, tq, n_blk):
    q0 = pl.program_id(1) * tq
    nt = (((1,), (1,)), ((), ()))
    sg = lax.dot_general(km_ref[0], q_ref[...], nt, preferred_element_type=F32)
    j = lax.broadcasted_iota(jnp.int32, sg.shape, 0)
    cur = (q0 + lax.broadcasted_iota(jnp.int32, (1, tq), 1)) // MOBA_BLOCK
    past = j < cur
    score = jnp.where(j < n_blk, jnp.where(past, sg, NEG), BELOW_NEG)
    sel = _topk_mask(score, j.astype(F32), min(MOBA_TOPK, n_blk), 0)
    bt_ref[0] = jnp.where((sel & past) | (j == cur), 0.0, NEG).astype(BF16)


def _moba_select(q, k_mean, n_blk, tq=512):
    s_len = q.shape[0]
    tq = min(tq, s_len)
    n_heads = k_mean.shape[0]
    return pl.pallas_call(
        functools.partial(_moba_select_body, tq=tq, n_blk=n_blk),
        grid=(n_heads, s_len // tq),
        in_specs=[pl.BlockSpec((tq, HEAD_DIM), lambda hi, qi: (qi, hi)),
                  pl.BlockSpec((1, LANES, HEAD_DIM), lambda hi, qi: (hi, 0, 0))],
        out_specs=pl.BlockSpec((1, LANES, tq), lambda hi, qi: (hi, 0, qi)),
        out_shape=jax.ShapeDtypeStruct((n_heads, LANES, s_len), BF16),
        compiler_params=_cparams(("parallel", "parallel")),
        name="moba_select",
    )(q, k_mean)


def _window_body(q_ref, kt_ref, v_ref, oc_ref, os_ref, g_ref, o_ref, *, tq):
    qi = pl.program_id(1)
    q0 = qi * tq
    n_past = WINDOW // tq
    q = jnp.concatenate([q_ref[:, r * HEAD_DIM:(r + 1) * HEAD_DIM] for r in range(NSA_REP)], axis=0)
    t1 = q0 + lax.broadcasted_iota(jnp.int32, (tq, 1), 0)
    t = jnp.concatenate([t1] * NSA_REP, axis=0)
    lane = lax.broadcasted_iota(jnp.int32, (1, tq), 1)
    ones = jnp.ones((tq, HEAD_DIM), BF16)
    scores, tiles = [], []
    for i in range(n_past + 1):
        raw = qi - n_past + i
        idx = jnp.maximum(raw, 0)
        tiles.append(idx)
        s = jnp.dot(q, kt_ref[0, idx], preferred_element_type=F32)
        pos = raw * tq + lane
        if i == 0:
            s = jnp.where(pos > t - WINDOW, s, NEG)
        if i == n_past:
            s = jnp.where(pos <= t, s, NEG)
        else:
            s = s + jnp.where(raw >= 0, 0.0, NEG)
        scores.append(s)
    m = jnp.max(scores[0], axis=-1, keepdims=True)
    for s in scores[1:]:
        m = jnp.maximum(m, jnp.max(s, axis=-1, keepdims=True))
    acc = jnp.zeros((NSA_REP * tq, 2 * HEAD_DIM), F32)
    for i, s in enumerate(scores):
        start = pl.multiple_of(tiles[i] * tq, tq)
        v_aug = jnp.concatenate([v_ref[pl.ds(start, tq), :], ones], axis=1)
        acc = acc + jnp.dot(jnp.exp2(s - m).astype(BF16), v_aug, preferred_element_type=F32)
    o_win = acc[:, :HEAD_DIM] / acc[:, HEAD_DIM:]
    gate = g_ref[0]
    for r in range(NSA_REP):
        sl = slice(r * HEAD_DIM, (r + 1) * HEAD_DIM)
        o = (gate[:, r:r + 1] * oc_ref[:, sl]
             + gate[:, NSA_REP + r:NSA_REP + r + 1] * os_ref[:, sl]
             + gate[:, 2 * NSA_REP + r:2 * NSA_REP + r + 1] * o_win[r * tq:(r + 1) * tq])
        o_ref[:, sl] = o.astype(o_ref.dtype)


def _window_combine(q_r, kw_t, v, v_block0, o_cmp, o_slc, gates):
    s_len = q_r.shape[0]
    g, n_kt, _, tq = kw_t.shape
    assert WINDOW % tq == 0, tq
    gw = NSA_REP * HEAD_DIM
    tile = pl.BlockSpec((tq, gw), lambda gi, qi: (qi, gi))
    return pl.pallas_call(
        functools.partial(_window_body, tq=tq),
        grid=(g, s_len // tq),
        in_specs=[tile,
                  pl.BlockSpec((1, n_kt, HEAD_DIM, tq), lambda gi, qi: (gi, 0, 0, 0)),
                  pl.BlockSpec((s_len, HEAD_DIM), lambda gi, qi: (0, v_block0 + gi)),
                  tile, tile,
                  pl.BlockSpec((1, tq, LANES), lambda gi, qi: (gi, qi, 0))],
        out_specs=tile,
        out_shape=jax.ShapeDtypeStruct((s_len, g * gw), BF16),
        compiler_params=_cparams(("parallel", "parallel")),
        name="nsa_window_combine",
    )(q_r, kw_t, v, o_cmp, o_slc, gates)


def _sgu_body(zu_ref, zv_ref, gain_ref, w_ref, bt_ref, o_ref, *, tm):
    gd = zu_ref.shape[1] // SGU_GROUPS
    row = lax.broadcasted_iota(jnp.int32, (SGU_CHUNK, SGU_CHUNK), 0)
    col = lax.broadcasted_iota(jnp.int32, (SGU_CHUNK, SGU_CHUNK), 1)
    bt = bt_ref[...]
    for g in range(SGU_GROUPS):
        sl = slice(g * gd, (g + 1) * gd)
        v = _gelu(zv_ref[:, sl])
        v = (v * lax.rsqrt(jnp.mean(v * v, axis=-1, keepdims=True) + NORM_EPS) * gain_ref[:, sl]).astype(BF16)
        w = jnp.where(col <= row, w_ref[g], 0.0).astype(BF16)
        for c in range(tm // SGU_CHUNK):
            rs = slice(c * SGU_CHUNK, (c + 1) * SGU_CHUNK)
            mixed = jnp.dot(w, v[rs], preferred_element_type=F32) + bt[:, g:g + 1]
            o_ref[rs, sl] = (_gelu(zu_ref[rs, sl]) * mixed).astype(o_ref.dtype)


def _sgu(z_uv, gain, w_s, b_s, tm=512):
    s_len = z_uv.shape[0]
    width = z_uv.shape[1] // 2
    tm = min(tm, s_len)
    return pl.pallas_call(
        functools.partial(_sgu_body, tm=tm),
        grid=(s_len // tm,),
        in_specs=[pl.BlockSpec((tm, width), lambda i: (i, 0)),
                  pl.BlockSpec((tm, width), lambda i: (i, 1)),
                  pl.BlockSpec((1, width), lambda i: (0, 0)),
                  pl.BlockSpec((SGU_GROUPS, SGU_CHUNK, SGU_CHUNK), lambda i: (0, 0, 0)),
                  pl.BlockSpec((SGU_CHUNK, SGU_GROUPS), lambda i: (0, 0))],
        out_specs=pl.BlockSpec((tm, width), lambda i: (i, 0)),
        out_shape=jax.ShapeDtypeStruct((s_len, width), BF16),
        compiler_params=_cparams(("parallel",)),
        name="sgu",
    )(z_uv, z_uv, gain.reshape(1, width), w_s, b_s.T)


def _merge_body(oa_ref, ob_ref, oc_ref, pa_ref, pb_ref, pc_ref, ga_ref, gb_ref, gc_ref, y_ref):
    y = ga_ref[...] * jnp.dot(oa_ref[...], pa_ref[...], preferred_element_type=F32)
    y = y + gb_ref[...] * jnp.dot(ob_ref[...], pb_ref[...], preferred_element_type=F32)
    y = y + gc_ref[...] * jnp.dot(oc_ref[...], pc_ref[...], preferred_element_type=F32)
    y_ref[...] = y.astype(y_ref.dtype)


def _merge(o_a, o_b, o_c, p_a, p_b, p_c, layer, gm, tm=512, tn=1024):
    s_len = o_a.shape[0]
    d = p_a.shape[2]
    tm, tn = min(tm, s_len), _tile(d, tn)
    nj = d // tn

    def rows(w):
        return pl.BlockSpec((tm, w), lambda i, j: (i, 0))

    def cols(kdim):
        return pl.BlockSpec((None, kdim, tn), lambda i, j: (layer, 0, j))

    def gate(off):
        return pl.BlockSpec((tm, tn), lambda i, j: (i, off * nj + j))

    return pl.pallas_call(
        _merge_body,
        grid=(s_len // tm, nj),
        in_specs=[rows(o_a.shape[1]), rows(o_b.shape[1]), rows(o_c.shape[1]),
                  cols(p_a.shape[1]), cols(p_b.shape[1]), cols(p_c.shape[1]),
                  gate(0), gate(1), gate(2)],
        out_specs=pl.BlockSpec((tm, tn), lambda i, j: (i, j)),
        out_shape=jax.ShapeDtypeStruct((s_len, d), BF16),
        compiler_params=_cparams(("parallel", "parallel")),
        name="gated_merge",
    )(o_a, o_b, o_c, p_a, p_b, p_c, gm, gm, gm)


def _keys_t(k, n_heads, tk, onehot_t=None):
    s_len = k.shape[0]
    kt = k.reshape(s_len // tk, tk, n_heads, HEAD_DIM).transpose(2, 0, 3, 1)
    if onehot_t is not None:
        kt = jnp.concatenate([kt, jnp.broadcast_to(onehot_t[None], (n_heads,) + onehot_t.shape)], axis=2)
    return kt


def _onehot_t(s_len, block, tk):
    key = np.arange(s_len)
    oh = ((key // block) % LANES)[None, :] == np.arange(LANES)[:, None]
    return jnp.asarray(oh.reshape(LANES, s_len // tk, tk).transpose(1, 0, 2), dtype=BF16)


def _overlap_t(n_pad, n_slc_pad):
    i = np.arange(n_pad)[None, :]
    j = np.arange(n_slc_pad)[:, None]
    ov = (i * CMP_STRIDE <= j * SLC_LEN + SLC_LEN - 1) & (i * CMP_STRIDE + CMP_LEN - 1 >= j * SLC_LEN)
    return jnp.asarray(ov, dtype=BF16)


def _layer(x, cos, sin, p, big, layer):
    s_len, d_model = x.shape
    scale = HEAD_DIM ** -0.5 * math.log2(math.e)
    w_in = p["w_in"]
    nsa_w = big["proj_a"].shape[1]
    sgu_w = big["proj_b"].shape[1]
    moba_w = big["proj_c"].shape[1]
    n_heads = nsa_w // HEAD_DIM
    n_groups = n_heads // NSA_REP
    kv_w = n_groups * HEAD_DIM
    moba_heads = moba_w // HEAD_DIM
    sizes = (nsa_w, kv_w, kv_w, kv_w, kv_w, kv_w, kv_w, 3 * n_heads, sgu_w, sgu_w,
             moba_w, moba_w, moba_w, d_model, d_model, d_model)
    offs = np.concatenate([[0], np.cumsum(sizes)])

    def seg(a, b):
        return w_in[:, offs[a]:offs[b]].astype(BF16)

    def tile_gain(gain, reps):
        return jnp.tile(gain, reps).reshape(1, reps * HEAD_DIM)

    h = _rmsnorm(x, p["norm_mix"])
    rope_extras = (cos, sin)
    rope_specs = (_spec_rope, _spec_rope)

    q_c, q_r = _matmul(
        h, seg(0, 1), functools.partial(_ep_q, scale=scale),
        [jax.ShapeDtypeStruct((s_len, nsa_w), BF16)] * 2, [_spec_tile, _spec_tile],
        extras=(tile_gain(p["nsa_q_norm"], n_heads),) + rope_extras,
        extra_specs=(_spec_col,) + rope_specs, ts=HEAVY_EPILOGUE_TS, name="proj_nsa_q")
    (kcvc,) = _matmul(h, seg(1, 3), _ep_cast, [jax.ShapeDtypeStruct((s_len, 2 * kv_w), F32)],
                      [_spec_tile], name="proj_nsa_cmp_kv")
    (kskw,) = _matmul(
        h, jnp.concatenate([seg(3, 4), seg(5, 6)], axis=1), functools.partial(_ep_krot, block_mean=False),
        [jax.ShapeDtypeStruct((s_len, 2 * kv_w), BF16)], [_spec_tile],
        extras=(jnp.concatenate([tile_gain(p["nsa_ks_norm"], n_groups),
                                 tile_gain(p["nsa_kw_norm"], n_groups)], axis=1),) + rope_extras,
        extra_specs=(_spec_col,) + rope_specs, ts=HEAVY_EPILOGUE_TS, name="proj_nsa_k")
    (vsvw,) = _matmul(h, jnp.concatenate([seg(4, 5), seg(6, 7)], axis=1), _ep_cast,
                      [jax.ShapeDtypeStruct((s_len, 2 * kv_w), BF16)], [_spec_tile], name="proj_nsa_v")
    n_gate = 3 * n_heads
    w_gate = jnp.pad(seg(7, 8), ((0, 0), (0, LANES - n_gate)))
    b_gate = jnp.pad(p["nsa_gate_b"], (0, LANES - n_gate)).reshape(1, LANES)
    (gates,) = _matmul(h, w_gate, _ep_sigmoid_bias, [jax.ShapeDtypeStruct((s_len, LANES), F32)],
                       [_spec_tile], extras=(b_gate,), extra_specs=(_spec_col,), name="proj_nsa_gates")

    n_chunk = s_len // CMP_STRIDE
    chunks = kcvc.reshape(n_chunk, CMP_STRIDE, 2, n_groups, HEAD_DIM).transpose(2, 3, 0, 1, 4)
    chunks = chunks.reshape(2, n_groups, n_chunk, CMP_STRIDE * HEAD_DIM)
    kc = _compress(chunks[0], p["phi_pe_k"], p["phi_w1_k"], p["phi_w2_k"], p["nsa_kc_norm"], True)
    vc = _compress(chunks[1], p["phi_pe_v"], p["phi_w1_v"], p["phi_w2_v"], p["nsa_kc_norm"], False)

    n_slc = s_len // SLC_LEN
    n_slc_pad = -(-n_slc // LANES) * LANES
    o_cmp, sel_bias_t = _nsa_cmp(q_c, jnp.swapaxes(kc, 1, 2), vc, _overlap_t(n_chunk, n_slc_pad), n_slc)
    sel_bias = jnp.swapaxes(sel_bias_t, 1, 2)
    tk = min(512, s_len)
    ks_t = _keys_t(kskw[:, :kv_w], n_groups, tk, _onehot_t(s_len, SLC_LEN, tk))
    o_slc = _flash(q_r, sel_bias, ks_t, vsvw, rep=NSA_REP, tq=min(512, s_len),
                   keys_per_var=LANES * SLC_LEN, out_dtype=F32, name="nsa_selected", n_split=8)
    kw_t = _keys_t(kskw[:, kv_w:], n_groups, min(256, s_len))
    group_gates = gates[:, :n_gate].reshape(s_len, 3, n_groups, NSA_REP).transpose(2, 0, 1, 3)
    group_gates = jnp.pad(group_gates.reshape(n_groups, s_len, 3 * NSA_REP),
                          ((0, 0), (0, 0), (0, LANES - 3 * NSA_REP)))
    o_a = _window_combine(q_r, kw_t, vsvw, n_groups, o_cmp, o_slc, group_gates)

    (z_uv,) = _matmul(h, seg(8, 10), _ep_cast, [jax.ShapeDtypeStruct((s_len, 2 * sgu_w), F32)],
                      [_spec_tile], name="proj_sgu")
    o_b = _sgu(z_uv, p["sgu_norm"], p["sgu_w"], p["sgu_b"])

    (mq,) = _matmul(
        h, seg(10, 11), functools.partial(_ep_qrot, scale=scale),
        [jax.ShapeDtypeStruct((s_len, moba_w), BF16)], [_spec_tile],
        extras=(tile_gain(p["moba_q_norm"], moba_heads),) + rope_extras,
        extra_specs=(_spec_col,) + rope_specs, ts=HEAVY_EPILOGUE_TS, name="proj_moba_q")
    tm_k = min(1024, s_len)
    mk, mk_mean = _matmul(
        h, seg(11, 12), functools.partial(_ep_krot, block_mean=True),
        [jax.ShapeDtypeStruct((s_len, moba_w), BF16),
         jax.ShapeDtypeStruct((s_len // tm_k, tm_k // MOBA_BLOCK, moba_w), F32)],
        [_spec_tile, _spec_blockmean],
        extras=(tile_gain(p["moba_k_norm"], moba_heads),) + rope_extras,
        extra_specs=(_spec_col,) + rope_specs, tm=tm_k, ts=HEAVY_EPILOGUE_TS, name="proj_moba_k")
    (mv,) = _matmul(h, seg(12, 13), _ep_cast, [jax.ShapeDtypeStruct((s_len, moba_w), BF16)],
                    [_spec_tile], name="proj_moba_v")
    n_blk = s_len // MOBA_BLOCK
    assert n_blk <= LANES, n_blk
    k_mean = mk_mean.reshape(n_blk, moba_heads, HEAD_DIM).transpose(1, 0, 2)
    k_mean = jnp.pad(k_mean, ((0, 0), (0, LANES - n_blk), (0, 0))).astype(BF16)
    moba_bias = jnp.swapaxes(_moba_select(mq, k_mean, n_blk), 1, 2)
    mk_t = _keys_t(mk, moba_heads, tk, _onehot_t(s_len, MOBA_BLOCK, tk))
    o_c = _flash(mq, moba_bias, mk_t, mv, rep=1, tq=min(1024, s_len), keys_per_var=LANES * MOBA_BLOCK,
                 out_dtype=BF16, name="moba", n_split=4)

    (gm,) = _matmul(h, seg(13, 16), _ep_sigmoid, [jax.ShapeDtypeStruct((s_len, 3 * d_model), F32)],
                    [_spec_tile], name="proj_merge_gates")
    y = _merge(o_a, o_b, o_c, big["proj_a"], big["proj_b"], big["proj_c"], layer, gm)
    (x,) = _matmul(y, big["w_out"], _ep_residual, [jax.ShapeDtypeStruct((s_len, d_model), F32)],
                   [_spec_tile], extras=(x,), extra_specs=(_spec_tile,), b_layer=layer, name="out_proj")

    h2 = _rmsnorm(x, p["norm_mlp"])
    (hid,) = _matmul(h2, big["mlp_w1"], _ep_relu2,
                     [jax.ShapeDtypeStruct((s_len, big["mlp_w1"].shape[2]), BF16)], [_spec_tile],
                     b_layer=layer, name="mlp_up")
    (x,) = _matmul(hid, big["mlp_w2"], _ep_residual, [jax.ShapeDtypeStruct((s_len, d_model), F32)],
                   [_spec_tile], extras=(x,), extra_specs=(_spec_tile,), b_layer=layer, name="mlp_down")
    return x


_LAYER_PARAMS = ("norm_mix", "norm_mlp", "w_in", "nsa_gate_b", "nsa_q_norm", "nsa_kc_norm", "nsa_ks_norm",
                 "nsa_kw_norm", "phi_pe_k", "phi_w1_k", "phi_w2_k", "phi_pe_v", "phi_w1_v", "phi_w2_v",
                 "sgu_norm", "sgu_w", "sgu_b", "moba_q_norm", "moba_k_norm", "proj_a", "proj_b", "proj_c",
                 "w_out", "mlp_w1", "mlp_w2")
_BIG_PARAMS = ("proj_a", "proj_b", "proj_c", "w_out", "mlp_w1", "mlp_w2")


def kernel(x, positions, norm_mix, norm_mlp, w_in, nsa_gate_b, nsa_q_norm, nsa_kc_norm, nsa_ks_norm, nsa_kw_norm, phi_pe_k, phi_w1_k, phi_w2_k, phi_pe_v, phi_w1_v, phi_w2_v, sgu_norm, sgu_w, sgu_b, moba_q_norm, moba_k_norm, proj_a, proj_b, proj_c, w_out, mlp_w1, mlp_w2):
    stacked = dict(zip(_LAYER_PARAMS, (norm_mix, norm_mlp, w_in, nsa_gate_b, nsa_q_norm, nsa_kc_norm,
                                       nsa_ks_norm, nsa_kw_norm, phi_pe_k, phi_w1_k, phi_w2_k, phi_pe_v,
                                       phi_w1_v, phi_w2_v, sgu_norm, sgu_w, sgu_b, moba_q_norm, moba_k_norm,
                                       proj_a, proj_b, proj_c, w_out, mlp_w1, mlp_w2)))
    depth = w_in.shape[0]
    big = {k: stacked.pop(k).astype(BF16) for k in _BIG_PARAMS}
    inv = ROPE_THETA ** (-jnp.arange(0, HEAD_DIM, 2, dtype=F32) / HEAD_DIM)
    outs = []
    for b in range(x.shape[0]):
        ang = positions[b].astype(F32)[:, None] * inv
        cos = jnp.concatenate([jnp.cos(ang), jnp.cos(ang)], axis=-1)
        sin = jnp.concatenate([-jnp.sin(ang), jnp.sin(ang)], axis=-1)
        xb = x[b]
        for l in range(depth):
            xb = _layer(xb, cos, sin, {k: v[l] for k, v in stacked.items()}, big, l)
        outs.append(xb)
    return jnp.stack(outs)
```

```python
import functools
import math

import jax
import jax.numpy as jnp
import numpy as np
from jax import lax
from jax.experimental import pallas as pl
from jax.experimental.pallas import tpu as pltpu

F32 = jnp.float32
BF16 = jnp.bfloat16

HEAD_DIM = 128
LANES = 128
ROPE_THETA = 10000.0
NORM_EPS = 1e-6
NEG = -1e30
FORCED = 1e9
BELOW_NEG = -3e38

NSA_REP = 4
CMP_LEN = 32
CMP_STRIDE = 16
SLC_LEN = 64
SLC_TOPK = 16
WINDOW = 512
SGU_GROUPS = 8
SGU_CHUNK = 128
MOBA_BLOCK = 256
MOBA_TOPK = 3

LIGHT_EPILOGUE_TS = 512
HEAVY_EPILOGUE_TS = 1024

MIB = 1024 * 1024
VMEM_LIMIT = 52 * MIB


def _cparams(sem, vmem=VMEM_LIMIT):
    return pltpu.CompilerParams(dimension_semantics=sem, vmem_limit_bytes=vmem)


def _tile(n, pref):
    if n <= pref:
        return n
    t = (pref // LANES) * LANES
    while t >= LANES:
        if n % t == 0:
            return t
        t -= LANES
    raise ValueError(f"no 128-multiple tile divides {n}")


def _gelu(x):
    c = math.sqrt(2.0 / math.pi)
    return 0.5 * x * (1.0 + jnp.tanh(c * (x + 0.044715 * (x * x * x))))


def _sigmoid(x):
    return 1.0 / (1.0 + jnp.exp(-x))


def _head_norm(x, gain):
    return x * lax.rsqrt(jnp.mean(x * x, axis=-1, keepdims=True) + NORM_EPS) * gain


def _rope(x, cos, sin_signed):
    return x * cos + pltpu.roll(x, HEAD_DIM // 2, 1) * sin_signed


def _rmsnorm_body(x_ref, g_ref, o_ref):
    x = x_ref[...]
    y = x * lax.rsqrt(jnp.mean(x * x, axis=-1, keepdims=True) + NORM_EPS)
    o_ref[...] = (y * g_ref[...]).astype(o_ref.dtype)


def _rmsnorm(x, gain, tm=256):
    m, d = x.shape
    tm = min(tm, m)
    return pl.pallas_call(
        _rmsnorm_body,
        grid=(m // tm,),
        in_specs=[pl.BlockSpec((tm, d), lambda i: (i, 0)),
                  pl.BlockSpec((1, d), lambda i: (0, 0))],
        out_specs=pl.BlockSpec((tm, d), lambda i: (i, 0)),
        out_shape=jax.ShapeDtypeStruct((m, d), BF16),
        compiler_params=_cparams(("parallel",)),
        name="rmsnorm",
    )(x, gain.reshape(1, d))


def _mm_body(*refs, n_extra, n_out, nk, tn, ts, epilogue):
    a_ref, b_ref = refs[0], refs[1]
    extra = refs[2:2 + n_extra]
    outs = refs[2 + n_extra:2 + n_extra + n_out]
    acc_ref = refs[-1] if nk > 1 else None

    def finish():
        for c0 in range(0, tn, ts):
            part = jnp.dot(a_ref[...], b_ref[:, c0:c0 + ts], preferred_element_type=F32)
            if nk > 1:
                part = part + acc_ref[:, c0:c0 + ts]
            epilogue(part, extra, outs, slice(c0, c0 + ts))

    if nk == 1:
        finish()
        return
    k = pl.program_id(2)
    split_finish = ts < tn

    @pl.when(k == 0)
    def _():
        acc_ref[...] = jnp.dot(a_ref[...], b_ref[...], preferred_element_type=F32)

    @pl.when((k > 0) & (k < nk - 1) if split_finish else k > 0)
    def _():
        acc_ref[...] += jnp.dot(a_ref[...], b_ref[...], preferred_element_type=F32)

    @pl.when(k == nk - 1)
    def _():
        if split_finish:
            finish()
        else:
            epilogue(acc_ref[...], extra, outs, slice(0, tn))


def _matmul(a, b, epilogue, out_shapes, out_specs, extras=(), extra_specs=(),
            tm=1024, tn=1024, tk=2048, ts=LIGHT_EPILOGUE_TS, b_layer=0, name="matmul"):
    m, kdim = a.shape
    n = b.shape[-1]
    tm, tn, tk = min(tm, m), _tile(n, tn), _tile(kdim, tk)
    ts = min(ts, tn)
    if b.ndim == 3:
        b_spec = pl.BlockSpec((None, tk, tn), lambda i, j, k: (b_layer, k, j))
    else:
        b_spec = pl.BlockSpec((tk, tn), lambda i, j, k: (k, j))
    nk = kdim // tk
    body = functools.partial(_mm_body, n_extra=len(extras), n_out=len(out_shapes), nk=nk, tn=tn, ts=ts,
                             epilogue=epilogue)
    scratch = [] if nk == 1 else [pltpu.VMEM((tm, tn), F32)]
    return pl.pallas_call(
        body,
        grid=(m // tm, n // tn, nk),
        in_specs=[pl.BlockSpec((tm, tk), lambda i, j, k: (i, k)),
                  b_spec]
                 + [s(tm, tn) for s in extra_specs],
        out_specs=[s(tm, tn) for s in out_specs],
        out_shape=out_shapes,
        scratch_shapes=scratch,
        compiler_params=_cparams(("parallel", "parallel", "arbitrary")),
        name=name,
    )(a, b, *extras)


def _spec_tile(tm, tn):
    return pl.BlockSpec((tm, tn), lambda i, j, k: (i, j))


def _spec_col(tm, tn):
    return pl.BlockSpec((1, tn), lambda i, j, k: (0, j))


def _spec_rope(tm, tn):
    return pl.BlockSpec((tm, HEAD_DIM), lambda i, j, k: (i, 0))


def _spec_blockmean(tm, tn):
    return pl.BlockSpec((1, tm // MOBA_BLOCK, tn), lambda i, j, k: (i, 0, j))


def _ep_cast(acc, extra, outs, cols):
    outs[0][:, cols] = acc.astype(outs[0].dtype)


def _ep_sigmoid_bias(acc, extra, outs, cols):
    outs[0][:, cols] = _sigmoid(acc + extra[0][:, cols])


def _ep_sigmoid(acc, extra, outs, cols):
    outs[0][:, cols] = _sigmoid(acc)


def _ep_relu2(acc, extra, outs, cols):
    r = jnp.maximum(acc, 0.0)
    outs[0][:, cols] = (r * r).astype(outs[0].dtype)


def _ep_residual(acc, extra, outs, cols):
    outs[0][:, cols] = extra[0][:, cols] + acc


def _heads(cols):
    return [(slice(c - cols.start, c - cols.start + HEAD_DIM), slice(c, c + HEAD_DIM))
            for c in range(cols.start, cols.stop, HEAD_DIM)]


def _ep_q(acc, extra, outs, cols, *, scale):
    cos, sin = extra[1][...], extra[2][...]
    for a_sl, t_sl in _heads(cols):
        y = _head_norm(acc[:, a_sl], extra[0][:, t_sl])
        outs[0][:, t_sl] = (y * scale).astype(BF16)
        outs[1][:, t_sl] = (_rope(y, cos, sin) * scale).astype(BF16)


def _ep_qrot(acc, extra, outs, cols, *, scale):
    cos, sin = extra[1][...], extra[2][...]
    for a_sl, t_sl in _heads(cols):
        y = _head_norm(acc[:, a_sl], extra[0][:, t_sl])
        outs[0][:, t_sl] = (_rope(y, cos, sin) * scale).astype(BF16)


def _ep_krot(acc, extra, outs, cols, *, block_mean):
    cos, sin = extra[1][...], extra[2][...]
    for a_sl, t_sl in _heads(cols):
        y = _rope(_head_norm(acc[:, a_sl], extra[0][:, t_sl]), cos, sin)
        outs[0][:, t_sl] = y.astype(BF16)
        if block_mean:
            for blk in range(acc.shape[0] // MOBA_BLOCK):
                rows = y[blk * MOBA_BLOCK:(blk + 1) * MOBA_BLOCK]
                outs[1][0, blk:blk + 1, t_sl] = jnp.mean(rows, axis=0, keepdims=True)


def _compress_body(a_ref, pe_ref, w1_ref, w2_ref, g_ref, o_ref, *, norm):
    half = CMP_STRIDE * HEAD_DIM
    a = a_ref[0]
    n_chunk = a.shape[0]
    x1 = (a + pe_ref[:, :half]).astype(BF16)
    x2 = (a + pe_ref[:, half:]).astype(BF16)
    p1 = jnp.dot(x1, w1_ref[:half, :], preferred_element_type=F32)
    p2 = jnp.dot(x2, w1_ref[half:, :], preferred_element_type=F32)
    h = _gelu(p1 + pltpu.roll(p2, n_chunk - 1, 0))
    o = jnp.dot(h.astype(BF16), w2_ref[...], preferred_element_type=F32)
    if norm:
        o = _head_norm(o, g_ref[...])
    o_ref[0] = o.astype(o_ref.dtype)


def _compress(a, pe, w1, w2, gain, norm):
    g, n_chunk, width = a.shape
    hidden = w1.shape[1]
    return pl.pallas_call(
        functools.partial(_compress_body, norm=norm),
        grid=(g,),
        in_specs=[pl.BlockSpec((1, n_chunk, width), lambda i: (i, 0, 0)),
                  pl.BlockSpec((1, 2 * width), lambda i: (0, 0)),
                  pl.BlockSpec((2 * width, hidden), lambda i: (0, 0)),
                  pl.BlockSpec((hidden, HEAD_DIM), lambda i: (0, 0)),
                  pl.BlockSpec((1, HEAD_DIM), lambda i: (0, 0))],
        out_specs=pl.BlockSpec((1, n_chunk, HEAD_DIM), lambda i: (i, 0, 0)),
        out_shape=jax.ShapeDtypeStruct((g, n_chunk, HEAD_DIM), BF16),
        compiler_params=_cparams(("parallel",)),
        name="nsa_compress",
    )(a, pe.reshape(1, 2 * width), w1.astype(BF16), w2.astype(BF16), gain.reshape(1, HEAD_DIM))


def _topk_mask(score, index_f, k, axis):
    n = score.shape[axis]
    sel = jnp.zeros(score.shape, dtype=jnp.bool_)
    for _ in range(k):
        m = jnp.max(score, axis=axis, keepdims=True)
        first = jnp.min(jnp.where(score == m, index_f, float(n)), axis=axis, keepdims=True)
        hit = index_f == first
        sel = jnp.logical_or(sel, hit)
        score = jnp.where(hit, BELOW_NEG, score)
    return sel


def _nsa_cmp_body(q_ref, kt_ref, v_ref, ovt_ref, o_ref, sbt_ref, *, tq, n_slc, cw):
    qi = pl.program_id(1)
    q0 = qi * tq
    n_pad = kt_ref.shape[-1]
    q = jnp.concatenate([q_ref[:, r * HEAD_DIM:(r + 1) * HEAD_DIM] for r in range(NSA_REP)], axis=0)
    t1 = q0 + lax.broadcasted_iota(jnp.int32, (tq, 1), 0)
    t = jnp.concatenate([t1] * NSA_REP, axis=0)

    def attend(width):
        s = jnp.dot(q, kt_ref[0, :, :width], preferred_element_type=F32)
        cmp_end = lax.broadcasted_iota(jnp.int32, (1, width), 1) * CMP_STRIDE + (CMP_LEN - 1)
        s = jnp.where(cmp_end <= t, s, NEG)
        m = jnp.max(s, axis=-1, keepdims=True)
        e = jnp.exp2(s - m)
        inv = jnp.where(m > 0.5 * NEG, 1.0 / jnp.sum(e, axis=-1, keepdims=True), 0.0)
        p = e * inv
        o = jnp.dot(p.astype(BF16), v_ref[0, :width, :], preferred_element_type=F32)
        for r in range(NSA_REP):
            o_ref[:, r * HEAD_DIM:(r + 1) * HEAD_DIM] = o[r * tq:(r + 1) * tq]
        ps = p[0:tq]
        for r in range(1, NSA_REP):
            ps = ps + p[r * tq:(r + 1) * tq]
        ps_hi = ps.astype(BF16)
        ps_lo = (ps - ps_hi.astype(F32)).astype(BF16)
        nt = (((1,), (1,)), ((), ()))
        n_rows = width * CMP_STRIDE // SLC_LEN
        ovt = ovt_ref[:n_rows, :width]
        imp = (lax.dot_general(ovt, ps_hi, nt, preferred_element_type=F32)
               + lax.dot_general(ovt, ps_lo, nt, preferred_element_type=F32))
        j = lax.broadcasted_iota(jnp.int32, imp.shape, 0)
        cur = (q0 + lax.broadcasted_iota(jnp.int32, (1, tq), 1)) // SLC_LEN
        forced = (j == 0) | (j == cur) | (j == cur - 1)
        allowed = j <= cur
        score = jnp.where(allowed & jnp.logical_not(forced), imp, NEG)
        score = jnp.where(j < n_slc, score, BELOW_NEG)
        sel = _topk_mask(score, j.astype(F32), min(SLC_TOPK, n_slc) - 3, 0)
        sbt_ref[0, :n_rows, :] = jnp.where((sel | forced) & allowed, 0.0, NEG).astype(BF16)
        if n_rows < sbt_ref.shape[1]:
            sbt_ref[0, n_rows:, :] = jnp.full((sbt_ref.shape[1] - n_rows, tq), NEG, BF16)

    n_chunks = ((q0 + tq) // CMP_STRIDE - 1 + cw - 1) // cw
    for v in range(1, n_pad // cw + 1):
        pl.when(n_chunks == v)(functools.partial(attend, v * cw))


def _nsa_cmp(q_c, kc_t, vc, overlap_t, n_slc, tq=256):
    s_len = q_c.shape[0]
    tq = min(tq, s_len)
    g, _, n_pad = kc_t.shape
    n_slc_pad = overlap_t.shape[0]
    gw = NSA_REP * HEAD_DIM
    cw = min(256, n_pad)
    assert n_pad % cw == 0, (n_pad, cw)
    return pl.pallas_call(
        functools.partial(_nsa_cmp_body, tq=tq, n_slc=n_slc, cw=cw),
        grid=(g, s_len // tq),
        in_specs=[pl.BlockSpec((tq, gw), lambda gi, qi: (qi, gi)),
                  pl.BlockSpec((1, HEAD_DIM, n_pad), lambda gi, qi: (gi, 0, 0)),
                  pl.BlockSpec((1, n_pad, HEAD_DIM), lambda gi, qi: (gi, 0, 0)),
                  pl.BlockSpec((n_slc_pad, n_pad), lambda gi, qi: (0, 0))],
        out_specs=[pl.BlockSpec((tq, gw), lambda gi, qi: (qi, gi)),
                   pl.BlockSpec((1, n_slc_pad, tq), lambda gi, qi: (gi, 0, qi))],
        out_shape=[jax.ShapeDtypeStruct((s_len, g * gw), F32),
                   jax.ShapeDtypeStruct((g, n_slc_pad, s_len), BF16)],
        compiler_params=_cparams(("parallel", "parallel")),
        name="nsa_cmp_select",
    )(q_c, kc_t, vc, overlap_t)


def _flash_body(q_ref, sb_ref, kt_ref, v_ref, o_ref, qa_s, m_s, acc_s, s_s, *,
                rep, tq, tk, keys_per_var, n_var, n_split):
    rows = rep * tq
    qi = pl.program_id(1)
    q0 = qi * tq
    t1 = q0 + lax.broadcasted_iota(jnp.int32, (tq, 1), 0)
    q = jnp.concatenate([q_ref[:, r * HEAD_DIM:(r + 1) * HEAD_DIM] for r in range(rep)], axis=0)
    for var in range(n_var):
        sb = sb_ref[0][:, var * LANES:(var + 1) * LANES]
        qa_s[var] = jnp.concatenate([q, jnp.concatenate([sb] * rep, axis=0)], axis=1)
    m_s[...] = jnp.full(m_s.shape, NEG, F32)
    acc_s[...] = jnp.zeros(acc_s.shape, F32)
    t = jnp.concatenate([t1] * rep, axis=0)
    ones = jnp.ones((tk, HEAD_DIM), BF16)

    chunk = rows // n_split

    def scores(kt, c):
        var = (kt * tk) // keys_per_var if n_var > 1 else 0
        return jnp.dot(qa_s[var, c * chunk:(c + 1) * chunk, :], kt_ref[0, kt], preferred_element_type=F32)

    def step(kt, causal, prefetch):
        start = pl.multiple_of(kt * tk, tk)
        v_aug = jnp.concatenate([v_ref[pl.ds(start, tk), :], ones], axis=1)
        for c in range(n_split):
            rs = slice(c * chunk, (c + 1) * chunk)
            s = s_s[rs, :]
            if prefetch and c % 2 == 0:
                s_s[rs, :] = scores(kt + 1, c)
            if causal:
                pos = kt * tk + lax.broadcasted_iota(jnp.int32, (1, tk), 1)
                s = jnp.where(pos <= t[rs], s, NEG)
            m_old = m_s[rs, :]
            m_new = jnp.maximum(m_old, jnp.max(s, axis=-1, keepdims=True))
            alpha = jnp.exp2(m_old - m_new)
            p = jnp.concatenate([jnp.exp2(s[:, b * LANES:(b + 1) * LANES] - m_new).astype(BF16)
                                 for b in range(tk // LANES)], axis=1)
            pv = jnp.dot(p, v_aug, preferred_element_type=F32)
            if prefetch and c % 2 == 1:
                s_s[rs, :] = scores(kt + 1, c)
            acc_s[rs, :] = jnp.concatenate([alpha, alpha], axis=1) * acc_s[rs, :] + pv
            m_s[rs, :] = m_new

    n_full = q0 // tk
    n_diag = max(1, tq // tk)
    for c in range(n_split):
        s_s[c * chunk:(c + 1) * chunk, :] = scores(0, c)

    def full_step(kt, carry):
        step(kt, False, True)
        return carry

    lax.fori_loop(0, n_full, full_step, 0)
    for d in range(n_diag):
        step(n_full + d, True, d < n_diag - 1)
    acc = acc_s[...]
    o = acc[:, :HEAD_DIM] / acc[:, HEAD_DIM:]
    for r in range(rep):
        o_ref[:, r * HEAD_DIM:(r + 1) * HEAD_DIM] = o[r * tq:(r + 1) * tq].astype(o_ref.dtype)


def _flash(q, bias, kt_aug, v, *, rep, tq, keys_per_var, out_dtype, name, n_split):
    s_len = q.shape[0]
    g, n_kt, kdim, tk = kt_aug.shape
    gw = rep * HEAD_DIM
    rows = rep * tq
    n_var = bias.shape[-1] // LANES
    assert (tk % tq == 0 or tq % tk == 0) and rows % n_split == 0, (tq, tk, rows, n_split)
    return pl.pallas_call(
        functools.partial(_flash_body, rep=rep, tq=tq, tk=tk, keys_per_var=keys_per_var,
                          n_var=n_var, n_split=n_split),
        grid=(g, s_len // tq),
        in_specs=[pl.BlockSpec((tq, gw), lambda gi, qi: (qi, gi)),
                  pl.BlockSpec((1, tq, n_var * LANES), lambda gi, qi: (gi, qi, 0)),
                  pl.BlockSpec((1, n_kt, kdim, tk), lambda gi, qi: (gi, 0, 0, 0)),
                  pl.BlockSpec((s_len, HEAD_DIM), lambda gi, qi: (0, gi))],
        out_specs=pl.BlockSpec((tq, gw), lambda gi, qi: (qi, gi)),
        out_shape=jax.ShapeDtypeStruct((s_len, g * gw), out_dtype),
        scratch_shapes=[pltpu.VMEM((n_var, rows, 2 * HEAD_DIM), BF16),
                        pltpu.VMEM((rows, LANES), F32),
                        pltpu.VMEM((rows, 2 * HEAD_DIM), F32),
                        pltpu.VMEM((rows, tk), F32)],
        compiler_params=_cparams(("parallel", "arbitrary")),
        name=name,
    )(q, bias, kt_aug, v)


def _flash_t_body(qt_ref, bt_ref, k_ref, vt_ref, o_ref, qa_s, m_s, l_s, acc_s, s_s, *,
                  rep, tq, tk, keys_per_var, n_var, n_split):
    width = rep * tq
    qi = pl.program_id(1)
    q0 = qi * tq
    for var in range(n_var):
        bt = bt_ref[0, var * LANES:(var + 1) * LANES, :]
        qa_s[var, :HEAD_DIM, :] = jnp.concatenate(
            [qt_ref[r * HEAD_DIM:(r + 1) * HEAD_DIM, :] for r in range(rep)], axis=1)
        qa_s[var, HEAD_DIM:, :] = jnp.concatenate([bt] * rep, axis=1)
    m_s[...] = jnp.full(m_s.shape, NEG, F32)
    l_s[...] = jnp.zeros(l_s.shape, F32)
    acc_s[...] = jnp.zeros(acc_s.shape, F32)
    t1 = q0 + lax.broadcasted_iota(jnp.int32, (1, tq), 1)
    t = jnp.concatenate([t1] * rep, axis=1)
    cw = width // n_split

    def scores(kt, c):
        var = (kt * tk) // keys_per_var if n_var > 1 else 0
        return jnp.dot(k_ref[0, kt], qa_s[var, :, c * cw:(c + 1) * cw], preferred_element_type=F32)

    def consume(kt, c, causal):
        cs = slice(c * cw, (c + 1) * cw)
        s = s_s[:, cs]
        if causal:
            pos = kt * tk + lax.broadcasted_iota(jnp.int32, (tk, 1), 0)
            s = jnp.where(pos <= t[:, cs], s, NEG)
        m_old = m_s[:, cs]
        m_new = jnp.maximum(m_old, jnp.max(s, axis=0, keepdims=True))
        alpha = jnp.exp2(m_old - m_new)
        p = jnp.exp2(s - m_new)
        l_s[:, cs] = alpha * l_s[:, cs] + jnp.sum(p, axis=0, keepdims=True)
        m_s[:, cs] = m_new
        return alpha, p.astype(BF16)

    def step(kt, causal, prefetch):
        v_t = vt_ref[0, kt]
        for c0 in range(0, n_split, 2):
            ready = [consume(kt, c, causal) for c in (c0, c0 + 1)]
            if prefetch:
                for c in (c0, c0 + 1):
                    s_s[:, c * cw:(c + 1) * cw] = scores(kt + 1, c)
            for c, (alpha, p) in zip((c0, c0 + 1), ready):
                cs = slice(c * cw, (c + 1) * cw)
                acc_s[:, cs] = alpha * acc_s[:, cs] + jnp.dot(v_t, p, preferred_element_type=F32)

    n_full = q0 // tk
    n_diag = max(1, tq // tk)
    for c in range(n_split):
        s_s[:, c * cw:(c + 1) * cw] = scores(0, c)

    def full_step(kt, carry):
        step(kt, False, True)
        return carry

    lax.fori_loop(0, n_full, full_step, 0)
    for d in range(n_diag):
        step(n_full + d, True, d < n_diag - 1)
    o_t = acc_s[...] / l_s[...]
    for r in range(rep):
        o_ref[:, r * HEAD_DIM:(r + 1) * HEAD_DIM] = o_t[:, r * tq:(r + 1) * tq].T.astype(o_ref.dtype)


def _flash_t(q_t, bias_t, k_aug, v_t, *, rep, tq, keys_per_var, out_dtype, name, n_split):
    s_len = q_t.shape[1]
    g, n_kt, tk, kdim = k_aug.shape
    gw = rep * HEAD_DIM
    width = rep * tq
    n_var = bias_t.shape[1] // LANES
    assert (tk % tq == 0 or tq % tk == 0) and n_split % 2 == 0 and width % (n_split * LANES) == 0, \
        (tq, tk, width, n_split)
    return pl.pallas_call(
        functools.partial(_flash_t_body, rep=rep, tq=tq, tk=tk, keys_per_var=keys_per_var,
                          n_var=n_var, n_split=n_split),
        grid=(g, s_len // tq),
        in_specs=[pl.BlockSpec((gw, tq), lambda gi, qi: (gi, qi)),
                  pl.BlockSpec((1, n_var * LANES, tq), lambda gi, qi: (gi, 0, qi)),
                  pl.BlockSpec((1, n_kt, tk, kdim), lambda gi, qi: (gi, 0, 0, 0)),
                  pl.BlockSpec((1, n_kt, HEAD_DIM, tk), lambda gi, qi: (gi, 0, 0, 0))],
        out_specs=pl.BlockSpec((tq, gw), lambda gi, qi: (qi, gi)),
        out_shape=jax.ShapeDtypeStruct((s_len, g * gw), out_dtype),
        scratch_shapes=[pltpu.VMEM((n_var, 2 * HEAD_DIM, width), BF16),
                        pltpu.VMEM((1, width), F32),
                        pltpu.VMEM((1, width), F32),
                        pltpu.VMEM((HEAD_DIM, width), F32),
                        pltpu.VMEM((tk, width), F32)],
        compiler_params=_cparams(("parallel", "arbitrary")),
        name=name,
    )(q_t, bias_t, k_aug, v_t)


def _moba_select_body(q_ref, km_ref, bt_ref, *, tq, n_blk):
    q0 = pl.program_id(1) * tq
    nt = (((1,), (1,)), ((), ()))
    sg = lax.dot_general(km_ref[0], q_ref[...], nt, preferred_element_type=F32)
    j = lax.broadcasted_iota(jnp.int32, sg.shape, 0)
    cur = (q0 + lax.broadcasted_iota(jnp.int32, (1, tq), 1)) // MOBA_BLOCK
    past = j < cur
    score = jnp.where(j < n_blk, jnp.where(past, sg, NEG), BELOW_NEG)
    sel = _topk_mask(score, j.astype(F32), min(MOBA_TOPK, n_blk), 0)
    bt_ref[0] = jnp.where((sel & past) | (j == cur), 0.0, NEG).astype(BF16)


def _moba_select(q, k_mean, n_blk, tq=512):
    s_len = q.shape[0]
    tq = min(tq, s_len)
    n_heads = k_mean.shape[0]
    return pl.pallas_call(
        functools.partial(_moba_select_body, tq=tq, n_blk=n_blk),
        grid=(n_heads, s_len // tq),
        in_specs=[pl.BlockSpec((tq, HEAD_DIM), lambda hi, qi: (qi, hi)),
                  pl.BlockSpec((1, LANES, HEAD_DIM), lambda hi, qi: (hi, 0, 0))],
        out_specs=pl.BlockSpec((1, LANES, tq), lambda hi, qi: (hi, 0, qi)),
        out_shape=jax.ShapeDtypeStruct((n_heads, LANES, s_len), BF16),
        compiler_params=_cparams(("parallel", "parallel")),
        name="moba_select",
    )(q, k_mean)


def _window_body(q_ref, kt_ref, v_ref, oc_ref, os_ref, g_ref, o_ref, *, tq):
    qi = pl.program_id(1)
    q0 = qi * tq
    n_past = WINDOW // tq
    q = jnp.concatenate([q_ref[:, r * HEAD_DIM:(r + 1) * HEAD_DIM] for r in range(NSA_REP)], axis=0)
    t1 = q0 + lax.broadcasted_iota(jnp.int32, (tq, 1), 0)
    t = jnp.concatenate([t1] * NSA_REP, axis=0)
    lane = lax.broadcasted_iota(jnp.int32, (1, tq), 1)
    ones = jnp.ones((tq, HEAD_DIM), BF16)
    scores, tiles = [], []
    for i in range(n_past + 1):
        raw = qi - n_past + i
        idx = jnp.maximum(raw, 0)
        tiles.append(idx)
        s = jnp.dot(q, kt_ref[0, idx], preferred_element_type=F32)
        pos = raw * tq + lane
        if i == 0:
            s = jnp.where(pos > t - WINDOW, s, NEG)
        if i == n_past:
            s = jnp.where(pos <= t, s, NEG)
        else:
            s = s + jnp.where(raw >= 0, 0.0, NEG)
        scores.append(s)
    m = jnp.max(scores[0], axis=-1, keepdims=True)
    for s in scores[1:]:
        m = jnp.maximum(m, jnp.max(s, axis=-1, keepdims=True))
    acc = jnp.zeros((NSA_REP * tq, 2 * HEAD_DIM), F32)
    for i, s in enumerate(scores):
        start = pl.multiple_of(tiles[i] * tq, tq)
        v_aug = jnp.concatenate([v_ref[pl.ds(start, tq), :], ones], axis=1)
        acc = acc + jnp.dot(jnp.exp2(s - m).astype(BF16), v_aug, preferred_element_type=F32)
    o_win = acc[:, :HEAD_DIM] / acc[:, HEAD_DIM:]
    gate = g_ref[0]
    for r in range(NSA_REP):
        sl = slice(r * HEAD_DIM, (r + 1) * HEAD_DIM)
        o = (gate[:, r:r + 1] * oc_ref[:, sl]
             + gate[:, NSA_REP + r:NSA_REP + r + 1] * os_ref[:, sl]
             + gate[:, 2 * NSA_REP + r:2 * NSA_REP + r + 1] * o_win[r * tq:(r + 1) * tq])
        o_ref[:, sl] = o.astype(o_ref.dtype)


def _window_combine(q_r, kw_t, v, v_block0, o_cmp, o_slc, gates):
    s_len = q_r.shape[0]
    g, n_kt, _, tq = kw_t.shape
    assert WINDOW % tq == 0, tq
    gw = NSA_REP * HEAD_DIM
    tile = pl.BlockSpec((tq, gw), lambda gi, qi: (qi, gi))
    return pl.pallas_call(
        functools.partial(_window_body, tq=tq),
        grid=(g, s_len // tq),
        in_specs=[tile,
                  pl.BlockSpec((1, n_kt, HEAD_DIM, tq), lambda gi, qi: (gi, 0, 0, 0)),
                  pl.BlockSpec((s_len, HEAD_DIM), lambda gi, qi: (0, v_block0 + gi)),
                  tile, tile,
                  pl.BlockSpec((1, tq, LANES), lambda gi, qi: (gi, qi, 0))],
        out_specs=tile,
        out_shape=jax.ShapeDtypeStruct((s_len, g * gw), BF16),
        compiler_params=_cparams(("parallel", "parallel")),
        name="nsa_window_combine",
    )(q_r, kw_t, v, o_cmp, o_slc, gates)


def _sgu_body(zu_ref, zv_ref, gain_ref, w_ref, bt_ref, o_ref, *, tm):
    gd = zu_ref.shape[1] // SGU_GROUPS
    row = lax.broadcasted_iota(jnp.int32, (SGU_CHUNK, SGU_CHUNK), 0)
    col = lax.broadcasted_iota(jnp.int32, (SGU_CHUNK, SGU_CHUNK), 1)
    bt = bt_ref[...]
    for g in range(SGU_GROUPS):
        sl = slice(g * gd, (g + 1) * gd)
        v = _gelu(zv_ref[:, sl])
        v = (v * lax.rsqrt(jnp.mean(v * v, axis=-1, keepdims=True) + NORM_EPS) * gain_ref[:, sl]).astype(BF16)
        w = jnp.where(col <= row, w_ref[g], 0.0).astype(BF16)
        for c in range(tm // SGU_CHUNK):
            rs = slice(c * SGU_CHUNK, (c + 1) * SGU_CHUNK)
            mixed = jnp.dot(w, v[rs], preferred_element_type=F32) + bt[:, g:g + 1]
            o_ref[rs, sl] = (_gelu(zu_ref[rs, sl]) * mixed).astype(o_ref.dtype)


def _sgu(z_uv, gain, w_s, b_s, tm=512):
    s_len = z_uv.shape[0]
    width = z_uv.shape[1] // 2
    tm = min(tm, s_len)
    return pl.pallas_call(
        functools.partial(_sgu_body, tm=tm),
        grid=(s_len // tm,),
        in_specs=[pl.BlockSpec((tm, width), lambda i: (i, 0)),
                  pl.BlockSpec((tm, width), lambda i: (i, 1)),
                  pl.BlockSpec((1, width), lambda i: (0, 0)),
                  pl.BlockSpec((SGU_GROUPS, SGU_CHUNK, SGU_CHUNK), lambda i: (0, 0, 0)),
                  pl.BlockSpec((SGU_CHUNK, SGU_GROUPS), lambda i: (0, 0))],
        out_specs=pl.BlockSpec((tm, width), lambda i: (i, 0)),
        out_shape=jax.ShapeDtypeStruct((s_len, width), BF16),
        compiler_params=_cparams(("parallel",)),
        name="sgu",
    )(z_uv, z_uv, gain.reshape(1, width), w_s, b_s.T)


def _merge_body(oa_ref, ob_ref, oc_ref, pa_ref, pb_ref, pc_ref, ga_ref, gb_ref, gc_ref, y_ref):
    y = ga_ref[...] * jnp.dot(oa_ref[...], pa_ref[...], preferred_element_type=F32)
    y = y + gb_ref[...] * jnp.dot(ob_ref[...], pb_ref[...], preferred_element_type=F32)
    y = y + gc_ref[...] * jnp.dot(oc_ref[...], pc_ref[...], preferred_element_type=F32)
    y_ref[...] = y.astype(y_ref.dtype)


def _merge(o_a, o_b, o_c, p_a, p_b, p_c, layer, gm, tm=512, tn=1024):
    s_len = o_a.shape[0]
    d = p_a.shape[2]
    tm, tn = min(tm, s_len), _tile(d, tn)
    nj = d // tn

    def rows(w):
        return pl.BlockSpec((tm, w), lambda i, j: (i, 0))

    def cols(kdim):
        return pl.BlockSpec((None, kdim, tn), lambda i, j: (layer, 0, j))

    def gate(off):
        return pl.BlockSpec((tm, tn), lambda i, j: (i, off * nj + j))

    return pl.pallas_call(
        _merge_body,
        grid=(s_len // tm, nj),
        in_specs=[rows(o_a.shape[1]), rows(o_b.shape[1]), rows(o_c.shape[1]),
                  cols(p_a.shape[1]), cols(p_b.shape[1]), cols(p_c.shape[1]),
                  gate(0), gate(1), gate(2)],
        out_specs=pl.BlockSpec((tm, tn), lambda i, j: (i, j)),
        out_shape=jax.ShapeDtypeStruct((s_len, d), BF16),
        compiler_params=_cparams(("parallel", "parallel")),
        name="gated_merge",
    )(o_a, o_b, o_c, p_a, p_b, p_c, gm, gm, gm)


def _keys_t(k, n_heads, tk, onehot_t=None):
    s_len = k.shape[0]
    kt = k.reshape(s_len // tk, tk, n_heads, HEAD_DIM).transpose(2, 0, 3, 1)
    if onehot_t is not None:
        kt = jnp.concatenate([kt, jnp.broadcast_to(onehot_t[None], (n_heads,) + onehot_t.shape)], axis=2)
    return kt


def _keys_aug(k, n_heads, tk, block):
    s_len = k.shape[0]
    key = np.arange(s_len)
    onehot = jnp.asarray(((key // block) % LANES)[:, None] == np.arange(LANES)[None, :], dtype=BF16)
    kh = k.reshape(s_len, n_heads, HEAD_DIM).transpose(1, 0, 2)
    ka = jnp.concatenate([kh, jnp.broadcast_to(onehot[None], kh.shape)], axis=2)
    return ka.reshape(n_heads, s_len // tk, tk, 2 * HEAD_DIM)


def _values_t(v, n_heads, tk):
    s_len = v.shape[0]
    return v.reshape(s_len // tk, tk, n_heads, HEAD_DIM).transpose(2, 0, 3, 1)


def _onehot_t(s_len, block, tk):
    key = np.arange(s_len)
    oh = ((key // block) % LANES)[None, :] == np.arange(LANES)[:, None]
    return jnp.asarray(oh.reshape(LANES, s_len // tk, tk).transpose(1, 0, 2), dtype=BF16)


def _overlap_t(n_pad, n_slc_pad):
    i = np.arange(n_pad)[None, :]
    j = np.arange(n_slc_pad)[:, None]
    ov = (i * CMP_STRIDE <= j * SLC_LEN + SLC_LEN - 1) & (i * CMP_STRIDE + CMP_LEN - 1 >= j * SLC_LEN)
    return jnp.asarray(ov, dtype=BF16)


def _layer(x, cos, sin, p, big, layer):
    s_len, d_model = x.shape
    scale = HEAD_DIM ** -0.5 * math.log2(math.e)
    w_in = p["w_in"]
    nsa_w = big["proj_a"].shape[1]
    sgu_w = big["proj_b"].shape[1]
    moba_w = big["proj_c"].shape[1]
    n_heads = nsa_w // HEAD_DIM
    n_groups = n_heads // NSA_REP
    kv_w = n_groups * HEAD_DIM
    moba_heads = moba_w // HEAD_DIM
    sizes = (nsa_w, kv_w, kv_w, kv_w, kv_w, kv_w, kv_w, 3 * n_heads, sgu_w, sgu_w,
             moba_w, moba_w, moba_w, d_model, d_model, d_model)
    offs = np.concatenate([[0], np.cumsum(sizes)])

    def seg(a, b):
        return w_in[:, offs[a]:offs[b]].astype(BF16)

    def tile_gain(gain, reps):
        return jnp.tile(gain, reps).reshape(1, reps * HEAD_DIM)

    h = _rmsnorm(x, p["norm_mix"])
    rope_extras = (cos, sin)
    rope_specs = (_spec_rope, _spec_rope)

    q_c, q_r = _matmul(
        h, seg(0, 1), functools.partial(_ep_q, scale=scale),
        [jax.ShapeDtypeStruct((s_len, nsa_w), BF16)] * 2, [_spec_tile, _spec_tile],
        extras=(tile_gain(p["nsa_q_norm"], n_heads),) + rope_extras,
        extra_specs=(_spec_col,) + rope_specs, ts=HEAVY_EPILOGUE_TS, name="proj_nsa_q")
    (kcvc,) = _matmul(h, seg(1, 3), _ep_cast, [jax.ShapeDtypeStruct((s_len, 2 * kv_w), F32)],
                      [_spec_tile], name="proj_nsa_cmp_kv")
    (kskw,) = _matmul(
        h, jnp.concatenate([seg(3, 4), seg(5, 6)], axis=1), functools.partial(_ep_krot, block_mean=False),
        [jax.ShapeDtypeStruct((s_len, 2 * kv_w), BF16)], [_spec_tile],
        extras=(jnp.concatenate([tile_gain(p["nsa_ks_norm"], n_groups),
                                 tile_gain(p["nsa_kw_norm"], n_groups)], axis=1),) + rope_extras,
        extra_specs=(_spec_col,) + rope_specs, ts=HEAVY_EPILOGUE_TS, name="proj_nsa_k")
    (vsvw,) = _matmul(h, jnp.concatenate([seg(4, 5), seg(6, 7)], axis=1), _ep_cast,
                      [jax.ShapeDtypeStruct((s_len, 2 * kv_w), BF16)], [_spec_tile], name="proj_nsa_v")
    n_gate = 3 * n_heads
    w_gate = jnp.pad(seg(7, 8), ((0, 0), (0, LANES - n_gate)))
    b_gate = jnp.pad(p["nsa_gate_b"], (0, LANES - n_gate)).reshape(1, LANES)
    (gates,) = _matmul(h, w_gate, _ep_sigmoid_bias, [jax.ShapeDtypeStruct((s_len, LANES), F32)],
                       [_spec_tile], extras=(b_gate,), extra_specs=(_spec_col,), name="proj_nsa_gates")

    n_chunk = s_len // CMP_STRIDE
    chunks = kcvc.reshape(n_chunk, CMP_STRIDE, 2, n_groups, HEAD_DIM).transpose(2, 3, 0, 1, 4)
    chunks = chunks.reshape(2, n_groups, n_chunk, CMP_STRIDE * HEAD_DIM)
    kc = _compress(chunks[0], p["phi_pe_k"], p["phi_w1_k"], p["phi_w2_k"], p["nsa_kc_norm"], True)
    vc = _compress(chunks[1], p["phi_pe_v"], p["phi_w1_v"], p["phi_w2_v"], p["nsa_kc_norm"], False)

    n_slc = s_len // SLC_LEN
    n_slc_pad = -(-n_slc // LANES) * LANES
    o_cmp, sel_bias_t = _nsa_cmp(q_c, jnp.swapaxes(kc, 1, 2), vc, _overlap_t(n_chunk, n_slc_pad), n_slc)
    tk = min(512, s_len)
    o_slc = _flash_t(q_r.T, sel_bias_t, _keys_aug(kskw[:, :kv_w], n_groups, tk, SLC_LEN),
                     _values_t(vsvw[:, :kv_w], n_groups, tk), rep=NSA_REP, tq=min(512, s_len),
                     keys_per_var=LANES * SLC_LEN, out_dtype=F32, name="nsa_selected", n_split=8)
    kw_t = _keys_t(kskw[:, kv_w:], n_groups, min(256, s_len))
    group_gates = gates[:, :n_gate].reshape(s_len, 3, n_groups, NSA_REP).transpose(2, 0, 1, 3)
    group_gates = jnp.pad(group_gates.reshape(n_groups, s_len, 3 * NSA_REP),
                          ((0, 0), (0, 0), (0, LANES - 3 * NSA_REP)))
    o_a = _window_combine(q_r, kw_t, vsvw, n_groups, o_cmp, o_slc, group_gates)

    (z_uv,) = _matmul(h, seg(8, 10), _ep_cast, [jax.ShapeDtypeStruct((s_len, 2 * sgu_w), F32)],
                      [_spec_tile], name="proj_sgu")
    o_b = _sgu(z_uv, p["sgu_norm"], p["sgu_w"], p["sgu_b"])

    (mq,) = _matmul(
        h, seg(10, 11), functools.partial(_ep_qrot, scale=scale),
        [jax.ShapeDtypeStruct((s_len, moba_w), BF16)], [_spec_tile],
        extras=(tile_gain(p["moba_q_norm"], moba_heads),) + rope_extras,
        extra_specs=(_spec_col,) + rope_specs, ts=HEAVY_EPILOGUE_TS, name="proj_moba_q")
    tm_k = min(1024, s_len)
    mk, mk_mean = _matmul(
        h, seg(11, 12), functools.partial(_ep_krot, block_mean=True),
        [jax.ShapeDtypeStruct((s_len, moba_w), BF16),
         jax.ShapeDtypeStruct((s_len // tm_k, tm_k // MOBA_BLOCK, moba_w), F32)],
        [_spec_tile, _spec_blockmean],
        extras=(tile_gain(p["moba_k_norm"], moba_heads),) + rope_extras,
        extra_specs=(_spec_col,) + rope_specs, tm=tm_k, ts=HEAVY_EPILOGUE_TS, name="proj_moba_k")
    (mv,) = _matmul(h, seg(12, 13), _ep_cast, [jax.ShapeDtypeStruct((s_len, moba_w), BF16)],
                    [_spec_tile], name="proj_moba_v")
    n_blk = s_len // MOBA_BLOCK
    assert n_blk <= LANES, n_blk
    k_mean = mk_mean.reshape(n_blk, moba_heads, HEAD_DIM).transpose(1, 0, 2)
    k_mean = jnp.pad(k_mean, ((0, 0), (0, LANES - n_blk), (0, 0))).astype(BF16)
    o_c = _flash_t(mq.T, _moba_select(mq, k_mean, n_blk), _keys_aug(mk, moba_heads, tk, MOBA_BLOCK),
                   _values_t(mv, moba_heads, tk), rep=1, tq=min(1024, s_len),
                   keys_per_var=LANES * MOBA_BLOCK, out_dtype=BF16, name="moba", n_split=4)

    (gm,) = _matmul(h, seg(13, 16), _ep_sigmoid, [jax.ShapeDtypeStruct((s_len, 3 * d_model), F32)],
                    [_spec_tile], name="proj_merge_gates")
    y = _merge(o_a, o_b, o_c, big["proj_a"], big["proj_b"], big["proj_c"], layer, gm)
    (x,) = _matmul(y, big["w_out"], _ep_residual, [jax.ShapeDtypeStruct((s_len, d_model), F32)],
                   [_spec_tile], extras=(x,), extra_specs=(_spec_tile,), b_layer=layer, name="out_proj")

    h2 = _rmsnorm(x, p["norm_mlp"])
    (hid,) = _matmul(h2, big["mlp_w1"], _ep_relu2,
                     [jax.ShapeDtypeStruct((s_len, big["mlp_w1"].shape[2]), BF16)], [_spec_tile],
                     b_layer=layer, name="mlp_up")
    (x,) = _matmul(hid, big["mlp_w2"], _ep_residual, [jax.ShapeDtypeStruct((s_len, d_model), F32)],
                   [_spec_tile], extras=(x,), extra_specs=(_spec_tile,), b_layer=layer, name="mlp_down")
    return x


_LAYER_PARAMS = ("norm_mix", "norm_mlp", "w_in", "nsa_gate_b", "nsa_q_norm", "nsa_kc_norm", "nsa_ks_norm",
                 "nsa_kw_norm", "phi_pe_k", "phi_w1_k", "phi_w2_k", "phi_pe_v", "phi_w1_v", "phi_w2_v",
                 "sgu_norm", "sgu_w", "sgu_b", "moba_q_norm", "moba_k_norm", "proj_a", "proj_b", "proj_c",
                 "w_out", "mlp_w1", "mlp_w2")
_BIG_PARAMS = ("proj_a", "proj_b", "proj_c", "w_out", "mlp_w1", "mlp_w2")


def kernel(x, positions, norm_mix, norm_mlp, w_in, nsa_gate_b, nsa_q_norm, nsa_kc_norm, nsa_ks_norm, nsa_kw_norm, phi_pe_k, phi_w1_k, phi_w2_k, phi_pe_v, phi_w1_v, phi_w2_v, sgu_norm, sgu_w, sgu_b, moba_q_norm, moba_k_norm, proj_a, proj_b, proj_c, w_out, mlp_w1, mlp_w2):
    stacked = dict(zip(_LAYER_PARAMS, (norm_mix, norm_mlp, w_in, nsa_gate_b, nsa_q_norm, nsa_kc_norm,
                                       nsa_ks_norm, nsa_kw_norm, phi_pe_k, phi_w1_k, phi_w2_k, phi_pe_v,
                                       phi_w1_v, phi_w2_v, sgu_norm, sgu_w, sgu_b, moba_q_norm, moba_k_norm,
                                       proj_a, proj_b, proj_c, w_out, mlp_w1, mlp_w2)))
    depth = w_in.shape[0]
    big = {k: stacked.pop(k).astype(BF16) for k in _BIG_PARAMS}
    inv = ROPE_THETA ** (-jnp.arange(0, HEAD_DIM, 2, dtype=F32) / HEAD_DIM)
    outs = []
    for b in range(x.shape[0]):
        ang = positions[b].astype(F32)[:, None] * inv
        cos = jnp.concatenate([jnp.cos(ang), jnp.cos(ang)], axis=-1)
        sin = jnp.concatenate([-jnp.sin(ang), jnp.sin(ang)], axis=-1)
        xb = x[b]
        for l in range(depth):
            xb = _layer(xb, cos, sin, {k: v[l] for k, v in stacked.items()}, big, l)
        outs.append(xb)
    return jnp.stack(outs)
```

```python
import functools
import math

import jax
import jax.numpy as jnp
import numpy as np
from jax import lax
from jax.experimental import pallas as pl
from jax.experimental.pallas import tpu as pltpu

F32 = jnp.float32
BF16 = jnp.bfloat16

HEAD_DIM = 128
LANES = 128
ROPE_THETA = 10000.0
NORM_EPS = 1e-6
NEG = -1e30
FORCED = 1e9
BELOW_NEG = -3e38

NSA_REP = 4
CMP_LEN = 32
CMP_STRIDE = 16
SLC_LEN = 64
SLC_TOPK = 16
WINDOW = 512
SGU_GROUPS = 8
SGU_CHUNK = 128
MOBA_BLOCK = 256
MOBA_TOPK = 3

LIGHT_EPILOGUE_TS = 512
HEAVY_EPILOGUE_TS = 1024

MIB = 1024 * 1024
VMEM_LIMIT = 52 * MIB


def _cparams(sem, vmem=VMEM_LIMIT):
    return pltpu.CompilerParams(dimension_semantics=sem, vmem_limit_bytes=vmem)


def _tile(n, pref):
    if n <= pref:
        return n
    t = (pref // LANES) * LANES
    while t >= LANES:
        if n % t == 0:
            return t
        t -= LANES
    raise ValueError(f"no 128-multiple tile divides {n}")


def _gelu(x):
    c = math.sqrt(2.0 / math.pi)
    return 0.5 * x * (1.0 + jnp.tanh(c * (x + 0.044715 * (x * x * x))))


def _sigmoid(x):
    return 1.0 / (1.0 + jnp.exp(-x))


def _head_norm(x, gain):
    return x * lax.rsqrt(jnp.mean(x * x, axis=-1, keepdims=True) + NORM_EPS) * gain


def _rope(x, cos, sin_signed):
    return x * cos + pltpu.roll(x, HEAD_DIM // 2, 1) * sin_signed


def _rmsnorm_body(x_ref, g_ref, o_ref):
    x = x_ref[...]
    y = x * lax.rsqrt(jnp.mean(x * x, axis=-1, keepdims=True) + NORM_EPS)
    o_ref[...] = (y * g_ref[...]).astype(o_ref.dtype)


def _rmsnorm(x, gain, tm=256):
    m, d = x.shape
    tm = min(tm, m)
    return pl.pallas_call(
        _rmsnorm_body,
        grid=(m // tm,),
        in_specs=[pl.BlockSpec((tm, d), lambda i: (i, 0)),
                  pl.BlockSpec((1, d), lambda i: (0, 0))],
        out_specs=pl.BlockSpec((tm, d), lambda i: (i, 0)),
        out_shape=jax.ShapeDtypeStruct((m, d), BF16),
        compiler_params=_cparams(("parallel",)),
        name="rmsnorm",
    )(x, gain.reshape(1, d))


def _mm_body(*refs, n_extra, n_out, nk, tn, ts, epilogue):
    a_ref, b_ref = refs[0], refs[1]
    extra = refs[2:2 + n_extra]
    outs = refs[2 + n_extra:2 + n_extra + n_out]
    acc_ref = refs[-1] if nk > 1 else None

    def finish():
        for c0 in range(0, tn, ts):
            part = jnp.dot(a_ref[...], b_ref[:, c0:c0 + ts], preferred_element_type=F32)
            if nk > 1:
                part = part + acc_ref[:, c0:c0 + ts]
            epilogue(part, extra, outs, slice(c0, c0 + ts))

    if nk == 1:
        finish()
        return
    k = pl.program_id(2)
    split_finish = ts < tn

    @pl.when(k == 0)
    def _():
        acc_ref[...] = jnp.dot(a_ref[...], b_ref[...], preferred_element_type=F32)

    @pl.when((k > 0) & (k < nk - 1) if split_finish else k > 0)
    def _():
        acc_ref[...] += jnp.dot(a_ref[...], b_ref[...], preferred_element_type=F32)

    @pl.when(k == nk - 1)
    def _():
        if split_finish:
            finish()
        else:
            epilogue(acc_ref[...], extra, outs, slice(0, tn))


def _matmul(a, b, epilogue, out_shapes, out_specs, extras=(), extra_specs=(),
            tm=1024, tn=1024, tk=2048, ts=LIGHT_EPILOGUE_TS, b_layer=0, name="matmul"):
    m, kdim = a.shape
    n = b.shape[-1]
    tm, tn, tk = min(tm, m), _tile(n, tn), _tile(kdim, tk)
    ts = min(ts, tn)
    if b.ndim == 3:
        b_spec = pl.BlockSpec((None, tk, tn), lambda i, j, k: (b_layer, k, j))
    else:
        b_spec = pl.BlockSpec((tk, tn), lambda i, j, k: (k, j))
    nk = kdim // tk
    body = functools.partial(_mm_body, n_extra=len(extras), n_out=len(out_shapes), nk=nk, tn=tn, ts=ts,
                             epilogue=epilogue)
    scratch = [] if nk == 1 else [pltpu.VMEM((tm, tn), F32)]
    return pl.pallas_call(
        body,
        grid=(m // tm, n // tn, nk),
        in_specs=[pl.BlockSpec((tm, tk), lambda i, j, k: (i, k)),
                  b_spec]
                 + [s(tm, tn) for s in extra_specs],
        out_specs=[s(tm, tn) for s in out_specs],
        out_shape=out_shapes,
        scratch_shapes=scratch,
        compiler_params=_cparams(("parallel", "parallel", "arbitrary")),
        name=name,
    )(a, b, *extras)


def _spec_tile(tm, tn):
    return pl.BlockSpec((tm, tn), lambda i, j, k: (i, j))


def _spec_col(tm, tn):
    return pl.BlockSpec((1, tn), lambda i, j, k: (0, j))


def _spec_rope(tm, tn):
    return pl.BlockSpec((tm, HEAD_DIM), lambda i, j, k: (i, 0))


def _spec_blockmean(tm, tn):
    return pl.BlockSpec((1, tm // MOBA_BLOCK, tn), lambda i, j, k: (i, 0, j))


def _ep_cast(acc, extra, outs, cols):
    outs[0][:, cols] = acc.astype(outs[0].dtype)


def _ep_sigmoid_bias(acc, extra, outs, cols):
    outs[0][:, cols] = _sigmoid(acc + extra[0][:, cols])


def _ep_sigmoid(acc, extra, outs, cols):
    outs[0][:, cols] = _sigmoid(acc)


def _ep_relu2(acc, extra, outs, cols):
    r = jnp.maximum(acc, 0.0)
    outs[0][:, cols] = (r * r).astype(outs[0].dtype)


def _ep_residual(acc, extra, outs, cols):
    outs[0][:, cols] = extra[0][:, cols] + acc


def _heads(cols):
    return [(slice(c - cols.start, c - cols.start + HEAD_DIM), slice(c, c + HEAD_DIM))
            for c in range(cols.start, cols.stop, HEAD_DIM)]


def _ep_q(acc, extra, outs, cols, *, scale):
    cos, sin = extra[1][...], extra[2][...]
    for a_sl, t_sl in _heads(cols):
        y = _head_norm(acc[:, a_sl], extra[0][:, t_sl])
        outs[0][:, t_sl] = (y * scale).astype(BF16)
        outs[1][:, t_sl] = (_rope(y, cos, sin) * scale).astype(BF16)


def _ep_qrot(acc, extra, outs, cols, *, scale):
    cos, sin = extra[1][...], extra[2][...]
    for a_sl, t_sl in _heads(cols):
        y = _head_norm(acc[:, a_sl], extra[0][:, t_sl])
        outs[0][:, t_sl] = (_rope(y, cos, sin) * scale).astype(BF16)


def _ep_krot(acc, extra, outs, cols, *, block_mean):
    cos, sin = extra[1][...], extra[2][...]
    for a_sl, t_sl in _heads(cols):
        y = _rope(_head_norm(acc[:, a_sl], extra[0][:, t_sl]), cos, sin)
        outs[0][:, t_sl] = y.astype(BF16)
        if block_mean:
            for blk in range(acc.shape[0] // MOBA_BLOCK):
                rows = y[blk * MOBA_BLOCK:(blk + 1) * MOBA_BLOCK]
                outs[1][0, blk:blk + 1, t_sl] = jnp.mean(rows, axis=0, keepdims=True)


def _compress_body(a_ref, pe_ref, w1_ref, w2_ref, g_ref, o_ref, *, norm):
    half = CMP_STRIDE * HEAD_DIM
    a = a_ref[0]
    n_chunk = a.shape[0]
    x1 = (a + pe_ref[:, :half]).astype(BF16)
    x2 = (a + pe_ref[:, half:]).astype(BF16)
    p1 = jnp.dot(x1, w1_ref[:half, :], preferred_element_type=F32)
    p2 = jnp.dot(x2, w1_ref[half:, :], preferred_element_type=F32)
    h = _gelu(p1 + pltpu.roll(p2, n_chunk - 1, 0))
    o = jnp.dot(h.astype(BF16), w2_ref[...], preferred_element_type=F32)
    if norm:
        o = _head_norm(o, g_ref[...])
    o_ref[0] = o.astype(o_ref.dtype)


def _compress(a, pe, w1, w2, gain, norm):
    g, n_chunk, width = a.shape
    hidden = w1.shape[1]
    return pl.pallas_call(
        functools.partial(_compress_body, norm=norm),
        grid=(g,),
        in_specs=[pl.BlockSpec((1, n_chunk, width), lambda i: (i, 0, 0)),
                  pl.BlockSpec((1, 2 * width), lambda i: (0, 0)),
                  pl.BlockSpec((2 * width, hidden), lambda i: (0, 0)),
                  pl.BlockSpec((hidden, HEAD_DIM), lambda i: (0, 0)),
                  pl.BlockSpec((1, HEAD_DIM), lambda i: (0, 0))],
        out_specs=pl.BlockSpec((1, n_chunk, HEAD_DIM), lambda i: (i, 0, 0)),
        out_shape=jax.ShapeDtypeStruct((g, n_chunk, HEAD_DIM), BF16),
        compiler_params=_cparams(("parallel",)),
        name="nsa_compress",
    )(a, pe.reshape(1, 2 * width), w1.astype(BF16), w2.astype(BF16), gain.reshape(1, HEAD_DIM))


def _topk_mask(score, index_f, k, axis):
    n = score.shape[axis]
    sel = jnp.zeros(score.shape, dtype=jnp.bool_)
    for _ in range(k):
        m = jnp.max(score, axis=axis, keepdims=True)
        first = jnp.min(jnp.where(score == m, index_f, float(n)), axis=axis, keepdims=True)
        hit = index_f == first
        sel = jnp.logical_or(sel, hit)
        score = jnp.where(hit, BELOW_NEG, score)
    return sel


def _nsa_cmp_body(q_ref, kt_ref, v_ref, ovt_ref, o_ref, sbt_ref, *, tq, n_slc, cw):
    qi = pl.program_id(1)
    q0 = qi * tq
    n_pad = kt_ref.shape[-1]
    q = jnp.concatenate([q_ref[:, r * HEAD_DIM:(r + 1) * HEAD_DIM] for r in range(NSA_REP)], axis=0)
    t1 = q0 + lax.broadcasted_iota(jnp.int32, (tq, 1), 0)
    t = jnp.concatenate([t1] * NSA_REP, axis=0)

    def attend(width):
        s = jnp.dot(q, kt_ref[0, :, :width], preferred_element_type=F32)
        cmp_end = lax.broadcasted_iota(jnp.int32, (1, width), 1) * CMP_STRIDE + (CMP_LEN - 1)
        s = jnp.where(cmp_end <= t, s, NEG)
        m = jnp.max(s, axis=-1, keepdims=True)
        e = jnp.exp2(s - m)
        inv = jnp.where(m > 0.5 * NEG, 1.0 / jnp.sum(e, axis=-1, keepdims=True), 0.0)
        p = e * inv
        o = jnp.dot(p.astype(BF16), v_ref[0, :width, :], preferred_element_type=F32)
        for r in range(NSA_REP):
            o_ref[:, r * HEAD_DIM:(r + 1) * HEAD_DIM] = o[r * tq:(r + 1) * tq]
        ps = p[0:tq]
        for r in range(1, NSA_REP):
            ps = ps + p[r * tq:(r + 1) * tq]
        ps_hi = ps.astype(BF16)
        ps_lo = (ps - ps_hi.astype(F32)).astype(BF16)
        nt = (((1,), (1,)), ((), ()))
        n_rows = width * CMP_STRIDE // SLC_LEN
        ovt = ovt_ref[:n_rows, :width]
        imp = (lax.dot_general(ovt, ps_hi, nt, preferred_element_type=F32)
               + lax.dot_general(ovt, ps_lo, nt, preferred_element_type=F32))
        j = lax.broadcasted_iota(jnp.int32, imp.shape, 0)
        cur = (q0 + lax.broadcasted_iota(jnp.int32, (1, tq), 1)) // SLC_LEN
        forced = (j == 0) | (j == cur) | (j == cur - 1)
        allowed = j <= cur
        score = jnp.where(allowed & jnp.logical_not(forced), imp, NEG)
        score = jnp.where(j < n_slc, score, BELOW_NEG)
        sel = _topk_mask(score, j.astype(F32), min(SLC_TOPK, n_slc) - 3, 0)
        sbt_ref[0, :n_rows, :] = jnp.where((sel | forced) & allowed, 0.0, NEG).astype(BF16)
        if n_rows < sbt_ref.shape[1]:
            sbt_ref[0, n_rows:, :] = jnp.full((sbt_ref.shape[1] - n_rows, tq), NEG, BF16)

    n_chunks = ((q0 + tq) // CMP_STRIDE - 1 + cw - 1) // cw
    for v in range(1, n_pad // cw + 1):
        pl.when(n_chunks == v)(functools.partial(attend, v * cw))


def _nsa_cmp(q_c, kc_t, vc, overlap_t, n_slc, tq=256):
    s_len = q_c.shape[0]
    tq = min(tq, s_len)
    g, _, n_pad = kc_t.shape
    n_slc_pad = overlap_t.shape[0]
    gw = NSA_REP * HEAD_DIM
    cw = min(256, n_pad)
    assert n_pad % cw == 0, (n_pad, cw)
    return pl.pallas_call(
        functools.partial(_nsa_cmp_body, tq=tq, n_slc=n_slc, cw=cw),
        grid=(g, s_len // tq),
        in_specs=[pl.BlockSpec((tq, gw), lambda gi, qi: (qi, gi)),
                  pl.BlockSpec((1, HEAD_DIM, n_pad), lambda gi, qi: (gi, 0, 0)),
                  pl.BlockSpec((1, n_pad, HEAD_DIM), lambda gi, qi: (gi, 0, 0)),
                  pl.BlockSpec((n_slc_pad, n_pad), lambda gi, qi: (0, 0))],
        out_specs=[pl.BlockSpec((tq, gw), lambda gi, qi: (qi, gi)),
                   pl.BlockSpec((1, n_slc_pad, tq), lambda gi, qi: (gi, 0, qi))],
        out_shape=[jax.ShapeDtypeStruct((s_len, g * gw), F32),
                   jax.ShapeDtypeStruct((g, n_slc_pad, s_len), BF16)],
        compiler_params=_cparams(("parallel", "parallel")),
        name="nsa_cmp_select",
    )(q_c, kc_t, vc, overlap_t)


def _flash_t_body(qt_ref, bt_ref, k_ref, oh_ref, vt_ref, o_ref, qa_s, m_s, l_s, acc_s, s_s, *,
                  rep, tq, tk, keys_per_var, n_var, n_split):
    width = rep * tq
    qi = pl.program_id(1)
    q0 = qi * tq
    for var in range(n_var):
        bt = bt_ref[0, var * LANES:(var + 1) * LANES, :]
        qa_s[var, :HEAD_DIM, :] = jnp.concatenate(
            [qt_ref[r * HEAD_DIM:(r + 1) * HEAD_DIM, :] for r in range(rep)], axis=1)
        qa_s[var, HEAD_DIM:, :] = jnp.concatenate([bt] * rep, axis=1)
    m_s[...] = jnp.full(m_s.shape, NEG, F32)
    l_s[...] = jnp.zeros(l_s.shape, F32)
    acc_s[...] = jnp.zeros(acc_s.shape, F32)
    t1 = q0 + lax.broadcasted_iota(jnp.int32, (1, tq), 1)
    t = jnp.concatenate([t1] * rep, axis=1)
    cw = width // n_split

    def key_tile(kt):
        start = pl.multiple_of(kt * tk, tk)
        return jnp.concatenate([k_ref[pl.ds(start, tk), :], oh_ref[pl.ds(start, tk), :]], axis=1)

    def scores(k_tile, kt, c):
        var = (kt * tk) // keys_per_var if n_var > 1 else 0
        return jnp.dot(k_tile, qa_s[var, :, c * cw:(c + 1) * cw], preferred_element_type=F32)

    def consume(kt, c, s, causal):
        cs = slice(c * cw, (c + 1) * cw)
        if causal:
            pos = kt * tk + lax.broadcasted_iota(jnp.int32, (tk, 1), 0)
            s = jnp.where(pos <= t[:, cs], s, NEG)
        m_old = m_s[:, cs]
        m_new = jnp.maximum(m_old, jnp.max(s, axis=0, keepdims=True))
        alpha = jnp.exp2(m_old - m_new)
        p = jnp.exp2(s - m_new)
        l_s[:, cs] = alpha * l_s[:, cs] + jnp.sum(p, axis=0, keepdims=True)
        m_s[:, cs] = m_new
        return alpha, p.astype(BF16)

    def step(kt, causal, prefetch):
        v_t = vt_ref[0, kt]
        k_next = key_tile(kt + 1) if prefetch else None
        for c0 in range(0, n_split, 2):
            pair = (c0, c0 + 1)
            held = [s_s[:, c * cw:(c + 1) * cw] for c in pair]
            if prefetch:
                for c in pair:
                    s_s[:, c * cw:(c + 1) * cw] = scores(k_next, kt + 1, c)
            ready = [consume(kt, c, s, causal) for c, s in zip(pair, held)]
            for c, (alpha, p) in zip(pair, ready):
                cs = slice(c * cw, (c + 1) * cw)
                acc_s[:, cs] = alpha * acc_s[:, cs] + jnp.dot(v_t, p, preferred_element_type=F32)

    n_full = q0 // tk
    n_diag = max(1, tq // tk)
    k_first = key_tile(0)
    for c in range(n_split):
        s_s[:, c * cw:(c + 1) * cw] = scores(k_first, 0, c)

    def full_step(kt, carry):
        step(kt, False, True)
        return carry

    lax.fori_loop(0, n_full, full_step, 0)
    for d in range(n_diag):
        step(n_full + d, True, d < n_diag - 1)
    o_t = acc_s[...] / l_s[...]
    for r in range(rep):
        o_ref[:, r * HEAD_DIM:(r + 1) * HEAD_DIM] = o_t[:, r * tq:(r + 1) * tq].T.astype(o_ref.dtype)


def _flash_t(q_t, bias_t, k, block_onehot, v_t, *, rep, tq, keys_per_var, out_dtype, name, n_split):
    s_len = q_t.shape[1]
    g, n_kt, _, tk = v_t.shape
    gw = rep * HEAD_DIM
    width = rep * tq
    n_var = bias_t.shape[1] // LANES
    assert (tk % tq == 0 or tq % tk == 0) and n_split % 2 == 0 and width % (n_split * LANES) == 0, \
        (tq, tk, width, n_split)
    return pl.pallas_call(
        functools.partial(_flash_t_body, rep=rep, tq=tq, tk=tk, keys_per_var=keys_per_var,
                          n_var=n_var, n_split=n_split),
        grid=(g, s_len // tq),
        in_specs=[pl.BlockSpec((gw, tq), lambda gi, qi: (gi, qi)),
                  pl.BlockSpec((1, n_var * LANES, tq), lambda gi, qi: (gi, 0, qi)),
                  pl.BlockSpec((s_len, HEAD_DIM), lambda gi, qi: (0, gi)),
                  pl.BlockSpec((s_len, LANES), lambda gi, qi: (0, 0)),
                  pl.BlockSpec((1, n_kt, HEAD_DIM, tk), lambda gi, qi: (gi, 0, 0, 0))],
        out_specs=pl.BlockSpec((tq, gw), lambda gi, qi: (qi, gi)),
        out_shape=jax.ShapeDtypeStruct((s_len, g * gw), out_dtype),
        scratch_shapes=[pltpu.VMEM((n_var, 2 * HEAD_DIM, width), BF16),
                        pltpu.VMEM((1, width), F32),
                        pltpu.VMEM((1, width), F32),
                        pltpu.VMEM((HEAD_DIM, width), F32),
                        pltpu.VMEM((tk, width), F32)],
        compiler_params=_cparams(("parallel", "arbitrary")),
        name=name,
    )(q_t, bias_t, k, block_onehot, v_t)


def _moba_select_body(q_ref, km_ref, bt_ref, *, tq, n_blk):
    q0 = pl.program_id(1) * tq
    nt = (((1,), (1,)), ((), ()))
    sg = lax.dot_general(km_ref[0], q_ref[...], nt, preferred_element_type=F32)
    j = lax.broadcasted_iota(jnp.int32, sg.shape, 0)
    cur = (q0 + lax.broadcasted_iota(jnp.int32, (1, tq), 1)) // MOBA_BLOCK
    past = j < cur
    score = jnp.where(j < n_blk, jnp.where(past, sg, NEG), BELOW_NEG)
    sel = _topk_mask(score, j.astype(F32), min(MOBA_TOPK, n_blk), 0)
    bt_ref[0] = jnp.where((sel & past) | (j == cur), 0.0, NEG).astype(BF16)


def _moba_select(q, k_mean, n_blk, tq=512):
    s_len = q.shape[0]
    tq = min(tq, s_len)
    n_heads = k_mean.shape[0]
    return pl.pallas_call(
        functools.partial(_moba_select_body, tq=tq, n_blk=n_blk),
        grid=(n_heads, s_len // tq),
        in_specs=[pl.BlockSpec((tq, HEAD_DIM), lambda hi, qi: (qi, hi)),
                  pl.BlockSpec((1, LANES, HEAD_DIM), lambda hi, qi: (hi, 0, 0))],
        out_specs=pl.BlockSpec((1, LANES, tq), lambda hi, qi: (hi, 0, qi)),
        out_shape=jax.ShapeDtypeStruct((n_heads, LANES, s_len), BF16),
        compiler_params=_cparams(("parallel", "parallel")),
        name="moba_select",
    )(q, k_mean)


def _window_body(q_ref, kt_ref, v_ref, oc_ref, os_ref, g_ref, o_ref, *, tq):
    qi = pl.program_id(1)
    q0 = qi * tq
    n_past = WINDOW // tq
    q = jnp.concatenate([q_ref[:, r * HEAD_DIM:(r + 1) * HEAD_DIM] for r in range(NSA_REP)], axis=0)
    t1 = q0 + lax.broadcasted_iota(jnp.int32, (tq, 1), 0)
    t = jnp.concatenate([t1] * NSA_REP, axis=0)
    lane = lax.broadcasted_iota(jnp.int32, (1, tq), 1)
    ones = jnp.ones((tq, HEAD_DIM), BF16)
    scores, tiles = [], []
    for i in range(n_past + 1):
        raw = qi - n_past + i
        idx = jnp.maximum(raw, 0)
        tiles.append(idx)
        s = jnp.dot(q, kt_ref[0, idx], preferred_element_type=F32)
        pos = raw * tq + lane
        if i == 0:
            s = jnp.where(pos > t - WINDOW, s, NEG)
        if i == n_past:
            s = jnp.where(pos <= t, s, NEG)
        else:
            s = s + jnp.where(raw >= 0, 0.0, NEG)
        scores.append(s)
    m = jnp.max(scores[0], axis=-1, keepdims=True)
    for s in scores[1:]:
        m = jnp.maximum(m, jnp.max(s, axis=-1, keepdims=True))
    acc = jnp.zeros((NSA_REP * tq, 2 * HEAD_DIM), F32)
    for i, s in enumerate(scores):
        start = pl.multiple_of(tiles[i] * tq, tq)
        v_aug = jnp.concatenate([v_ref[pl.ds(start, tq), :], ones], axis=1)
        acc = acc + jnp.dot(jnp.exp2(s - m).astype(BF16), v_aug, preferred_element_type=F32)
    o_win = acc[:, :HEAD_DIM] / acc[:, HEAD_DIM:]
    gate = g_ref[0]
    for r in range(NSA_REP):
        sl = slice(r * HEAD_DIM, (r + 1) * HEAD_DIM)
        o = (gate[:, r:r + 1] * oc_ref[:, sl]
             + gate[:, NSA_REP + r:NSA_REP + r + 1] * os_ref[:, sl]
             + gate[:, 2 * NSA_REP + r:2 * NSA_REP + r + 1] * o_win[r * tq:(r + 1) * tq])
        o_ref[:, sl] = o.astype(o_ref.dtype)


def _window_combine(q_r, kw_t, v, v_block0, o_cmp, o_slc, gates):
    s_len = q_r.shape[0]
    g, n_kt, _, tq = kw_t.shape
    assert WINDOW % tq == 0, tq
    gw = NSA_REP * HEAD_DIM
    tile = pl.BlockSpec((tq, gw), lambda gi, qi: (qi, gi))
    return pl.pallas_call(
        functools.partial(_window_body, tq=tq),
        grid=(g, s_len // tq),
        in_specs=[tile,
                  pl.BlockSpec((1, n_kt, HEAD_DIM, tq), lambda gi, qi: (gi, 0, 0, 0)),
                  pl.BlockSpec((s_len, HEAD_DIM), lambda gi, qi: (0, v_block0 + gi)),
                  tile, tile,
                  pl.BlockSpec((1, tq, LANES), lambda gi, qi: (gi, qi, 0))],
        out_specs=tile,
        out_shape=jax.ShapeDtypeStruct((s_len, g * gw), BF16),
        compiler_params=_cparams(("parallel", "parallel")),
        name="nsa_window_combine",
    )(q_r, kw_t, v, o_cmp, o_slc, gates)


def _sgu_body(zu_ref, zv_ref, gain_ref, w_ref, bt_ref, o_ref, *, tm):
    gd = zu_ref.shape[1] // SGU_GROUPS
    row = lax.broadcasted_iota(jnp.int32, (SGU_CHUNK, SGU_CHUNK), 0)
    col = lax.broadcasted_iota(jnp.int32, (SGU_CHUNK, SGU_CHUNK), 1)
    bt = bt_ref[...]
    for g in range(SGU_GROUPS):
        sl = slice(g * gd, (g + 1) * gd)
        v = _gelu(zv_ref[:, sl])
        v = (v * lax.rsqrt(jnp.mean(v * v, axis=-1, keepdims=True) + NORM_EPS) * gain_ref[:, sl]).astype(BF16)
        w = jnp.where(col <= row, w_ref[g], 0.0).astype(BF16)
        for c in range(tm // SGU_CHUNK):
            rs = slice(c * SGU_CHUNK, (c + 1) * SGU_CHUNK)
            mixed = jnp.dot(w, v[rs], preferred_element_type=F32) + bt[:, g:g + 1]
            o_ref[rs, sl] = (_gelu(zu_ref[rs, sl]) * mixed).astype(o_ref.dtype)


def _sgu(z_uv, gain, w_s, b_s, tm=512):
    s_len = z_uv.shape[0]
    width = z_uv.shape[1] // 2
    tm = min(tm, s_len)
    return pl.pallas_call(
        functools.partial(_sgu_body, tm=tm),
        grid=(s_len // tm,),
        in_specs=[pl.BlockSpec((tm, width), lambda i: (i, 0)),
                  pl.BlockSpec((tm, width), lambda i: (i, 1)),
                  pl.BlockSpec((1, width), lambda i: (0, 0)),
                  pl.BlockSpec((SGU_GROUPS, SGU_CHUNK, SGU_CHUNK), lambda i: (0, 0, 0)),
                  pl.BlockSpec((SGU_CHUNK, SGU_GROUPS), lambda i: (0, 0))],
        out_specs=pl.BlockSpec((tm, width), lambda i: (i, 0)),
        out_shape=jax.ShapeDtypeStruct((s_len, width), BF16),
        compiler_params=_cparams(("parallel",)),
        name="sgu",
    )(z_uv, z_uv, gain.reshape(1, width), w_s, b_s.T)


def _merge_body(oa_ref, ob_ref, oc_ref, pa_ref, pb_ref, pc_ref, ga_ref, gb_ref, gc_ref, y_ref):
    y = ga_ref[...] * jnp.dot(oa_ref[...], pa_ref[...], preferred_element_type=F32)
    y = y + gb_ref[...] * jnp.dot(ob_ref[...], pb_ref[...], preferred_element_type=F32)
    y = y + gc_ref[...] * jnp.dot(oc_ref[...], pc_ref[...], preferred_element_type=F32)
    y_ref[...] = y.astype(y_ref.dtype)


def _merge(o_a, o_b, o_c, p_a, p_b, p_c, layer, gm, tm=512, tn=1024):
    s_len = o_a.shape[0]
    d = p_a.shape[2]
    tm, tn = min(tm, s_len), _tile(d, tn)
    nj = d // tn

    def rows(w):
        return pl.BlockSpec((tm, w), lambda i, j: (i, 0))

    def cols(kdim):
        return pl.BlockSpec((None, kdim, tn), lambda i, j: (layer, 0, j))

    def gate(off):
        return pl.BlockSpec((tm, tn), lambda i, j: (i, off * nj + j))

    return pl.pallas_call(
        _merge_body,
        grid=(s_len // tm, nj),
        in_specs=[rows(o_a.shape[1]), rows(o_b.shape[1]), rows(o_c.shape[1]),
                  cols(p_a.shape[1]), cols(p_b.shape[1]), cols(p_c.shape[1]),
                  gate(0), gate(1), gate(2)],
        out_specs=pl.BlockSpec((tm, tn), lambda i, j: (i, j)),
        out_shape=jax.ShapeDtypeStruct((s_len, d), BF16),
        compiler_params=_cparams(("parallel", "parallel")),
        name="gated_merge",
    )(o_a, o_b, o_c, p_a, p_b, p_c, gm, gm, gm)


def _block_onehot(s_len, block):
    key = np.arange(s_len)
    return jnp.asarray(((key // block) % LANES)[:, None] == np.arange(LANES)[None, :], dtype=BF16)


def _tiles_t(v, n_heads, tk):
    s_len = v.shape[0]
    return v.reshape(s_len // tk, tk, n_heads, HEAD_DIM).transpose(2, 0, 3, 1)


def _overlap_t(n_pad, n_slc_pad):
    i = np.arange(n_pad)[None, :]
    j = np.arange(n_slc_pad)[:, None]
    ov = (i * CMP_STRIDE <= j * SLC_LEN + SLC_LEN - 1) & (i * CMP_STRIDE + CMP_LEN - 1 >= j * SLC_LEN)
    return jnp.asarray(ov, dtype=BF16)


def _layer(x, cos, sin, p, big, layer):
    s_len, d_model = x.shape
    scale = HEAD_DIM ** -0.5 * math.log2(math.e)
    w_in = p["w_in"]
    nsa_w = big["proj_a"].shape[1]
    sgu_w = big["proj_b"].shape[1]
    moba_w = big["proj_c"].shape[1]
    n_heads = nsa_w // HEAD_DIM
    n_groups = n_heads // NSA_REP
    kv_w = n_groups * HEAD_DIM
    moba_heads = moba_w // HEAD_DIM
    sizes = (nsa_w, kv_w, kv_w, kv_w, kv_w, kv_w, kv_w, 3 * n_heads, sgu_w, sgu_w,
             moba_w, moba_w, moba_w, d_model, d_model, d_model)
    offs = np.concatenate([[0], np.cumsum(sizes)])

    def seg(a, b):
        return w_in[:, offs[a]:offs[b]].astype(BF16)

    def tile_gain(gain, reps):
        return jnp.tile(gain, reps).reshape(1, reps * HEAD_DIM)

    h = _rmsnorm(x, p["norm_mix"])
    rope_extras = (cos, sin)
    rope_specs = (_spec_rope, _spec_rope)

    q_c, q_r = _matmul(
        h, seg(0, 1), functools.partial(_ep_q, scale=scale),
        [jax.ShapeDtypeStruct((s_len, nsa_w), BF16)] * 2, [_spec_tile, _spec_tile],
        extras=(tile_gain(p["nsa_q_norm"], n_heads),) + rope_extras,
        extra_specs=(_spec_col,) + rope_specs, ts=HEAVY_EPILOGUE_TS, name="proj_nsa_q")
    (kcvc,) = _matmul(h, seg(1, 3), _ep_cast, [jax.ShapeDtypeStruct((s_len, 2 * kv_w), F32)],
                      [_spec_tile], name="proj_nsa_cmp_kv")
    (kskw,) = _matmul(
        h, jnp.concatenate([seg(3, 4), seg(5, 6)], axis=1), functools.partial(_ep_krot, block_mean=False),
        [jax.ShapeDtypeStruct((s_len, 2 * kv_w), BF16)], [_spec_tile],
        extras=(jnp.concatenate([tile_gain(p["nsa_ks_norm"], n_groups),
                                 tile_gain(p["nsa_kw_norm"], n_groups)], axis=1),) + rope_extras,
        extra_specs=(_spec_col,) + rope_specs, ts=HEAVY_EPILOGUE_TS, name="proj_nsa_k")
    (vsvw,) = _matmul(h, jnp.concatenate([seg(4, 5), seg(6, 7)], axis=1), _ep_cast,
                      [jax.ShapeDtypeStruct((s_len, 2 * kv_w), BF16)], [_spec_tile], name="proj_nsa_v")
    n_gate = 3 * n_heads
    w_gate = jnp.pad(seg(7, 8), ((0, 0), (0, LANES - n_gate)))
    b_gate = jnp.pad(p["nsa_gate_b"], (0, LANES - n_gate)).reshape(1, LANES)
    (gates,) = _matmul(h, w_gate, _ep_sigmoid_bias, [jax.ShapeDtypeStruct((s_len, LANES), F32)],
                       [_spec_tile], extras=(b_gate,), extra_specs=(_spec_col,), name="proj_nsa_gates")

    n_chunk = s_len // CMP_STRIDE
    chunks = kcvc.reshape(n_chunk, CMP_STRIDE, 2, n_groups, HEAD_DIM).transpose(2, 3, 0, 1, 4)
    chunks = chunks.reshape(2, n_groups, n_chunk, CMP_STRIDE * HEAD_DIM)
    kc = _compress(chunks[0], p["phi_pe_k"], p["phi_w1_k"], p["phi_w2_k"], p["nsa_kc_norm"], True)
    vc = _compress(chunks[1], p["phi_pe_v"], p["phi_w1_v"], p["phi_w2_v"], p["nsa_kc_norm"], False)

    n_slc = s_len // SLC_LEN
    n_slc_pad = -(-n_slc // LANES) * LANES
    o_cmp, sel_bias_t = _nsa_cmp(q_c, jnp.swapaxes(kc, 1, 2), vc, _overlap_t(n_chunk, n_slc_pad), n_slc)
    tk = min(512, s_len)
    o_slc = _flash_t(q_r.T, sel_bias_t, kskw, _block_onehot(s_len, SLC_LEN),
                     _tiles_t(vsvw[:, :kv_w], n_groups, tk), rep=NSA_REP, tq=min(512, s_len),
                     keys_per_var=LANES * SLC_LEN, out_dtype=F32, name="nsa_selected", n_split=8)
    kw_t = _tiles_t(kskw[:, kv_w:], n_groups, min(256, s_len))
    group_gates = gates[:, :n_gate].reshape(s_len, 3, n_groups, NSA_REP).transpose(2, 0, 1, 3)
    group_gates = jnp.pad(group_gates.reshape(n_groups, s_len, 3 * NSA_REP),
                          ((0, 0), (0, 0), (0, LANES - 3 * NSA_REP)))
    o_a = _window_combine(q_r, kw_t, vsvw, n_groups, o_cmp, o_slc, group_gates)

    (z_uv,) = _matmul(h, seg(8, 10), _ep_cast, [jax.ShapeDtypeStruct((s_len, 2 * sgu_w), F32)],
                      [_spec_tile], name="proj_sgu")
    o_b = _sgu(z_uv, p["sgu_norm"], p["sgu_w"], p["sgu_b"])

    (mq,) = _matmul(
        h, seg(10, 11), functools.partial(_ep_qrot, scale=scale),
        [jax.ShapeDtypeStruct((s_len, moba_w), BF16)], [_spec_tile],
        extras=(tile_gain(p["moba_q_norm"], moba_heads),) + rope_extras,
        extra_specs=(_spec_col,) + rope_specs, ts=HEAVY_EPILOGUE_TS, name="proj_moba_q")
    tm_k = min(1024, s_len)
    mk, mk_mean = _matmul(
        h, seg(11, 12), functools.partial(_ep_krot, block_mean=True),
        [jax.ShapeDtypeStruct((s_len, moba_w), BF16),
         jax.ShapeDtypeStruct((s_len // tm_k, tm_k // MOBA_BLOCK, moba_w), F32)],
        [_spec_tile, _spec_blockmean],
        extras=(tile_gain(p["moba_k_norm"], moba_heads),) + rope_extras,
        extra_specs=(_spec_col,) + rope_specs, tm=tm_k, ts=HEAVY_EPILOGUE_TS, name="proj_moba_k")
    (mv,) = _matmul(h, seg(12, 13), _ep_cast, [jax.ShapeDtypeStruct((s_len, moba_w), BF16)],
                    [_spec_tile], name="proj_moba_v")
    n_blk = s_len // MOBA_BLOCK
    assert n_blk <= LANES, n_blk
    k_mean = mk_mean.reshape(n_blk, moba_heads, HEAD_DIM).transpose(1, 0, 2)
    k_mean = jnp.pad(k_mean, ((0, 0), (0, LANES - n_blk), (0, 0))).astype(BF16)
    o_c = _flash_t(mq.T, _moba_select(mq, k_mean, n_blk), mk, _block_onehot(s_len, MOBA_BLOCK),
                   _tiles_t(mv, moba_heads, tk), rep=1, tq=min(1024, s_len),
                   keys_per_var=LANES * MOBA_BLOCK, out_dtype=BF16, name="moba", n_split=4)

    (gm,) = _matmul(h, seg(13, 16), _ep_sigmoid, [jax.ShapeDtypeStruct((s_len, 3 * d_model), F32)],
                    [_spec_tile], name="proj_merge_gates")
    y = _merge(o_a, o_b, o_c, big["proj_a"], big["proj_b"], big["proj_c"], layer, gm)
    (x,) = _matmul(y, big["w_out"], _ep_residual, [jax.ShapeDtypeStruct((s_len, d_model), F32)],
                   [_spec_tile], extras=(x,), extra_specs=(_spec_tile,), b_layer=layer, name="out_proj")

    h2 = _rmsnorm(x, p["norm_mlp"])
    (hid,) = _matmul(h2, big["mlp_w1"], _ep_relu2,
                     [jax.ShapeDtypeStruct((s_len, big["mlp_w1"].shape[2]), BF16)], [_spec_tile],
                     b_layer=layer, name="mlp_up")
    (x,) = _matmul(hid, big["mlp_w2"], _ep_residual, [jax.ShapeDtypeStruct((s_len, d_model), F32)],
                   [_spec_tile], extras=(x,), extra_specs=(_spec_tile,), b_layer=layer, name="mlp_down")
    return x


_LAYER_PARAMS = ("norm_mix", "norm_mlp", "w_in", "nsa_gate_b", "nsa_q_norm", "nsa_kc_norm", "nsa_ks_norm",
                 "nsa_kw_norm", "phi_pe_k", "phi_w1_k", "phi_w2_k", "phi_pe_v", "phi_w1_v", "phi_w2_v",
                 "sgu_norm", "sgu_w", "sgu_b", "moba_q_norm", "moba_k_norm", "proj_a", "proj_b", "proj_c",
                 "w_out", "mlp_w1", "mlp_w2")
_BIG_PARAMS = ("proj_a", "proj_b", "proj_c", "w_out", "mlp_w1", "mlp_w2")


def kernel(x, positions, norm_mix, norm_mlp, w_in, nsa_gate_b, nsa_q_norm, nsa_kc_norm, nsa_ks_norm, nsa_kw_norm, phi_pe_k, phi_w1_k, phi_w2_k, phi_pe_v, phi_w1_v, phi_w2_v, sgu_norm, sgu_w, sgu_b, moba_q_norm, moba_k_norm, proj_a, proj_b, proj_c, w_out, mlp_w1, mlp_w2):
    stacked = dict(zip(_LAYER_PARAMS, (norm_mix, norm_mlp, w_in, nsa_gate_b, nsa_q_norm, nsa_kc_norm,
                                       nsa_ks_norm, nsa_kw_norm, phi_pe_k, phi_w1_k, phi_w2_k, phi_pe_v,
                                       phi_w1_v, phi_w2_v, sgu_norm, sgu_w, sgu_b, moba_q_norm, moba_k_norm,
                                       proj_a, proj_b, proj_c, w_out, mlp_w1, mlp_w2)))
    depth = w_in.shape[0]
    big = {k: stacked.pop(k).astype(BF16) for k in _BIG_PARAMS}
    inv = ROPE_THETA ** (-jnp.arange(0, HEAD_DIM, 2, dtype=F32) / HEAD_DIM)
    outs = []
    for b in range(x.shape[0]):
        ang = positions[b].astype(F32)[:, None] * inv
        cos = jnp.concatenate([jnp.cos(ang), jnp.cos(ang)], axis=-1)
        sin = jnp.concatenate([-jnp.sin(ang), jnp.sin(ang)], axis=-1)
        xb = x[b]
        for l in range(depth):
            xb = _layer(xb, cos, sin, {k: v[l] for k, v in stacked.items()}, big, l)
        outs.append(xb)
    return jnp.stack(outs)
```

```python
import functools
import math

import jax
import jax.numpy as jnp
import numpy as np
from jax import lax
from jax.experimental import pallas as pl
from jax.experimental.pallas import tpu as pltpu

F32 = jnp.float32
BF16 = jnp.bfloat16

HEAD_DIM = 128
LANES = 128
ROPE_THETA = 10000.0
NORM_EPS = 1e-6
NEG = -1e30
FORCED = 1e9
BELOW_NEG = -3e38

NSA_REP = 4
CMP_LEN = 32
CMP_STRIDE = 16
SLC_LEN = 64
SLC_TOPK = 16
WINDOW = 512
SGU_GROUPS = 8
SGU_CHUNK = 128
MOBA_BLOCK = 256
MOBA_TOPK = 3

LIGHT_EPILOGUE_TS = 512
HEAVY_EPILOGUE_TS = 1024

MIB = 1024 * 1024
VMEM_LIMIT = 52 * MIB


def _cparams(sem, vmem=VMEM_LIMIT):
    return pltpu.CompilerParams(dimension_semantics=sem, vmem_limit_bytes=vmem)


def _tile(n, pref):
    if n <= pref:
        return n
    t = (pref // LANES) * LANES
    while t >= LANES:
        if n % t == 0:
            return t
        t -= LANES
    raise ValueError(f"no 128-multiple tile divides {n}")


def _gelu(x):
    c = math.sqrt(2.0 / math.pi)
    return 0.5 * x * (1.0 + jnp.tanh(c * (x + 0.044715 * (x * x * x))))


def _sigmoid(x):
    return 1.0 / (1.0 + jnp.exp(-x))


def _head_norm(x, gain):
    return x * lax.rsqrt(jnp.mean(x * x, axis=-1, keepdims=True) + NORM_EPS) * gain


def _rope(x, cos, sin_signed):
    return x * cos + pltpu.roll(x, HEAD_DIM // 2, 1) * sin_signed


def _rmsnorm_body(x_ref, g_ref, o_ref):
    x = x_ref[...]
    y = x * lax.rsqrt(jnp.mean(x * x, axis=-1, keepdims=True) + NORM_EPS)
    o_ref[...] = (y * g_ref[...]).astype(o_ref.dtype)


def _rmsnorm(x, gain, tm=256):
    m, d = x.shape
    tm = min(tm, m)
    return pl.pallas_call(
        _rmsnorm_body,
        grid=(m // tm,),
        in_specs=[pl.BlockSpec((tm, d), lambda i: (i, 0)),
                  pl.BlockSpec((1, d), lambda i: (0, 0))],
        out_specs=pl.BlockSpec((tm, d), lambda i: (i, 0)),
        out_shape=jax.ShapeDtypeStruct((m, d), BF16),
        compiler_params=_cparams(("parallel",)),
        name="rmsnorm",
    )(x, gain.reshape(1, d))


def _mm_body(*refs, n_extra, n_out, nk, tn, ts, epilogue):
    a_ref, b_ref = refs[0], refs[1]
    extra = refs[2:2 + n_extra]
    outs = refs[2 + n_extra:2 + n_extra + n_out]
    acc_ref = refs[-1] if nk > 1 else None

    def finish():
        for c0 in range(0, tn, ts):
            part = jnp.dot(a_ref[...], b_ref[:, c0:c0 + ts], preferred_element_type=F32)
            if nk > 1:
                part = part + acc_ref[:, c0:c0 + ts]
            epilogue(part, extra, outs, slice(c0, c0 + ts))

    if nk == 1:
        finish()
        return
    k = pl.program_id(2)
    split_finish = ts < tn

    @pl.when(k == 0)
    def _():
        acc_ref[...] = jnp.dot(a_ref[...], b_ref[...], preferred_element_type=F32)

    @pl.when((k > 0) & (k < nk - 1) if split_finish else k > 0)
    def _():
        acc_ref[...] += jnp.dot(a_ref[...], b_ref[...], preferred_element_type=F32)

    @pl.when(k == nk - 1)
    def _():
        if split_finish:
            finish()
        else:
            epilogue(acc_ref[...], extra, outs, slice(0, tn))


def _matmul(a, b, epilogue, out_shapes, out_specs, extras=(), extra_specs=(),
            tm=1024, tn=1024, tk=2048, ts=LIGHT_EPILOGUE_TS, b_layer=0, name="matmul"):
    m, kdim = a.shape
    n = b.shape[-1]
    tm, tn, tk = min(tm, m), _tile(n, tn), _tile(kdim, tk)
    ts = min(ts, tn)
    if b.ndim == 3:
        b_spec = pl.BlockSpec((None, tk, tn), lambda i, j, k: (b_layer, k, j))
    else:
        b_spec = pl.BlockSpec((tk, tn), lambda i, j, k: (k, j))
    nk = kdim // tk
    body = functools.partial(_mm_body, n_extra=len(extras), n_out=len(out_shapes), nk=nk, tn=tn, ts=ts,
                             epilogue=epilogue)
    scratch = [] if nk == 1 else [pltpu.VMEM((tm, tn), F32)]
    return pl.pallas_call(
        body,
        grid=(m // tm, n // tn, nk),
        in_specs=[pl.BlockSpec((tm, tk), lambda i, j, k: (i, k)),
                  b_spec]
                 + [s(tm, tn) for s in extra_specs],
        out_specs=[s(tm, tn) for s in out_specs],
        out_shape=out_shapes,
        scratch_shapes=scratch,
        compiler_params=_cparams(("parallel", "parallel", "arbitrary")),
        name=name,
    )(a, b, *extras)


def _spec_tile(tm, tn):
    return pl.BlockSpec((tm, tn), lambda i, j, k: (i, j))


def _spec_col(tm, tn):
    return pl.BlockSpec((1, tn), lambda i, j, k: (0, j))


def _spec_rope(tm, tn):
    return pl.BlockSpec((tm, HEAD_DIM), lambda i, j, k: (i, 0))


def _spec_blockmean(tm, tn):
    return pl.BlockSpec((1, tm // MOBA_BLOCK, tn), lambda i, j, k: (i, 0, j))


def _ep_cast(acc, extra, outs, cols):
    outs[0][:, cols] = acc.astype(outs[0].dtype)


def _ep_sigmoid_bias(acc, extra, outs, cols):
    outs[0][:, cols] = _sigmoid(acc + extra[0][:, cols])


def _ep_sigmoid(acc, extra, outs, cols):
    outs[0][:, cols] = _sigmoid(acc)


def _ep_relu2(acc, extra, outs, cols):
    r = jnp.maximum(acc, 0.0)
    outs[0][:, cols] = (r * r).astype(outs[0].dtype)


def _ep_residual(acc, extra, outs, cols):
    outs[0][:, cols] = extra[0][:, cols] + acc


def _heads(cols):
    return [(slice(c - cols.start, c - cols.start + HEAD_DIM), slice(c, c + HEAD_DIM))
            for c in range(cols.start, cols.stop, HEAD_DIM)]


def _ep_q(acc, extra, outs, cols, *, scale):
    cos, sin = extra[1][...], extra[2][...]
    for a_sl, t_sl in _heads(cols):
        y = _head_norm(acc[:, a_sl], extra[0][:, t_sl])
        outs[0][:, t_sl] = (y * scale).astype(BF16)
        outs[1][:, t_sl] = (_rope(y, cos, sin) * scale).astype(BF16)


def _ep_qrot(acc, extra, outs, cols, *, scale):
    cos, sin = extra[1][...], extra[2][...]
    for a_sl, t_sl in _heads(cols):
        y = _head_norm(acc[:, a_sl], extra[0][:, t_sl])
        outs[0][:, t_sl] = (_rope(y, cos, sin) * scale).astype(BF16)


def _ep_krot(acc, extra, outs, cols, *, block_mean):
    cos, sin = extra[1][...], extra[2][...]
    for a_sl, t_sl in _heads(cols):
        y = _rope(_head_norm(acc[:, a_sl], extra[0][:, t_sl]), cos, sin)
        outs[0][:, t_sl] = y.astype(BF16)
        if block_mean:
            for blk in range(acc.shape[0] // MOBA_BLOCK):
                rows = y[blk * MOBA_BLOCK:(blk + 1) * MOBA_BLOCK]
                outs[1][0, blk:blk + 1, t_sl] = jnp.mean(rows, axis=0, keepdims=True)


def _compress_body(a_ref, pe_ref, w1_ref, w2_ref, g_ref, o_ref, *, norm):
    half = CMP_STRIDE * HEAD_DIM
    a = a_ref[0]
    n_chunk = a.shape[0]
    x1 = (a + pe_ref[:, :half]).astype(BF16)
    x2 = (a + pe_ref[:, half:]).astype(BF16)
    p1 = jnp.dot(x1, w1_ref[:half, :], preferred_element_type=F32)
    p2 = jnp.dot(x2, w1_ref[half:, :], preferred_element_type=F32)
    h = _gelu(p1 + pltpu.roll(p2, n_chunk - 1, 0))
    o = jnp.dot(h.astype(BF16), w2_ref[...], preferred_element_type=F32)
    if norm:
        o = _head_norm(o, g_ref[...])
    o_ref[0] = o.astype(o_ref.dtype)


def _compress(a, pe, w1, w2, gain, norm):
    g, n_chunk, width = a.shape
    hidden = w1.shape[1]
    return pl.pallas_call(
        functools.partial(_compress_body, norm=norm),
        grid=(g,),
        in_specs=[pl.BlockSpec((1, n_chunk, width), lambda i: (i, 0, 0)),
                  pl.BlockSpec((1, 2 * width), lambda i: (0, 0)),
                  pl.BlockSpec((2 * width, hidden), lambda i: (0, 0)),
                  pl.BlockSpec((hidden, HEAD_DIM), lambda i: (0, 0)),
                  pl.BlockSpec((1, HEAD_DIM), lambda i: (0, 0))],
        out_specs=pl.BlockSpec((1, n_chunk, HEAD_DIM), lambda i: (i, 0, 0)),
        out_shape=jax.ShapeDtypeStruct((g, n_chunk, HEAD_DIM), BF16),
        compiler_params=_cparams(("parallel",)),
        name="nsa_compress",
    )(a, pe.reshape(1, 2 * width), w1.astype(BF16), w2.astype(BF16), gain.reshape(1, HEAD_DIM))


def _topk_mask(score, index_f, k, axis):
    n = score.shape[axis]
    sel = jnp.zeros(score.shape, dtype=jnp.bool_)
    for _ in range(k):
        m = jnp.max(score, axis=axis, keepdims=True)
        first = jnp.min(jnp.where(score == m, index_f, float(n)), axis=axis, keepdims=True)
        hit = index_f == first
        sel = jnp.logical_or(sel, hit)
        score = jnp.where(hit, BELOW_NEG, score)
    return sel


def _nsa_cmp_body(q_ref, kt_ref, v_ref, ovt_ref, o_ref, sbt_ref, *, tq, n_slc, cw):
    qi = pl.program_id(1)
    q0 = qi * tq
    n_pad = kt_ref.shape[-1]
    q = jnp.concatenate([q_ref[:, r * HEAD_DIM:(r + 1) * HEAD_DIM] for r in range(NSA_REP)], axis=0)
    t1 = q0 + lax.broadcasted_iota(jnp.int32, (tq, 1), 0)
    t = jnp.concatenate([t1] * NSA_REP, axis=0)

    def attend(width):
        s = jnp.dot(q, kt_ref[0, :, :width], preferred_element_type=F32)
        cmp_end = lax.broadcasted_iota(jnp.int32, (1, width), 1) * CMP_STRIDE + (CMP_LEN - 1)
        s = jnp.where(cmp_end <= t, s, NEG)
        m = jnp.max(s, axis=-1, keepdims=True)
        e = jnp.exp2(s - m)
        inv = jnp.where(m > 0.5 * NEG, 1.0 / jnp.sum(e, axis=-1, keepdims=True), 0.0)
        p = e * inv
        o = jnp.dot(p.astype(BF16), v_ref[0, :width, :], preferred_element_type=F32)
        for r in range(NSA_REP):
            o_ref[:, r * HEAD_DIM:(r + 1) * HEAD_DIM] = o[r * tq:(r + 1) * tq]
        ps = p[0:tq]
        for r in range(1, NSA_REP):
            ps = ps + p[r * tq:(r + 1) * tq]
        ps_hi = ps.astype(BF16)
        ps_lo = (ps - ps_hi.astype(F32)).astype(BF16)
        nt = (((1,), (1,)), ((), ()))
        n_rows = width * CMP_STRIDE // SLC_LEN
        ovt = ovt_ref[:n_rows, :width]
        imp = (lax.dot_general(ovt, ps_hi, nt, preferred_element_type=F32)
               + lax.dot_general(ovt, ps_lo, nt, preferred_element_type=F32))
        j = lax.broadcasted_iota(jnp.int32, imp.shape, 0)
        cur = (q0 + lax.broadcasted_iota(jnp.int32, (1, tq), 1)) // SLC_LEN
        forced = (j == 0) | (j == cur) | (j == cur - 1)
        allowed = j <= cur
        score = jnp.where(allowed & jnp.logical_not(forced), imp, NEG)
        score = jnp.where(j < n_slc, score, BELOW_NEG)
        sel = _topk_mask(score, j.astype(F32), min(SLC_TOPK, n_slc) - 3, 0)
        sbt_ref[0, :n_rows, :] = jnp.where((sel | forced) & allowed, 0.0, NEG).astype(BF16)
        if n_rows < sbt_ref.shape[1]:
            sbt_ref[0, n_rows:, :] = jnp.full((sbt_ref.shape[1] - n_rows, tq), NEG, BF16)

    n_chunks = ((q0 + tq) // CMP_STRIDE - 1 + cw - 1) // cw
    for v in range(1, n_pad // cw + 1):
        pl.when(n_chunks == v)(functools.partial(attend, v * cw))


def _nsa_cmp(q_c, kc_t, vc, overlap_t, n_slc, tq=256):
    s_len = q_c.shape[0]
    tq = min(tq, s_len)
    g, _, n_pad = kc_t.shape
    n_slc_pad = overlap_t.shape[0]
    gw = NSA_REP * HEAD_DIM
    cw = min(256, n_pad)
    assert n_pad % cw == 0, (n_pad, cw)
    return pl.pallas_call(
        functools.partial(_nsa_cmp_body, tq=tq, n_slc=n_slc, cw=cw),
        grid=(g, s_len // tq),
        in_specs=[pl.BlockSpec((tq, gw), lambda gi, qi: (qi, gi)),
                  pl.BlockSpec((1, HEAD_DIM, n_pad), lambda gi, qi: (gi, 0, 0)),
                  pl.BlockSpec((1, n_pad, HEAD_DIM), lambda gi, qi: (gi, 0, 0)),
                  pl.BlockSpec((n_slc_pad, n_pad), lambda gi, qi: (0, 0))],
        out_specs=[pl.BlockSpec((tq, gw), lambda gi, qi: (qi, gi)),
                   pl.BlockSpec((1, n_slc_pad, tq), lambda gi, qi: (gi, 0, qi))],
        out_shape=[jax.ShapeDtypeStruct((s_len, g * gw), F32),
                   jax.ShapeDtypeStruct((g, n_slc_pad, s_len), BF16)],
        compiler_params=_cparams(("parallel", "parallel")),
        name="nsa_cmp_select",
    )(q_c, kc_t, vc, overlap_t)


def _flash_t_body(qt_ref, bt_ref, k_ref, oh_ref, vt_ref, o_ref, qa_s, m_s, l_s, acc_s, s_s, *,
                  rep, tq, tk, keys_per_var, n_var, n_split):
    width = rep * tq
    qi = pl.program_id(1)
    q0 = qi * tq
    for var in range(n_var):
        bt = bt_ref[0, var * LANES:(var + 1) * LANES, :]
        qa_s[var, :HEAD_DIM, :] = jnp.concatenate(
            [qt_ref[r * HEAD_DIM:(r + 1) * HEAD_DIM, :] for r in range(rep)], axis=1)
        qa_s[var, HEAD_DIM:, :] = jnp.concatenate([bt] * rep, axis=1)
    m_s[...] = jnp.full(m_s.shape, NEG, F32)
    l_s[...] = jnp.zeros(l_s.shape, F32)
    acc_s[...] = jnp.zeros(acc_s.shape, F32)
    t1 = q0 + lax.broadcasted_iota(jnp.int32, (1, tq), 1)
    t = jnp.concatenate([t1] * rep, axis=1)
    cw = width // n_split

    def key_tile(kt):
        start = pl.multiple_of(kt * tk, tk)
        return jnp.concatenate([k_ref[pl.ds(start, tk), :], oh_ref[pl.ds(start, tk), :]], axis=1)

    def scores(k_tile, kt, c):
        var = (kt * tk) // keys_per_var if n_var > 1 else 0
        return jnp.dot(k_tile, qa_s[var, :, c * cw:(c + 1) * cw], preferred_element_type=F32)

    def consume(kt, c, s, causal):
        cs = slice(c * cw, (c + 1) * cw)
        if causal:
            pos = kt * tk + lax.broadcasted_iota(jnp.int32, (tk, 1), 0)
            s = jnp.where(pos <= t[:, cs], s, NEG)
        m_old = m_s[:, cs]
        m_new = jnp.maximum(m_old, jnp.max(s, axis=0, keepdims=True))
        alpha = jnp.exp2(m_old - m_new)
        p = jnp.exp2(s - m_new)
        l_s[:, cs] = alpha * l_s[:, cs] + jnp.sum(p, axis=0, keepdims=True)
        m_s[:, cs] = m_new
        return alpha, p.astype(BF16)

    def step(kt, causal, prefetch, rd):
        v_t = vt_ref[0, kt]
        if prefetch:
            k_next = key_tile(kt + 1)
            for c in range(n_split):
                s_s[1 - rd, :, c * cw:(c + 1) * cw] = scores(k_next, kt + 1, c)
        for c in range(n_split):
            cs = slice(c * cw, (c + 1) * cw)
            alpha, p = consume(kt, c, s_s[rd, :, cs], causal)
            acc_s[:, cs] = alpha * acc_s[:, cs] + jnp.dot(v_t, p, preferred_element_type=F32)

    n_full = q0 // tk
    n_diag = max(1, tq // tk)
    k_first = key_tile(0)
    for c in range(n_split):
        s_s[0, :, c * cw:(c + 1) * cw] = scores(k_first, 0, c)

    def two_steps(j, carry):
        step(2 * j, False, True, 0)
        step(2 * j + 1, False, True, 1)
        return carry

    def diagonal(rd):
        for d in range(n_diag):
            step(n_full + d, True, d < n_diag - 1, (rd + d) % 2)

    lax.fori_loop(0, n_full // 2, two_steps, 0)

    @pl.when(n_full % 2 == 1)
    def _():
        step(n_full - 1, False, True, 0)
        diagonal(1)

    @pl.when(n_full % 2 == 0)
    def _():
        diagonal(0)

    o_t = acc_s[...] / l_s[...]
    for r in range(rep):
        o_ref[:, r * HEAD_DIM:(r + 1) * HEAD_DIM] = o_t[:, r * tq:(r + 1) * tq].T.astype(o_ref.dtype)


def _flash_t(q_t, bias_t, k, block_onehot, v_t, *, rep, tq, keys_per_var, out_dtype, name, n_split):
    s_len = q_t.shape[1]
    g, n_kt, _, tk = v_t.shape
    gw = rep * HEAD_DIM
    width = rep * tq
    n_var = bias_t.shape[1] // LANES
    assert (tk % tq == 0 or tq % tk == 0) and n_split % 2 == 0 and width % (n_split * LANES) == 0, \
        (tq, tk, width, n_split)
    return pl.pallas_call(
        functools.partial(_flash_t_body, rep=rep, tq=tq, tk=tk, keys_per_var=keys_per_var,
                          n_var=n_var, n_split=n_split),
        grid=(g, s_len // tq),
        in_specs=[pl.BlockSpec((gw, tq), lambda gi, qi: (gi, qi)),
                  pl.BlockSpec((1, n_var * LANES, tq), lambda gi, qi: (gi, 0, qi)),
                  pl.BlockSpec((s_len, HEAD_DIM), lambda gi, qi: (0, gi)),
                  pl.BlockSpec((s_len, LANES), lambda gi, qi: (0, 0)),
                  pl.BlockSpec((1, n_kt, HEAD_DIM, tk), lambda gi, qi: (gi, 0, 0, 0))],
        out_specs=pl.BlockSpec((tq, gw), lambda gi, qi: (qi, gi)),
        out_shape=jax.ShapeDtypeStruct((s_len, g * gw), out_dtype),
        scratch_shapes=[pltpu.VMEM((n_var, 2 * HEAD_DIM, width), BF16),
                        pltpu.VMEM((1, width), F32),
                        pltpu.VMEM((1, width), F32),
                        pltpu.VMEM((HEAD_DIM, width), F32),
                        pltpu.VMEM((2, tk, width), F32)],
        compiler_params=_cparams(("parallel", "arbitrary")),
        name=name,
    )(q_t, bias_t, k, block_onehot, v_t)


def _moba_select_body(q_ref, km_ref, bt_ref, *, tq, n_blk):
    q0 = pl.program_id(1) * tq
    nt = (((1,), (1,)), ((), ()))
    sg = lax.dot_general(km_ref[0], q_ref[...], nt, preferred_element_type=F32)
    j = lax.broadcasted_iota(jnp.int32, sg.shape, 0)
    cur = (q0 + lax.broadcasted_iota(jnp.int32, (1, tq), 1)) // MOBA_BLOCK
    past = j < cur
    score = jnp.where(j < n_blk, jnp.where(past, sg, NEG), BELOW_NEG)
    sel = _topk_mask(score, j.astype(F32), min(MOBA_TOPK, n_blk), 0)
    bt_ref[0] = jnp.where((sel & past) | (j == cur), 0.0, NEG).astype(BF16)


def _moba_select(q, k_mean, n_blk, tq=512):
    s_len = q.shape[0]
    tq = min(tq, s_len)
    n_heads = k_mean.shape[0]
    return pl.pallas_call(
        functools.partial(_moba_select_body, tq=tq, n_blk=n_blk),
        grid=(n_heads, s_len // tq),
        in_specs=[pl.BlockSpec((tq, HEAD_DIM), lambda hi, qi: (qi, hi)),
                  pl.BlockSpec((1, LANES, HEAD_DIM), lambda hi, qi: (hi, 0, 0))],
        out_specs=pl.BlockSpec((1, LANES, tq), lambda hi, qi: (hi, 0, qi)),
        out_shape=jax.ShapeDtypeStruct((n_heads, LANES, s_len), BF16),
        compiler_params=_cparams(("parallel", "parallel")),
        name="moba_select",
    )(q, k_mean)


def _window_body(q_ref, kt_ref, v_ref, oc_ref, os_ref, g_ref, o_ref, *, tq):
    qi = pl.program_id(1)
    q0 = qi * tq
    n_past = WINDOW // tq
    q = jnp.concatenate([q_ref[:, r * HEAD_DIM:(r + 1) * HEAD_DIM] for r in range(NSA_REP)], axis=0)
    t1 = q0 + lax.broadcasted_iota(jnp.int32, (tq, 1), 0)
    t = jnp.concatenate([t1] * NSA_REP, axis=0)
    lane = lax.broadcasted_iota(jnp.int32, (1, tq), 1)
    ones = jnp.ones((tq, HEAD_DIM), BF16)
    scores, tiles = [], []
    for i in range(n_past + 1):
        raw = qi - n_past + i
        idx = jnp.maximum(raw, 0)
        tiles.append(idx)
        s = jnp.dot(q, kt_ref[0, idx], preferred_element_type=F32)
        pos = raw * tq + lane
        if i == 0:
            s = jnp.where(pos > t - WINDOW, s, NEG)
        if i == n_past:
            s = jnp.where(pos <= t, s, NEG)
        else:
            s = s + jnp.where(raw >= 0, 0.0, NEG)
        scores.append(s)
    m = jnp.max(scores[0], axis=-1, keepdims=True)
    for s in scores[1:]:
        m = jnp.maximum(m, jnp.max(s, axis=-1, keepdims=True))
    acc = jnp.zeros((NSA_REP * tq, 2 * HEAD_DIM), F32)
    for i, s in enumerate(scores):
        start = pl.multiple_of(tiles[i] * tq, tq)
        v_aug = jnp.concatenate([v_ref[pl.ds(start, tq), :], ones], axis=1)
        acc = acc + jnp.dot(jnp.exp2(s - m).astype(BF16), v_aug, preferred_element_type=F32)
    o_win = acc[:, :HEAD_DIM] / acc[:, HEAD_DIM:]
    gate = g_ref[0]
    for r in range(NSA_REP):
        sl = slice(r * HEAD_DIM, (r + 1) * HEAD_DIM)
        o = (gate[:, r:r + 1] * oc_ref[:, sl]
             + gate[:, NSA_REP + r:NSA_REP + r + 1] * os_ref[:, sl]
             + gate[:, 2 * NSA_REP + r:2 * NSA_REP + r + 1] * o_win[r * tq:(r + 1) * tq])
        o_ref[:, sl] = o.astype(o_ref.dtype)


def _window_combine(q_r, kw_t, v, v_block0, o_cmp, o_slc, gates):
    s_len = q_r.shape[0]
    g, n_kt, _, tq = kw_t.shape
    assert WINDOW % tq == 0, tq
    gw = NSA_REP * HEAD_DIM
    tile = pl.BlockSpec((tq, gw), lambda gi, qi: (qi, gi))
    return pl.pallas_call(
        functools.partial(_window_body, tq=tq),
        grid=(g, s_len // tq),
        in_specs=[tile,
                  pl.BlockSpec((1, n_kt, HEAD_DIM, tq), lambda gi, qi: (gi, 0, 0, 0)),
                  pl.BlockSpec((s_len, HEAD_DIM), lambda gi, qi: (0, v_block0 + gi)),
                  tile, tile,
                  pl.BlockSpec((1, tq, LANES), lambda gi, qi: (gi, qi, 0))],
        out_specs=tile,
        out_shape=jax.ShapeDtypeStruct((s_len, g * gw), BF16),
        compiler_params=_cparams(("parallel", "parallel")),
        name="nsa_window_combine",
    )(q_r, kw_t, v, o_cmp, o_slc, gates)


def _sgu_body(zu_ref, zv_ref, gain_ref, w_ref, bt_ref, o_ref, *, tm):
    gd = zu_ref.shape[1] // SGU_GROUPS
    row = lax.broadcasted_iota(jnp.int32, (SGU_CHUNK, SGU_CHUNK), 0)
    col = lax.broadcasted_iota(jnp.int32, (SGU_CHUNK, SGU_CHUNK), 1)
    bt = bt_ref[...]
    for g in range(SGU_GROUPS):
        sl = slice(g * gd, (g + 1) * gd)
        v = _gelu(zv_ref[:, sl])
        v = (v * lax.rsqrt(jnp.mean(v * v, axis=-1, keepdims=True) + NORM_EPS) * gain_ref[:, sl]).astype(BF16)
        w = jnp.where(col <= row, w_ref[g], 0.0).astype(BF16)
        for c in range(tm // SGU_CHUNK):
            rs = slice(c * SGU_CHUNK, (c + 1) * SGU_CHUNK)
            mixed = jnp.dot(w, v[rs], preferred_element_type=F32) + bt[:, g:g + 1]
            o_ref[rs, sl] = (_gelu(zu_ref[rs, sl]) * mixed).astype(o_ref.dtype)


def _sgu(z_uv, gain, w_s, b_s, tm=512):
    s_len = z_uv.shape[0]
    width = z_uv.shape[1] // 2
    tm = min(tm, s_len)
    return pl.pallas_call(
        functools.partial(_sgu_body, tm=tm),
        grid=(s_len // tm,),
        in_specs=[pl.BlockSpec((tm, width), lambda i: (i, 0)),
                  pl.BlockSpec((tm, width), lambda i: (i, 1)),
                  pl.BlockSpec((1, width), lambda i: (0, 0)),
                  pl.BlockSpec((SGU_GROUPS, SGU_CHUNK, SGU_CHUNK), lambda i: (0, 0, 0)),
                  pl.BlockSpec((SGU_CHUNK, SGU_GROUPS), lambda i: (0, 0))],
        out_specs=pl.BlockSpec((tm, width), lambda i: (i, 0)),
        out_shape=jax.ShapeDtypeStruct((s_len, width), BF16),
        compiler_params=_cparams(("parallel",)),
        name="sgu",
    )(z_uv, z_uv, gain.reshape(1, width), w_s, b_s.T)


def _merge_body(oa_ref, ob_ref, oc_ref, pa_ref, pb_ref, pc_ref, ga_ref, gb_ref, gc_ref, y_ref):
    y = ga_ref[...] * jnp.dot(oa_ref[...], pa_ref[...], preferred_element_type=F32)
    y = y + gb_ref[...] * jnp.dot(ob_ref[...], pb_ref[...], preferred_element_type=F32)
    y = y + gc_ref[...] * jnp.dot(oc_ref[...], pc_ref[...], preferred_element_type=F32)
    y_ref[...] = y.astype(y_ref.dtype)


def _merge(o_a, o_b, o_c, p_a, p_b, p_c, layer, gm, tm=512, tn=1024):
    s_len = o_a.shape[0]
    d = p_a.shape[2]
    tm, tn = min(tm, s_len), _tile(d, tn)
    nj = d // tn

    def rows(w):
        return pl.BlockSpec((tm, w), lambda i, j: (i, 0))

    def cols(kdim):
        return pl.BlockSpec((None, kdim, tn), lambda i, j: (layer, 0, j))

    def gate(off):
        return pl.BlockSpec((tm, tn), lambda i, j: (i, off * nj + j))

    return pl.pallas_call(
        _merge_body,
        grid=(s_len // tm, nj),
        in_specs=[rows(o_a.shape[1]), rows(o_b.shape[1]), rows(o_c.shape[1]),
                  cols(p_a.shape[1]), cols(p_b.shape[1]), cols(p_c.shape[1]),
                  gate(0), gate(1), gate(2)],
        out_specs=pl.BlockSpec((tm, tn), lambda i, j: (i, j)),
        out_shape=jax.ShapeDtypeStruct((s_len, d), BF16),
        compiler_params=_cparams(("parallel", "parallel")),
        name="gated_merge",
    )(o_a, o_b, o_c, p_a, p_b, p_c, gm, gm, gm)


def _block_onehot(s_len, block):
    key = np.arange(s_len)
    return jnp.asarray(((key // block) % LANES)[:, None] == np.arange(LANES)[None, :], dtype=BF16)


def _tiles_t(v, n_heads, tk):
    s_len = v.shape[0]
    return v.reshape(s_len // tk, tk, n_heads, HEAD_DIM).transpose(2, 0, 3, 1)


def _overlap_t(n_pad, n_slc_pad):
    i = np.arange(n_pad)[None, :]
    j = np.arange(n_slc_pad)[:, None]
    ov = (i * CMP_STRIDE <= j * SLC_LEN + SLC_LEN - 1) & (i * CMP_STRIDE + CMP_LEN - 1 >= j * SLC_LEN)
    return jnp.asarray(ov, dtype=BF16)


def _layer(x, cos, sin, p, big, layer):
    s_len, d_model = x.shape
    scale = HEAD_DIM ** -0.5 * math.log2(math.e)
    w_in = p["w_in"]
    nsa_w = big["proj_a"].shape[1]
    sgu_w = big["proj_b"].shape[1]
    moba_w = big["proj_c"].shape[1]
    n_heads = nsa_w // HEAD_DIM
    n_groups = n_heads // NSA_REP
    kv_w = n_groups * HEAD_DIM
    moba_heads = moba_w // HEAD_DIM
    sizes = (nsa_w, kv_w, kv_w, kv_w, kv_w, kv_w, kv_w, 3 * n_heads, sgu_w, sgu_w,
             moba_w, moba_w, moba_w, d_model, d_model, d_model)
    offs = np.concatenate([[0], np.cumsum(sizes)])

    def seg(a, b):
        return w_in[:, offs[a]:offs[b]].astype(BF16)

    def tile_gain(gain, reps):
        return jnp.tile(gain, reps).reshape(1, reps * HEAD_DIM)

    h = _rmsnorm(x, p["norm_mix"])
    rope_extras = (cos, sin)
    rope_specs = (_spec_rope, _spec_rope)

    q_c, q_r = _matmul(
        h, seg(0, 1), functools.partial(_ep_q, scale=scale),
        [jax.ShapeDtypeStruct((s_len, nsa_w), BF16)] * 2, [_spec_tile, _spec_tile],
        extras=(tile_gain(p["nsa_q_norm"], n_heads),) + rope_extras,
        extra_specs=(_spec_col,) + rope_specs, ts=HEAVY_EPILOGUE_TS, name="proj_nsa_q")
    (kcvc,) = _matmul(h, seg(1, 3), _ep_cast, [jax.ShapeDtypeStruct((s_len, 2 * kv_w), F32)],
                      [_spec_tile], name="proj_nsa_cmp_kv")
    (kskw,) = _matmul(
        h, jnp.concatenate([seg(3, 4), seg(5, 6)], axis=1), functools.partial(_ep_krot, block_mean=False),
        [jax.ShapeDtypeStruct((s_len, 2 * kv_w), BF16)], [_spec_tile],
        extras=(jnp.concatenate([tile_gain(p["nsa_ks_norm"], n_groups),
                                 tile_gain(p["nsa_kw_norm"], n_groups)], axis=1),) + rope_extras,
        extra_specs=(_spec_col,) + rope_specs, ts=HEAVY_EPILOGUE_TS, name="proj_nsa_k")
    (vsvw,) = _matmul(h, jnp.concatenate([seg(4, 5), seg(6, 7)], axis=1), _ep_cast,
                      [jax.ShapeDtypeStruct((s_len, 2 * kv_w), BF16)], [_spec_tile], name="proj_nsa_v")
    n_gate = 3 * n_heads
    w_gate = jnp.pad(seg(7, 8), ((0, 0), (0, LANES - n_gate)))
    b_gate = jnp.pad(p["nsa_gate_b"], (0, LANES - n_gate)).reshape(1, LANES)
    (gates,) = _matmul(h, w_gate, _ep_sigmoid_bias, [jax.ShapeDtypeStruct((s_len, LANES), F32)],
                       [_spec_tile], extras=(b_gate,), extra_specs=(_spec_col,), name="proj_nsa_gates")

    n_chunk = s_len // CMP_STRIDE
    chunks = kcvc.reshape(n_chunk, CMP_STRIDE, 2, n_groups, HEAD_DIM).transpose(2, 3, 0, 1, 4)
    chunks = chunks.reshape(2, n_groups, n_chunk, CMP_STRIDE * HEAD_DIM)
    kc = _compress(chunks[0], p["phi_pe_k"], p["phi_w1_k"], p["phi_w2_k"], p["nsa_kc_norm"], True)
    vc = _compress(chunks[1], p["phi_pe_v"], p["phi_w1_v"], p["phi_w2_v"], p["nsa_kc_norm"], False)

    n_slc = s_len // SLC_LEN
    n_slc_pad = -(-n_slc // LANES) * LANES
    o_cmp, sel_bias_t = _nsa_cmp(q_c, jnp.swapaxes(kc, 1, 2), vc, _overlap_t(n_chunk, n_slc_pad), n_slc)
    tk = min(512, s_len)
    o_slc = _flash_t(q_r.T, sel_bias_t, kskw, _block_onehot(s_len, SLC_LEN),
                     _tiles_t(vsvw[:, :kv_w], n_groups, tk), rep=NSA_REP, tq=min(512, s_len),
                     keys_per_var=LANES * SLC_LEN, out_dtype=F32, name="nsa_selected", n_split=8)
    kw_t = _tiles_t(kskw[:, kv_w:], n_groups, min(256, s_len))
    group_gates = gates[:, :n_gate].reshape(s_len, 3, n_groups, NSA_REP).transpose(2, 0, 1, 3)
    group_gates = jnp.pad(group_gates.reshape(n_groups, s_len, 3 * NSA_REP),
                          ((0, 0), (0, 0), (0, LANES - 3 * NSA_REP)))
    o_a = _window_combine(q_r, kw_t, vsvw, n_groups, o_cmp, o_slc, group_gates)

    (z_uv,) = _matmul(h, seg(8, 10), _ep_cast, [jax.ShapeDtypeStruct((s_len, 2 * sgu_w), F32)],
                      [_spec_tile], name="proj_sgu")
    o_b = _sgu(z_uv, p["sgu_norm"], p["sgu_w"], p["sgu_b"])

    (mq,) = _matmul(
        h, seg(10, 11), functools.partial(_ep_qrot, scale=scale),
        [jax.ShapeDtypeStruct((s_len, moba_w), BF16)], [_spec_tile],
        extras=(tile_gain(p["moba_q_norm"], moba_heads),) + rope_extras,
        extra_specs=(_spec_col,) + rope_specs, ts=HEAVY_EPILOGUE_TS, name="proj_moba_q")
    tm_k = min(1024, s_len)
    mk, mk_mean = _matmul(
        h, seg(11, 12), functools.partial(_ep_krot, block_mean=True),
        [jax.ShapeDtypeStruct((s_len, moba_w), BF16),
         jax.ShapeDtypeStruct((s_len // tm_k, tm_k // MOBA_BLOCK, moba_w), F32)],
        [_spec_tile, _spec_blockmean],
        extras=(tile_gain(p["moba_k_norm"], moba_heads),) + rope_extras,
        extra_specs=(_spec_col,) + rope_specs, tm=tm_k, ts=HEAVY_EPILOGUE_TS, name="proj_moba_k")
    (mv,) = _matmul(h, seg(12, 13), _ep_cast, [jax.ShapeDtypeStruct((s_len, moba_w), BF16)],
                    [_spec_tile], name="proj_moba_v")
    n_blk = s_len // MOBA_BLOCK
    assert n_blk <= LANES, n_blk
    k_mean = mk_mean.reshape(n_blk, moba_heads, HEAD_DIM).transpose(1, 0, 2)
    k_mean = jnp.pad(k_mean, ((0, 0), (0, LANES - n_blk), (0, 0))).astype(BF16)
    o_c = _flash_t(mq.T, _moba_select(mq, k_mean, n_blk), mk, _block_onehot(s_len, MOBA_BLOCK),
                   _tiles_t(mv, moba_heads, min(1024, s_len)), rep=1, tq=min(1024, s_len),
                   keys_per_var=LANES * MOBA_BLOCK, out_dtype=BF16, name="moba", n_split=4)

    (gm,) = _matmul(h, seg(13, 16), _ep_sigmoid, [jax.ShapeDtypeStruct((s_len, 3 * d_model), F32)],
                    [_spec_tile], name="proj_merge_gates")
    y = _merge(o_a, o_b, o_c, big["proj_a"], big["proj_b"], big["proj_c"], layer, gm)
    (x,) = _matmul(y, big["w_out"], _ep_residual, [jax.ShapeDtypeStruct((s_len, d_model), F32)],
                   [_spec_tile], extras=(x,), extra_specs=(_spec_tile,), b_layer=layer, name="out_proj")

    h2 = _rmsnorm(x, p["norm_mlp"])
    (hid,) = _matmul(h2, big["mlp_w1"], _ep_relu2,
                     [jax.ShapeDtypeStruct((s_len, big["mlp_w1"].shape[2]), BF16)], [_spec_tile],
                     b_layer=layer, name="mlp_up")
    (x,) = _matmul(hid, big["mlp_w2"], _ep_residual, [jax.ShapeDtypeStruct((s_len, d_model), F32)],
                   [_spec_tile], extras=(x,), extra_specs=(_spec_tile,), b_layer=layer, name="mlp_down")
    return x


_LAYER_PARAMS = ("norm_mix", "norm_mlp", "w_in", "nsa_gate_b", "nsa_q_norm", "nsa_kc_norm", "nsa_ks_norm",
                 "nsa_kw_norm", "phi_pe_k", "phi_w1_k", "phi_w2_k", "phi_pe_v", "phi_w1_v", "phi_w2_v",
                 "sgu_norm", "sgu_w", "sgu_b", "moba_q_norm", "moba_k_norm", "proj_a", "proj_b", "proj_c",
                 "w_out", "mlp_w1", "mlp_w2")
_BIG_PARAMS = ("proj_a", "proj_b", "proj_c", "w_out", "mlp_w1", "mlp_w2")


def kernel(x, positions, norm_mix, norm_mlp, w_in, nsa_gate_b, nsa_q_norm, nsa_kc_norm, nsa_ks_norm, nsa_kw_norm, phi_pe_k, phi_w1_k, phi_w2_k, phi_pe_v, phi_w1_v, phi_w2_v, sgu_norm, sgu_w, sgu_b, moba_q_norm, moba_k_norm, proj_a, proj_b, proj_c, w_out, mlp_w1, mlp_w2):
    stacked = dict(zip(_LAYER_PARAMS, (norm_mix, norm_mlp, w_in, nsa_gate_b, nsa_q_norm, nsa_kc_norm,
                                       nsa_ks_norm, nsa_kw_norm, phi_pe_k, phi_w1_k, phi_w2_k, phi_pe_v,
                                       phi_w1_v, phi_w2_v, sgu_norm, sgu_w, sgu_b, moba_q_norm, moba_k_norm,
                                       proj_a, proj_b, proj_c, w_out, mlp_w1, mlp_w2)))
    depth = w_in.shape[0]
    big = {k: stacked.pop(k).astype(BF16) for k in _BIG_PARAMS}
    inv = ROPE_THETA ** (-jnp.arange(0, HEAD_DIM, 2, dtype=F32) / HEAD_DIM)
    outs = []
    for b in range(x.shape[0]):
        ang = positions[b].astype(F32)[:, None] * inv
        cos = jnp.concatenate([jnp.cos(ang), jnp.cos(ang)], axis=-1)
        sin = jnp.concatenate([-jnp.sin(ang), jnp.sin(ang)], axis=-1)
        xb = x[b]
        for l in range(depth):
            xb = _layer(xb, cos, sin, {k: v[l] for k, v in stacked.items()}, big, l)
        outs.append(xb)
    return jnp.stack(outs)
```

```python
import functools
import math

import jax
import jax.numpy as jnp
import numpy as np
from jax import lax
from jax.experimental import pallas as pl
from jax.experimental.pallas import tpu as pltpu

F32 = jnp.float32
BF16 = jnp.bfloat16

HEAD_DIM = 128
LANES = 128
ROPE_THETA = 10000.0
NORM_EPS = 1e-6
NEG = -1e30
BELOW_NEG = -3e38

NSA_REP = 4
CMP_LEN = 32
CMP_STRIDE = 16
SLC_LEN = 64
SLC_TOPK = 16
WINDOW = 512
SGU_GROUPS = 8
SGU_CHUNK = 128
MOBA_BLOCK = 256
MOBA_TOPK = 3

LIGHT_EPILOGUE_TS = 512
HEAVY_EPILOGUE_TS = 1024

MIB = 1024 * 1024
VMEM_LIMIT = 52 * MIB


def _cparams(sem, vmem=VMEM_LIMIT):
    return pltpu.CompilerParams(dimension_semantics=sem, vmem_limit_bytes=vmem)


def _tile(n, pref):
    if n <= pref:
        return n
    t = (pref // LANES) * LANES
    while t >= LANES:
        if n % t == 0:
            return t
        t -= LANES
    raise ValueError(f"no 128-multiple tile divides {n}")


def _gelu(x):
    c = math.sqrt(2.0 / math.pi)
    return 0.5 * x * (1.0 + jnp.tanh(c * (x + 0.044715 * (x * x * x))))


def _sigmoid(x):
    return 1.0 / (1.0 + jnp.exp(-x))


def _head_norm(x, gain):
    return x * lax.rsqrt(jnp.mean(x * x, axis=-1, keepdims=True) + NORM_EPS) * gain


def _rope(x, cos, sin_signed):
    return x * cos + pltpu.roll(x, HEAD_DIM // 2, 1) * sin_signed


def _rmsnorm_body(x_ref, g_ref, o_ref):
    x = x_ref[...]
    y = x * lax.rsqrt(jnp.mean(x * x, axis=-1, keepdims=True) + NORM_EPS)
    o_ref[...] = (y * g_ref[...]).astype(o_ref.dtype)


def _rmsnorm(x, gain, tm=256):
    m, d = x.shape
    tm = min(tm, m)
    return pl.pallas_call(
        _rmsnorm_body,
        grid=(m // tm,),
        in_specs=[pl.BlockSpec((tm, d), lambda i: (i, 0)),
                  pl.BlockSpec((1, d), lambda i: (0, 0))],
        out_specs=pl.BlockSpec((tm, d), lambda i: (i, 0)),
        out_shape=jax.ShapeDtypeStruct((m, d), BF16),
        compiler_params=_cparams(("parallel",)),
        name="rmsnorm",
    )(x, gain.reshape(1, d))


def _mm_body(*refs, n_extra, n_out, nk, tn, ts, epilogue):
    a_ref, b_ref = refs[0], refs[1]
    extra = refs[2:2 + n_extra]
    outs = refs[2 + n_extra:2 + n_extra + n_out]
    acc_ref = refs[-1] if nk > 1 else None

    def finish():
        for c0 in range(0, tn, ts):
            part = jnp.dot(a_ref[...], b_ref[:, c0:c0 + ts], preferred_element_type=F32)
            if nk > 1:
                part = part + acc_ref[:, c0:c0 + ts]
            epilogue(part, extra, outs, slice(c0, c0 + ts))

    if nk == 1:
        finish()
        return
    k = pl.program_id(2)
    split_finish = ts < tn

    @pl.when(k == 0)
    def _():
        acc_ref[...] = jnp.dot(a_ref[...], b_ref[...], preferred_element_type=F32)

    @pl.when((k > 0) & (k < nk - 1) if split_finish else k > 0)
    def _():
        acc_ref[...] += jnp.dot(a_ref[...], b_ref[...], preferred_element_type=F32)

    @pl.when(k == nk - 1)
    def _():
        if split_finish:
            finish()
        else:
            epilogue(acc_ref[...], extra, outs, slice(0, tn))


def _matmul(a, b, epilogue, out_shapes, out_specs, extras=(), extra_specs=(),
            tm=1024, tn=1024, tk=2048, ts=LIGHT_EPILOGUE_TS, b_layer=0, name="matmul"):
    m, kdim = a.shape
    n = b.shape[-1]
    tm, tn, tk = min(tm, m), _tile(n, tn), _tile(kdim, tk)
    ts = min(ts, tn)
    if b.ndim == 3:
        b_spec = pl.BlockSpec((None, tk, tn), lambda i, j, k: (b_layer, k, j))
    else:
        b_spec = pl.BlockSpec((tk, tn), lambda i, j, k: (k, j))
    nk = kdim // tk
    body = functools.partial(_mm_body, n_extra=len(extras), n_out=len(out_shapes), nk=nk, tn=tn, ts=ts,
                             epilogue=epilogue)
    scratch = [] if nk == 1 else [pltpu.VMEM((tm, tn), F32)]
    return pl.pallas_call(
        body,
        grid=(m // tm, n // tn, nk),
        in_specs=[pl.BlockSpec((tm, tk), lambda i, j, k: (i, k)),
                  b_spec]
                 + [s(tm, tn) for s in extra_specs],
        out_specs=[s(tm, tn) for s in out_specs],
        out_shape=out_shapes,
        scratch_shapes=scratch,
        compiler_params=_cparams(("parallel", "parallel", "arbitrary")),
        name=name,
    )(a, b, *extras)


def _spec_tile(tm, tn):
    return pl.BlockSpec((tm, tn), lambda i, j, k: (i, j))


def _spec_col(tm, tn):
    return pl.BlockSpec((1, tn), lambda i, j, k: (0, j))


def _spec_rope(tm, tn):
    return pl.BlockSpec((tm, HEAD_DIM), lambda i, j, k: (i, 0))


def _spec_blockmean(tm, tn):
    return pl.BlockSpec((1, tm // MOBA_BLOCK, tn), lambda i, j, k: (i, 0, j))


def _ep_cast(acc, extra, outs, cols):
    outs[0][:, cols] = acc.astype(outs[0].dtype)


def _ep_sigmoid_bias(acc, extra, outs, cols):
    outs[0][:, cols] = _sigmoid(acc + extra[0][:, cols])


def _ep_sigmoid(acc, extra, outs, cols):
    outs[0][:, cols] = _sigmoid(acc)


def _ep_relu2(acc, extra, outs, cols):
    r = jnp.maximum(acc, 0.0)
    outs[0][:, cols] = (r * r).astype(outs[0].dtype)


def _ep_residual(acc, extra, outs, cols):
    outs[0][:, cols] = extra[0][:, cols] + acc


def _heads(cols):
    return [(slice(c - cols.start, c - cols.start + HEAD_DIM), slice(c, c + HEAD_DIM))
            for c in range(cols.start, cols.stop, HEAD_DIM)]


def _ep_q(acc, extra, outs, cols, *, scale):
    cos, sin = extra[1][...], extra[2][...]
    for a_sl, t_sl in _heads(cols):
        y = _head_norm(acc[:, a_sl], extra[0][:, t_sl])
        outs[0][:, t_sl] = (y * scale).astype(BF16)
        outs[1][:, t_sl] = (_rope(y, cos, sin) * scale).astype(BF16)


def _ep_qrot(acc, extra, outs, cols, *, scale):
    cos, sin = extra[1][...], extra[2][...]
    for a_sl, t_sl in _heads(cols):
        y = _head_norm(acc[:, a_sl], extra[0][:, t_sl])
        outs[0][:, t_sl] = (_rope(y, cos, sin) * scale).astype(BF16)


def _ep_krot(acc, extra, outs, cols, *, block_mean):
    cos, sin = extra[1][...], extra[2][...]
    for a_sl, t_sl in _heads(cols):
        y = _rope(_head_norm(acc[:, a_sl], extra[0][:, t_sl]), cos, sin)
        outs[0][:, t_sl] = y.astype(BF16)
        if block_mean:
            for blk in range(acc.shape[0] // MOBA_BLOCK):
                rows = y[blk * MOBA_BLOCK:(blk + 1) * MOBA_BLOCK]
                outs[1][0, blk:blk + 1, t_sl] = jnp.mean(rows, axis=0, keepdims=True)


def _compress_body(a_ref, pe_ref, w1_ref, w2_ref, g_ref, o_ref, *, norm):
    half = CMP_STRIDE * HEAD_DIM
    a = a_ref[0]
    n_chunk = a.shape[0]
    x1 = (a + pe_ref[:, :half]).astype(BF16)
    x2 = (a + pe_ref[:, half:]).astype(BF16)
    p1 = jnp.dot(x1, w1_ref[:half, :], preferred_element_type=F32)
    p2 = jnp.dot(x2, w1_ref[half:, :], preferred_element_type=F32)
    h = _gelu(p1 + pltpu.roll(p2, n_chunk - 1, 0))
    o = jnp.dot(h.astype(BF16), w2_ref[...], preferred_element_type=F32)
    if norm:
        o = _head_norm(o, g_ref[...])
    o_ref[0] = o.astype(o_ref.dtype)


def _compress(a, pe, w1, w2, gain, norm):
    g, n_chunk, width = a.shape
    hidden = w1.shape[1]
    return pl.pallas_call(
        functools.partial(_compress_body, norm=norm),
        grid=(g,),
        in_specs=[pl.BlockSpec((1, n_chunk, width), lambda i: (i, 0, 0)),
                  pl.BlockSpec((1, 2 * width), lambda i: (0, 0)),
                  pl.BlockSpec((2 * width, hidden), lambda i: (0, 0)),
                  pl.BlockSpec((hidden, HEAD_DIM), lambda i: (0, 0)),
                  pl.BlockSpec((1, HEAD_DIM), lambda i: (0, 0))],
        out_specs=pl.BlockSpec((1, n_chunk, HEAD_DIM), lambda i: (i, 0, 0)),
        out_shape=jax.ShapeDtypeStruct((g, n_chunk, HEAD_DIM), BF16),
        compiler_params=_cparams(("parallel",)),
        name="nsa_compress",
    )(a, pe.reshape(1, 2 * width), w1.astype(BF16), w2.astype(BF16), gain.reshape(1, HEAD_DIM))


def _topk_mask(score, index_f, k, axis):
    n = score.shape[axis]
    sel = jnp.zeros(score.shape, dtype=jnp.bool_)
    for _ in range(k):
        m = jnp.max(score, axis=axis, keepdims=True)
        first = jnp.min(jnp.where(score == m, index_f, float(n)), axis=axis, keepdims=True)
        hit = index_f == first
        sel = jnp.logical_or(sel, hit)
        score = jnp.where(hit, BELOW_NEG, score)
    return sel


def _nsa_cmp_body(q_ref, kt_ref, v_ref, ovt_ref, o_ref, sbt_ref, *, tq, n_slc, cw):
    qi = pl.program_id(1)
    q0 = qi * tq
    n_pad = kt_ref.shape[-1]
    q = jnp.concatenate([q_ref[:, r * HEAD_DIM:(r + 1) * HEAD_DIM] for r in range(NSA_REP)], axis=0)
    t1 = q0 + lax.broadcasted_iota(jnp.int32, (tq, 1), 0)
    t = jnp.concatenate([t1] * NSA_REP, axis=0)

    def attend(width):
        s = jnp.dot(q, kt_ref[0, :, :width], preferred_element_type=F32)
        cmp_end = lax.broadcasted_iota(jnp.int32, (1, width), 1) * CMP_STRIDE + (CMP_LEN - 1)
        s = jnp.where(cmp_end <= t, s, NEG)
        m = jnp.max(s, axis=-1, keepdims=True)
        e = jnp.exp2(s - m)
        inv = jnp.where(m > 0.5 * NEG, 1.0 / jnp.sum(e, axis=-1, keepdims=True), 0.0)
        p = e * inv
        o = jnp.dot(p.astype(BF16), v_ref[0, :width, :], preferred_element_type=F32)
        for r in range(NSA_REP):
            o_ref[:, r * HEAD_DIM:(r + 1) * HEAD_DIM] = o[r * tq:(r + 1) * tq]
        ps = p[0:tq]
        for r in range(1, NSA_REP):
            ps = ps + p[r * tq:(r + 1) * tq]
        ps_hi = ps.astype(BF16)
        ps_lo = (ps - ps_hi.astype(F32)).astype(BF16)
        nt = (((1,), (1,)), ((), ()))
        n_rows = width * CMP_STRIDE // SLC_LEN
        ovt = ovt_ref[:n_rows, :width]
        imp = (lax.dot_general(ovt, ps_hi, nt, preferred_element_type=F32)
               + lax.dot_general(ovt, ps_lo, nt, preferred_element_type=F32))
        j = lax.broadcasted_iota(jnp.int32, imp.shape, 0)
        cur = (q0 + lax.broadcasted_iota(jnp.int32, (1, tq), 1)) // SLC_LEN
        forced = (j == 0) | (j == cur) | (j == cur - 1)
        allowed = j <= cur
        score = jnp.where(allowed & jnp.logical_not(forced), imp, NEG)
        score = jnp.where(j < n_slc, score, BELOW_NEG)
        sel = _topk_mask(score, j.astype(F32), min(SLC_TOPK, n_slc) - 3, 0)
        sbt_ref[0, :n_rows, :] = jnp.where((sel | forced) & allowed, 0.0, NEG).astype(BF16)
        if n_rows < sbt_ref.shape[1]:
            sbt_ref[0, n_rows:, :] = jnp.full((sbt_ref.shape[1] - n_rows, tq), NEG, BF16)

    n_chunks = ((q0 + tq) // CMP_STRIDE - 1 + cw - 1) // cw
    for v in range(1, n_pad // cw + 1):
        pl.when(n_chunks == v)(functools.partial(attend, v * cw))


def _nsa_cmp(q_c, kc_t, vc, overlap_t, n_slc, tq=256):
    s_len = q_c.shape[0]
    tq = min(tq, s_len)
    g, _, n_pad = kc_t.shape
    n_slc_pad = overlap_t.shape[0]
    gw = NSA_REP * HEAD_DIM
    cw = min(256, n_pad)
    assert n_pad % cw == 0, (n_pad, cw)
    return pl.pallas_call(
        functools.partial(_nsa_cmp_body, tq=tq, n_slc=n_slc, cw=cw),
        grid=(g, s_len // tq),
        in_specs=[pl.BlockSpec((tq, gw), lambda gi, qi: (qi, gi)),
                  pl.BlockSpec((1, HEAD_DIM, n_pad), lambda gi, qi: (gi, 0, 0)),
                  pl.BlockSpec((1, n_pad, HEAD_DIM), lambda gi, qi: (gi, 0, 0)),
                  pl.BlockSpec((n_slc_pad, n_pad), lambda gi, qi: (0, 0))],
        out_specs=[pl.BlockSpec((tq, gw), lambda gi, qi: (qi, gi)),
                   pl.BlockSpec((1, n_slc_pad, tq), lambda gi, qi: (gi, 0, qi))],
        out_shape=[jax.ShapeDtypeStruct((s_len, g * gw), F32),
                   jax.ShapeDtypeStruct((g, n_slc_pad, s_len), BF16)],
        compiler_params=_cparams(("parallel", "parallel")),
        name="nsa_cmp_select",
    )(q_c, kc_t, vc, overlap_t)


def _flash_t_body(qt_ref, bt_ref, k_ref, oh_ref, vt_ref, o_ref, qa_s, m_s, l_s, acc_s, s_s, *,
                  rep, tq, tk, keys_per_var, n_var, n_split):
    width = rep * tq
    qi = pl.program_id(1)
    q0 = qi * tq
    for var in range(n_var):
        bt = bt_ref[0, var * LANES:(var + 1) * LANES, :]
        qa_s[var, :HEAD_DIM, :] = jnp.concatenate(
            [qt_ref[r * HEAD_DIM:(r + 1) * HEAD_DIM, :] for r in range(rep)], axis=1)
        qa_s[var, HEAD_DIM:, :] = jnp.concatenate([bt] * rep, axis=1)
    m_s[...] = jnp.full(m_s.shape, NEG, F32)
    l_s[...] = jnp.zeros(l_s.shape, F32)
    acc_s[...] = jnp.zeros(acc_s.shape, F32)
    t1 = q0 + lax.broadcasted_iota(jnp.int32, (1, tq), 1)
    t = jnp.concatenate([t1] * rep, axis=1)
    cw = width // n_split

    def key_tile(kt):
        start = pl.multiple_of(kt * tk, tk)
        return jnp.concatenate([k_ref[pl.ds(start, tk), :], oh_ref[pl.ds(start, tk), :]], axis=1)

    def scores(k_tile, kt, c):
        var = (kt * tk) // keys_per_var if n_var > 1 else 0
        return jnp.dot(k_tile, qa_s[var, :, c * cw:(c + 1) * cw], preferred_element_type=F32)

    def consume(kt, c, s, causal):
        cs = slice(c * cw, (c + 1) * cw)
        if causal:
            pos = kt * tk + lax.broadcasted_iota(jnp.int32, (tk, 1), 0)
            s = jnp.where(pos <= t[:, cs], s, NEG)
        m_old = m_s[:, cs]
        m_new = jnp.maximum(m_old, jnp.max(s, axis=0, keepdims=True))
        alpha = jnp.exp2(m_old - m_new)
        p = jnp.exp2(s - m_new)
        l_s[:, cs] = alpha * l_s[:, cs] + jnp.sum(p, axis=0, keepdims=True)
        m_s[:, cs] = m_new
        return alpha, p.astype(BF16)

    def step(kt, causal, prefetch, rd):
        v_t = vt_ref[0, kt]
        if prefetch:
            k_next = key_tile(kt + 1)
            for c in range(n_split):
                s_s[1 - rd, :, c * cw:(c + 1) * cw] = scores(k_next, kt + 1, c)
        for c in range(n_split):
            cs = slice(c * cw, (c + 1) * cw)
            alpha, p = consume(kt, c, s_s[rd, :, cs], causal)
            acc_s[:, cs] = alpha * acc_s[:, cs] + jnp.dot(v_t, p, preferred_element_type=F32)

    n_full = q0 // tk
    n_diag = max(1, tq // tk)
    k_first = key_tile(0)
    for c in range(n_split):
        s_s[0, :, c * cw:(c + 1) * cw] = scores(k_first, 0, c)

    def two_steps(j, carry):
        step(2 * j, False, True, 0)
        step(2 * j + 1, False, True, 1)
        return carry

    def diagonal(rd):
        for d in range(n_diag):
            step(n_full + d, True, d < n_diag - 1, (rd + d) % 2)

    lax.fori_loop(0, n_full // 2, two_steps, 0)

    @pl.when(n_full % 2 == 1)
    def _():
        step(n_full - 1, False, True, 0)
        diagonal(1)

    @pl.when(n_full % 2 == 0)
    def _():
        diagonal(0)

    o_t = acc_s[...] / l_s[...]
    for r in range(rep):
        o_ref[:, r * HEAD_DIM:(r + 1) * HEAD_DIM] = o_t[:, r * tq:(r + 1) * tq].T.astype(o_ref.dtype)


def _flash_t(q_t, bias_t, k, block_onehot, v_t, *, rep, tq, keys_per_var, out_dtype, name, n_split):
    s_len = q_t.shape[1]
    g, n_kt, _, tk = v_t.shape
    gw = rep * HEAD_DIM
    width = rep * tq
    n_var = bias_t.shape[1] // LANES
    assert (tk % tq == 0 or tq % tk == 0) and n_split % 2 == 0 and width % (n_split * LANES) == 0, \
        (tq, tk, width, n_split)
    return pl.pallas_call(
        functools.partial(_flash_t_body, rep=rep, tq=tq, tk=tk, keys_per_var=keys_per_var,
                          n_var=n_var, n_split=n_split),
        grid=(g, s_len // tq),
        in_specs=[pl.BlockSpec((gw, tq), lambda gi, qi: (gi, qi)),
                  pl.BlockSpec((1, n_var * LANES, tq), lambda gi, qi: (gi, 0, qi)),
                  pl.BlockSpec((s_len, HEAD_DIM), lambda gi, qi: (0, gi)),
                  pl.BlockSpec((s_len, LANES), lambda gi, qi: (0, 0)),
                  pl.BlockSpec((1, n_kt, HEAD_DIM, tk), lambda gi, qi: (gi, 0, 0, 0))],
        out_specs=pl.BlockSpec((tq, gw), lambda gi, qi: (qi, gi)),
        out_shape=jax.ShapeDtypeStruct((s_len, g * gw), out_dtype),
        scratch_shapes=[pltpu.VMEM((n_var, 2 * HEAD_DIM, width), BF16),
                        pltpu.VMEM((1, width), F32),
                        pltpu.VMEM((1, width), F32),
                        pltpu.VMEM((HEAD_DIM, width), F32),
                        pltpu.VMEM((2, tk, width), F32)],
        compiler_params=_cparams(("parallel", "arbitrary")),
        name=name,
    )(q_t, bias_t, k, block_onehot, v_t)


def _moba_select_body(q_ref, km_ref, bt_ref, *, tq, n_blk):
    q0 = pl.program_id(1) * tq
    nt = (((1,), (1,)), ((), ()))
    sg = lax.dot_general(km_ref[0], q_ref[...], nt, preferred_element_type=F32)
    j = lax.broadcasted_iota(jnp.int32, sg.shape, 0)
    cur = (q0 + lax.broadcasted_iota(jnp.int32, (1, tq), 1)) // MOBA_BLOCK
    past = j < cur
    score = jnp.where(j < n_blk, jnp.where(past, sg, NEG), BELOW_NEG)
    sel = _topk_mask(score, j.astype(F32), min(MOBA_TOPK, n_blk), 0)
    bt_ref[0] = jnp.where((sel & past) | (j == cur), 0.0, NEG).astype(BF16)


def _moba_select(q, k_mean, n_blk, tq=512):
    s_len = q.shape[0]
    tq = min(tq, s_len)
    n_heads = k_mean.shape[0]
    return pl.pallas_call(
        functools.partial(_moba_select_body, tq=tq, n_blk=n_blk),
        grid=(n_heads, s_len // tq),
        in_specs=[pl.BlockSpec((tq, HEAD_DIM), lambda hi, qi: (qi, hi)),
                  pl.BlockSpec((1, LANES, HEAD_DIM), lambda hi, qi: (hi, 0, 0))],
        out_specs=pl.BlockSpec((1, LANES, tq), lambda hi, qi: (hi, 0, qi)),
        out_shape=jax.ShapeDtypeStruct((n_heads, LANES, s_len), BF16),
        compiler_params=_cparams(("parallel", "parallel")),
        name="moba_select",
    )(q, k_mean)


def _window_body(q_ref, kt_ref, v_ref, oc_ref, os_ref, g_ref, o_ref, *, tq):
    qi = pl.program_id(1)
    q0 = qi * tq
    n_past = WINDOW // tq
    q = jnp.concatenate([q_ref[:, r * HEAD_DIM:(r + 1) * HEAD_DIM] for r in range(NSA_REP)], axis=0)
    t1 = q0 + lax.broadcasted_iota(jnp.int32, (tq, 1), 0)
    t = jnp.concatenate([t1] * NSA_REP, axis=0)
    lane = lax.broadcasted_iota(jnp.int32, (1, tq), 1)
    ones = jnp.ones((tq, HEAD_DIM), BF16)
    scores, tiles = [], []
    for i in range(n_past + 1):
        raw = qi - n_past + i
        idx = jnp.maximum(raw, 0)
        tiles.append(idx)
        s = jnp.dot(q, kt_ref[0, idx], preferred_element_type=F32)
        pos = raw * tq + lane
        if i == 0:
            s = jnp.where(pos > t - WINDOW, s, NEG)
        if i == n_past:
            s = jnp.where(pos <= t, s, NEG)
        else:
            s = s + jnp.where(raw >= 0, 0.0, NEG)
        scores.append(s)
    m = jnp.max(scores[0], axis=-1, keepdims=True)
    for s in scores[1:]:
        m = jnp.maximum(m, jnp.max(s, axis=-1, keepdims=True))
    acc = jnp.zeros((NSA_REP * tq, 2 * HEAD_DIM), F32)
    for i, s in enumerate(scores):
        start = pl.multiple_of(tiles[i] * tq, tq)
        v_aug = jnp.concatenate([v_ref[pl.ds(start, tq), :], ones], axis=1)
        acc = acc + jnp.dot(jnp.exp2(s - m).astype(BF16), v_aug, preferred_element_type=F32)
    o_win = acc[:, :HEAD_DIM] / acc[:, HEAD_DIM:]
    gate = g_ref[0]
    for r in range(NSA_REP):
        sl = slice(r * HEAD_DIM, (r + 1) * HEAD_DIM)
        o = (gate[:, r:r + 1] * oc_ref[:, sl]
             + gate[:, NSA_REP + r:NSA_REP + r + 1] * os_ref[:, sl]
             + gate[:, 2 * NSA_REP + r:2 * NSA_REP + r + 1] * o_win[r * tq:(r + 1) * tq])
        o_ref[:, sl] = o.astype(o_ref.dtype)


def _window_combine(q_r, kw_t, v, v_block0, o_cmp, o_slc, gates):
    s_len = q_r.shape[0]
    g, n_kt, _, tq = kw_t.shape
    assert WINDOW % tq == 0, tq
    gw = NSA_REP * HEAD_DIM
    tile = pl.BlockSpec((tq, gw), lambda gi, qi: (qi, gi))
    return pl.pallas_call(
        functools.partial(_window_body, tq=tq),
        grid=(g, s_len // tq),
        in_specs=[tile,
                  pl.BlockSpec((1, n_kt, HEAD_DIM, tq), lambda gi, qi: (gi, 0, 0, 0)),
                  pl.BlockSpec((s_len, HEAD_DIM), lambda gi, qi: (0, v_block0 + gi)),
                  tile, tile,
                  pl.BlockSpec((1, tq, LANES), lambda gi, qi: (gi, qi, 0))],
        out_specs=tile,
        out_shape=jax.ShapeDtypeStruct((s_len, g * gw), BF16),
        compiler_params=_cparams(("parallel", "parallel")),
        name="nsa_window_combine",
    )(q_r, kw_t, v, o_cmp, o_slc, gates)


def _sgu_body(zu_ref, zv_ref, gain_ref, w_ref, bt_ref, o_ref, *, tm):
    gd = zu_ref.shape[1] // SGU_GROUPS
    row = lax.broadcasted_iota(jnp.int32, (SGU_CHUNK, SGU_CHUNK), 0)
    col = lax.broadcasted_iota(jnp.int32, (SGU_CHUNK, SGU_CHUNK), 1)
    bt = bt_ref[...]
    for g in range(SGU_GROUPS):
        sl = slice(g * gd, (g + 1) * gd)
        v = _gelu(zv_ref[:, sl])
        v = (v * lax.rsqrt(jnp.mean(v * v, axis=-1, keepdims=True) + NORM_EPS) * gain_ref[:, sl]).astype(BF16)
        w = jnp.where(col <= row, w_ref[g], 0.0).astype(BF16)
        for c in range(tm // SGU_CHUNK):
            rs = slice(c * SGU_CHUNK, (c + 1) * SGU_CHUNK)
            mixed = jnp.dot(w, v[rs], preferred_element_type=F32) + bt[:, g:g + 1]
            o_ref[rs, sl] = (_gelu(zu_ref[rs, sl]) * mixed).astype(o_ref.dtype)


def _sgu(z_uv, gain, w_s, b_s, tm=512):
    s_len = z_uv.shape[0]
    width = z_uv.shape[1] // 2
    tm = min(tm, s_len)
    return pl.pallas_call(
        functools.partial(_sgu_body, tm=tm),
        grid=(s_len // tm,),
        in_specs=[pl.BlockSpec((tm, width), lambda i: (i, 0)),
                  pl.BlockSpec((tm, width), lambda i: (i, 1)),
                  pl.BlockSpec((1, width), lambda i: (0, 0)),
                  pl.BlockSpec((SGU_GROUPS, SGU_CHUNK, SGU_CHUNK), lambda i: (0, 0, 0)),
                  pl.BlockSpec((SGU_CHUNK, SGU_GROUPS), lambda i: (0, 0))],
        out_specs=pl.BlockSpec((tm, width), lambda i: (i, 0)),
        out_shape=jax.ShapeDtypeStruct((s_len, width), BF16),
        compiler_params=_cparams(("parallel",)),
        name="sgu",
    )(z_uv, z_uv, gain.reshape(1, width), w_s, b_s.T)


def _merge_body(oa_ref, ob_ref, oc_ref, pa_ref, pb_ref, pc_ref, ga_ref, gb_ref, gc_ref, y_ref):
    y = ga_ref[...] * jnp.dot(oa_ref[...], pa_ref[...], preferred_element_type=F32)
    y = y + gb_ref[...] * jnp.dot(ob_ref[...], pb_ref[...], preferred_element_type=F32)
    y = y + gc_ref[...] * jnp.dot(oc_ref[...], pc_ref[...], preferred_element_type=F32)
    y_ref[...] = y.astype(y_ref.dtype)


def _merge(o_a, o_b, o_c, p_a, p_b, p_c, layer, gm, tm=512, tn=1024):
    s_len = o_a.shape[0]
    d = p_a.shape[2]
    tm, tn = min(tm, s_len), _tile(d, tn)
    nj = d // tn

    def rows(w):
        return pl.BlockSpec((tm, w), lambda i, j: (i, 0))

    def cols(kdim):
        return pl.BlockSpec((None, kdim, tn), lambda i, j: (layer, 0, j))

    def gate(off):
        return pl.BlockSpec((tm, tn), lambda i, j: (i, off * nj + j))

    return pl.pallas_call(
        _merge_body,
        grid=(s_len // tm, nj),
        in_specs=[rows(o_a.shape[1]), rows(o_b.shape[1]), rows(o_c.shape[1]),
                  cols(p_a.shape[1]), cols(p_b.shape[1]), cols(p_c.shape[1]),
                  gate(0), gate(1), gate(2)],
        out_specs=pl.BlockSpec((tm, tn), lambda i, j: (i, j)),
        out_shape=jax.ShapeDtypeStruct((s_len, d), BF16),
        compiler_params=_cparams(("parallel", "parallel")),
        name="gated_merge",
    )(o_a, o_b, o_c, p_a, p_b, p_c, gm, gm, gm)


def _block_onehot(s_len, block):
    key = np.arange(s_len)
    return jnp.asarray(((key // block) % LANES)[:, None] == np.arange(LANES)[None, :], dtype=BF16)


def _tiles_t(v, n_heads, tk):
    s_len = v.shape[0]
    return v.reshape(s_len // tk, tk, n_heads, HEAD_DIM).transpose(2, 0, 3, 1)


def _overlap_t(n_pad, n_slc_pad):
    i = np.arange(n_pad)[None, :]
    j = np.arange(n_slc_pad)[:, None]
    ov = (i * CMP_STRIDE <= j * SLC_LEN + SLC_LEN - 1) & (i * CMP_STRIDE + CMP_LEN - 1 >= j * SLC_LEN)
    return jnp.asarray(ov, dtype=BF16)


def _layer(x, cos, sin, p, big, layer):
    s_len, d_model = x.shape
    scale = HEAD_DIM ** -0.5 * math.log2(math.e)
    w_in = p["w_in"]
    nsa_w = big["proj_a"].shape[1]
    sgu_w = big["proj_b"].shape[1]
    moba_w = big["proj_c"].shape[1]
    n_heads = nsa_w // HEAD_DIM
    n_groups = n_heads // NSA_REP
    kv_w = n_groups * HEAD_DIM
    moba_heads = moba_w // HEAD_DIM
    sizes = (nsa_w, kv_w, kv_w, kv_w, kv_w, kv_w, kv_w, 3 * n_heads, sgu_w, sgu_w,
             moba_w, moba_w, moba_w, d_model, d_model, d_model)
    offs = np.concatenate([[0], np.cumsum(sizes)])

    def seg(a, b):
        return w_in[:, offs[a]:offs[b]].astype(BF16)

    def tile_gain(gain, reps):
        return jnp.tile(gain, reps).reshape(1, reps * HEAD_DIM)

    h = _rmsnorm(x, p["norm_mix"])
    rope_extras = (cos, sin)
    rope_specs = (_spec_rope, _spec_rope)

    q_c, q_r = _matmul(
        h, seg(0, 1), functools.partial(_ep_q, scale=scale),
        [jax.ShapeDtypeStruct((s_len, nsa_w), BF16)] * 2, [_spec_tile, _spec_tile],
        extras=(tile_gain(p["nsa_q_norm"], n_heads),) + rope_extras,
        extra_specs=(_spec_col,) + rope_specs, ts=HEAVY_EPILOGUE_TS, name="proj_nsa_q")
    (kcvc,) = _matmul(h, seg(1, 3), _ep_cast, [jax.ShapeDtypeStruct((s_len, 2 * kv_w), F32)],
                      [_spec_tile], name="proj_nsa_cmp_kv")
    (kskw,) = _matmul(
        h, jnp.concatenate([seg(3, 4), seg(5, 6)], axis=1), functools.partial(_ep_krot, block_mean=False),
        [jax.ShapeDtypeStruct((s_len, 2 * kv_w), BF16)], [_spec_tile],
        extras=(jnp.concatenate([tile_gain(p["nsa_ks_norm"], n_groups),
                                 tile_gain(p["nsa_kw_norm"], n_groups)], axis=1),) + rope_extras,
        extra_specs=(_spec_col,) + rope_specs, ts=HEAVY_EPILOGUE_TS, name="proj_nsa_k")
    (vsvw,) = _matmul(h, jnp.concatenate([seg(4, 5), seg(6, 7)], axis=1), _ep_cast,
                      [jax.ShapeDtypeStruct((s_len, 2 * kv_w), BF16)], [_spec_tile], name="proj_nsa_v")
    n_gate = 3 * n_heads
    w_gate = jnp.pad(seg(7, 8), ((0, 0), (0, LANES - n_gate)))
    b_gate = jnp.pad(p["nsa_gate_b"], (0, LANES - n_gate)).reshape(1, LANES)
    (gates,) = _matmul(h, w_gate, _ep_sigmoid_bias, [jax.ShapeDtypeStruct((s_len, LANES), F32)],
                       [_spec_tile], extras=(b_gate,), extra_specs=(_spec_col,), name="proj_nsa_gates")

    n_chunk = s_len // CMP_STRIDE
    chunks = kcvc.reshape(n_chunk, CMP_STRIDE, 2, n_groups, HEAD_DIM).transpose(2, 3, 0, 1, 4)
    chunks = chunks.reshape(2, n_groups, n_chunk, CMP_STRIDE * HEAD_DIM)
    kc = _compress(chunks[0], p["phi_pe_k"], p["phi_w1_k"], p["phi_w2_k"], p["nsa_kc_norm"], True)
    vc = _compress(chunks[1], p["phi_pe_v"], p["phi_w1_v"], p["phi_w2_v"], p["nsa_kc_norm"], False)

    n_slc = s_len // SLC_LEN
    n_slc_pad = -(-n_slc // LANES) * LANES
    o_cmp, sel_bias_t = _nsa_cmp(q_c, jnp.swapaxes(kc, 1, 2), vc, _overlap_t(n_chunk, n_slc_pad), n_slc)
    tk = min(512, s_len)
    o_slc = _flash_t(q_r.T, sel_bias_t, kskw, _block_onehot(s_len, SLC_LEN),
                     _tiles_t(vsvw[:, :kv_w], n_groups, tk), rep=NSA_REP, tq=min(512, s_len),
                     keys_per_var=LANES * SLC_LEN, out_dtype=F32, name="nsa_selected", n_split=8)
    kw_t = _tiles_t(kskw[:, kv_w:], n_groups, min(256, s_len))
    group_gates = gates[:, :n_gate].reshape(s_len, 3, n_groups, NSA_REP).transpose(2, 0, 1, 3)
    group_gates = jnp.pad(group_gates.reshape(n_groups, s_len, 3 * NSA_REP),
                          ((0, 0), (0, 0), (0, LANES - 3 * NSA_REP)))
    o_a = _window_combine(q_r, kw_t, vsvw, n_groups, o_cmp, o_slc, group_gates)

    (z_uv,) = _matmul(h, seg(8, 10), _ep_cast, [jax.ShapeDtypeStruct((s_len, 2 * sgu_w), F32)],
                      [_spec_tile], name="proj_sgu")
    o_b = _sgu(z_uv, p["sgu_norm"], p["sgu_w"], p["sgu_b"])

    (mq,) = _matmul(
        h, seg(10, 11), functools.partial(_ep_qrot, scale=scale),
        [jax.ShapeDtypeStruct((s_len, moba_w), BF16)], [_spec_tile],
        extras=(tile_gain(p["moba_q_norm"], moba_heads),) + rope_extras,
        extra_specs=(_spec_col,) + rope_specs, ts=HEAVY_EPILOGUE_TS, name="proj_moba_q")
    tm_k = min(1024, s_len)
    mk, mk_mean = _matmul(
        h, seg(11, 12), functools.partial(_ep_krot, block_mean=True),
        [jax.ShapeDtypeStruct((s_len, moba_w), BF16),
         jax.ShapeDtypeStruct((s_len // tm_k, tm_k // MOBA_BLOCK, moba_w), F32)],
        [_spec_tile, _spec_blockmean],
        extras=(tile_gain(p["moba_k_norm"], moba_heads),) + rope_extras,
        extra_specs=(_spec_col,) + rope_specs, tm=tm_k, ts=HEAVY_EPILOGUE_TS, name="proj_moba_k")
    (mv,) = _matmul(h, seg(12, 13), _ep_cast, [jax.ShapeDtypeStruct((s_len, moba_w), BF16)],
                    [_spec_tile], name="proj_moba_v")
    n_blk = s_len // MOBA_BLOCK
    assert n_blk <= LANES, n_blk
    k_mean = mk_mean.reshape(n_blk, moba_heads, HEAD_DIM).transpose(1, 0, 2)
    k_mean = jnp.pad(k_mean, ((0, 0), (0, LANES - n_blk), (0, 0))).astype(BF16)
    o_c = _flash_t(mq.T, _moba_select(mq, k_mean, n_blk), mk, _block_onehot(s_len, MOBA_BLOCK),
                   _tiles_t(mv, moba_heads, min(1024, s_len)), rep=1, tq=min(1024, s_len),
                   keys_per_var=LANES * MOBA_BLOCK, out_dtype=BF16, name="moba", n_split=4)

    (gm,) = _matmul(h, seg(13, 16), _ep_sigmoid, [jax.ShapeDtypeStruct((s_len, 3 * d_model), F32)],
                    [_spec_tile], name="proj_merge_gates")
    y = _merge(o_a, o_b, o_c, big["proj_a"], big["proj_b"], big["proj_c"], layer, gm)
    (x,) = _matmul(y, big["w_out"], _ep_residual, [jax.ShapeDtypeStruct((s_len, d_model), F32)],
                   [_spec_tile], extras=(x,), extra_specs=(_spec_tile,), b_layer=layer, name="out_proj")

    h2 = _rmsnorm(x, p["norm_mlp"])
    (hid,) = _matmul(h2, big["mlp_w1"], _ep_relu2,
                     [jax.ShapeDtypeStruct((s_len, big["mlp_w1"].shape[2]), BF16)], [_spec_tile],
                     b_layer=layer, name="mlp_up")
    (x,) = _matmul(hid, big["mlp_w2"], _ep_residual, [jax.ShapeDtypeStruct((s_len, d_model), F32)],
                   [_spec_tile], extras=(x,), extra_specs=(_spec_tile,), b_layer=layer, name="mlp_down")
    return x


_LAYER_PARAMS = ("norm_mix", "norm_mlp", "w_in", "nsa_gate_b", "nsa_q_norm", "nsa_kc_norm", "nsa_ks_norm",
                 "nsa_kw_norm", "phi_pe_k", "phi_w1_k", "phi_w2_k", "phi_pe_v", "phi_w1_v", "phi_w2_v",
                 "sgu_norm", "sgu_w", "sgu_b", "moba_q_norm", "moba_k_norm", "proj_a", "proj_b", "proj_c",
                 "w_out", "mlp_w1", "mlp_w2")
_BIG_PARAMS = ("proj_a", "proj_b", "proj_c", "w_out", "mlp_w1", "mlp_w2")


def kernel(x, positions, norm_mix, norm_mlp, w_in, nsa_gate_b, nsa_q_norm, nsa_kc_norm, nsa_ks_norm, nsa_kw_norm, phi_pe_k, phi_w1_k, phi_w2_k, phi_pe_v, phi_w1_v, phi_w2_v, sgu_norm, sgu_w, sgu_b, moba_q_norm, moba_k_norm, proj_a, proj_b, proj_c, w_out, mlp_w1, mlp_w2):
    stacked = dict(zip(_LAYER_PARAMS, (norm_mix, norm_mlp, w_in, nsa_gate_b, nsa_q_norm, nsa_kc_norm,
                                       nsa_ks_norm, nsa_kw_norm, phi_pe_k, phi_w1_k, phi_w2_k, phi_pe_v,
                                       phi_w1_v, phi_w2_v, sgu_norm, sgu_w, sgu_b, moba_q_norm, moba_k_norm,
                                       proj_a, proj_b, proj_c, w_out, mlp_w1, mlp_w2)))
    depth = w_in.shape[0]
    big = {k: stacked.pop(k).astype(BF16) for k in _BIG_PARAMS}
    inv = ROPE_THETA ** (-jnp.arange(0, HEAD_DIM, 2, dtype=F32) / HEAD_DIM)
    batch, s_len, d_model = x.shape
    outs = []
    for b in range(batch):
        ang = positions[b].astype(F32)[:, None] * inv
        cos = jnp.concatenate([jnp.cos(ang), jnp.cos(ang)], axis=-1)
        sin = jnp.concatenate([-jnp.sin(ang), jnp.sin(ang)], axis=-1)
        xb = x.reshape(s_len, d_model) if batch == 1 else x[b]
        for l in range(depth):
            xb = _layer(xb, cos, sin, {k: v[l] for k, v in stacked.items()}, big, l)
        outs.append(xb)
    return outs[0].reshape(1, s_len, d_model) if batch == 1 else jnp.stack(outs)
```

```python
import functools
import math

import jax
import jax.numpy as jnp
import numpy as np
from jax import lax
from jax.experimental import pallas as pl
from jax.experimental.pallas import tpu as pltpu

F32 = jnp.float32
BF16 = jnp.bfloat16

HEAD_DIM = 128
LANES = 128
ROPE_THETA = 10000.0
NORM_EPS = 1e-6
NEG = -1e30
BELOW_NEG = -3e38

NSA_REP = 4
CMP_LEN = 32
CMP_STRIDE = 16
SLC_LEN = 64
SLC_TOPK = 16
WINDOW = 512
SGU_GROUPS = 8
SGU_CHUNK = 128
MOBA_BLOCK = 256
MOBA_TOPK = 3

LIGHT_EPILOGUE_TS = 512
HEAVY_EPILOGUE_TS = 1024

MIB = 1024 * 1024
VMEM_LIMIT = 52 * MIB


def _cparams(sem, vmem=VMEM_LIMIT):
    return pltpu.CompilerParams(dimension_semantics=sem, vmem_limit_bytes=vmem)


def _tile(n, pref):
    if n <= pref:
        return n
    t = (pref // LANES) * LANES
    while t >= LANES:
        if n % t == 0:
            return t
        t -= LANES
    raise ValueError(f"no 128-multiple tile divides {n}")


def _gelu(x):
    c = math.sqrt(2.0 / math.pi)
    return 0.5 * x * (1.0 + jnp.tanh(c * (x + 0.044715 * (x * x * x))))


def _sigmoid(x):
    return 1.0 / (1.0 + jnp.exp(-x))


def _head_norm(x, gain):
    return x * lax.rsqrt(jnp.mean(x * x, axis=-1, keepdims=True) + NORM_EPS) * gain


def _rope(x, cos, sin_signed):
    return x * cos + pltpu.roll(x, HEAD_DIM // 2, 1) * sin_signed


def _rmsnorm_body(x_ref, g_ref, o_ref):
    x = x_ref[...]
    y = x * lax.rsqrt(jnp.mean(x * x, axis=-1, keepdims=True) + NORM_EPS)
    o_ref[...] = (y * g_ref[...]).astype(o_ref.dtype)


def _rmsnorm(x, gain, tm=256):
    m, d = x.shape
    tm = min(tm, m)
    return pl.pallas_call(
        _rmsnorm_body,
        grid=(m // tm,),
        in_specs=[pl.BlockSpec((tm, d), lambda i: (i, 0)),
                  pl.BlockSpec((1, d), lambda i: (0, 0))],
        out_specs=pl.BlockSpec((tm, d), lambda i: (i, 0)),
        out_shape=jax.ShapeDtypeStruct((m, d), BF16),
        compiler_params=_cparams(("parallel",)),
        name="rmsnorm",
    )(x, gain.reshape(1, d))


def _mm_body(*refs, n_extra, n_out, nk, tn, ts, epilogue):
    a_ref, b_ref = refs[0], refs[1]
    extra = refs[2:2 + n_extra]
    outs = refs[2 + n_extra:2 + n_extra + n_out]
    acc_ref = refs[-1] if nk > 1 else None

    def finish():
        for c0 in range(0, tn, ts):
            part = jnp.dot(a_ref[...], b_ref[:, c0:c0 + ts], preferred_element_type=F32)
            if nk > 1:
                part = part + acc_ref[:, c0:c0 + ts]
            epilogue(part, extra, outs, slice(c0, c0 + ts))

    if nk == 1:
        finish()
        return
    k = pl.program_id(2)
    split_finish = ts < tn

    @pl.when(k == 0)
    def _():
        acc_ref[...] = jnp.dot(a_ref[...], b_ref[...], preferred_element_type=F32)

    @pl.when((k > 0) & (k < nk - 1) if split_finish else k > 0)
    def _():
        acc_ref[...] += jnp.dot(a_ref[...], b_ref[...], preferred_element_type=F32)

    @pl.when(k == nk - 1)
    def _():
        if split_finish:
            finish()
        else:
            epilogue(acc_ref[...], extra, outs, slice(0, tn))


def _matmul(a, b, epilogue, out_shapes, out_specs, extras=(), extra_specs=(),
            tm=1024, tn=1024, tk=2048, ts=LIGHT_EPILOGUE_TS, b_layer=0, n=None, col0=0, col_step=1,
            name="matmul"):
    m, kdim = a.shape
    n = b.shape[-1] if n is None else n
    tm, tn, tk = min(tm, m), _tile(n, tn), _tile(kdim, tk)
    ts = min(ts, tn)
    assert col0 % tn == 0, (col0, tn)
    blk0 = col0 // tn
    if b.ndim == 3:
        b_spec = pl.BlockSpec((None, tk, tn), lambda i, j, k: (b_layer, k, blk0 + j * col_step))
    else:
        b_spec = pl.BlockSpec((tk, tn), lambda i, j, k: (k, blk0 + j * col_step))
    nk = kdim // tk
    body = functools.partial(_mm_body, n_extra=len(extras), n_out=len(out_shapes), nk=nk, tn=tn, ts=ts,
                             epilogue=epilogue)
    scratch = [] if nk == 1 else [pltpu.VMEM((tm, tn), F32)]
    return pl.pallas_call(
        body,
        grid=(m // tm, n // tn, nk),
        in_specs=[pl.BlockSpec((tm, tk), lambda i, j, k: (i, k)),
                  b_spec]
                 + [s(tm, tn) for s in extra_specs],
        out_specs=[s(tm, tn) for s in out_specs],
        out_shape=out_shapes,
        scratch_shapes=scratch,
        compiler_params=_cparams(("parallel", "parallel", "arbitrary")),
        name=name,
    )(a, b, *extras)


def _spec_tile(tm, tn):
    return pl.BlockSpec((tm, tn), lambda i, j, k: (i, j))


def _spec_col(tm, tn):
    return pl.BlockSpec((1, tn), lambda i, j, k: (0, j))


def _spec_rope(tm, tn):
    return pl.BlockSpec((tm, HEAD_DIM), lambda i, j, k: (i, 0))


def _spec_blockmean(tm, tn):
    return pl.BlockSpec((1, tm // MOBA_BLOCK, tn), lambda i, j, k: (i, 0, j))


def _ep_cast(acc, extra, outs, cols):
    outs[0][:, cols] = acc.astype(outs[0].dtype)


def _ep_sigmoid_bias(acc, extra, outs, cols):
    outs[0][:, cols] = _sigmoid(acc + extra[0][:, cols])


def _ep_sigmoid(acc, extra, outs, cols):
    outs[0][:, cols] = _sigmoid(acc)


def _ep_relu2(acc, extra, outs, cols):
    r = jnp.maximum(acc, 0.0)
    outs[0][:, cols] = (r * r).astype(outs[0].dtype)


def _ep_residual(acc, extra, outs, cols):
    outs[0][:, cols] = extra[0][:, cols] + acc


def _heads(cols):
    return [(slice(c - cols.start, c - cols.start + HEAD_DIM), slice(c, c + HEAD_DIM))
            for c in range(cols.start, cols.stop, HEAD_DIM)]


def _ep_q(acc, extra, outs, cols, *, scale):
    cos, sin = extra[1][...], extra[2][...]
    for a_sl, t_sl in _heads(cols):
        y = _head_norm(acc[:, a_sl], extra[0][:, t_sl])
        outs[0][:, t_sl] = (y * scale).astype(BF16)
        outs[1][:, t_sl] = (_rope(y, cos, sin) * scale).astype(BF16)


def _ep_qrot(acc, extra, outs, cols, *, scale):
    cos, sin = extra[1][...], extra[2][...]
    for a_sl, t_sl in _heads(cols):
        y = _head_norm(acc[:, a_sl], extra[0][:, t_sl])
        outs[0][:, t_sl] = (_rope(y, cos, sin) * scale).astype(BF16)


def _ep_krot(acc, extra, outs, cols, *, block_mean):
    cos, sin = extra[1][...], extra[2][...]
    for a_sl, t_sl in _heads(cols):
        y = _rope(_head_norm(acc[:, a_sl], extra[0][:, t_sl]), cos, sin)
        outs[0][:, t_sl] = y.astype(BF16)
        if block_mean:
            for blk in range(acc.shape[0] // MOBA_BLOCK):
                rows = y[blk * MOBA_BLOCK:(blk + 1) * MOBA_BLOCK]
                outs[1][0, blk:blk + 1, t_sl] = jnp.mean(rows, axis=0, keepdims=True)


def _compress_body(a_ref, pe_ref, w1_ref, w2_ref, g_ref, o_ref, *, norm):
    half = CMP_STRIDE * HEAD_DIM
    a = a_ref[0]
    n_chunk = a.shape[0]
    x1 = (a + pe_ref[:, :half]).astype(BF16)
    x2 = (a + pe_ref[:, half:]).astype(BF16)
    p1 = jnp.dot(x1, w1_ref[:half, :], preferred_element_type=F32)
    p2 = jnp.dot(x2, w1_ref[half:, :], preferred_element_type=F32)
    h = _gelu(p1 + pltpu.roll(p2, n_chunk - 1, 0))
    o = jnp.dot(h.astype(BF16), w2_ref[...], preferred_element_type=F32)
    if norm:
        o = _head_norm(o, g_ref[...])
    o_ref[0] = o.astype(o_ref.dtype)


def _compress(a, pe, w1, w2, gain, norm):
    g, n_chunk, width = a.shape
    hidden = w1.shape[1]
    return pl.pallas_call(
        functools.partial(_compress_body, norm=norm),
        grid=(g,),
        in_specs=[pl.BlockSpec((1, n_chunk, width), lambda i: (i, 0, 0)),
                  pl.BlockSpec((1, 2 * width), lambda i: (0, 0)),
                  pl.BlockSpec((2 * width, hidden), lambda i: (0, 0)),
                  pl.BlockSpec((hidden, HEAD_DIM), lambda i: (0, 0)),
                  pl.BlockSpec((1, HEAD_DIM), lambda i: (0, 0))],
        out_specs=pl.BlockSpec((1, n_chunk, HEAD_DIM), lambda i: (i, 0, 0)),
        out_shape=jax.ShapeDtypeStruct((g, n_chunk, HEAD_DIM), BF16),
        compiler_params=_cparams(("parallel",)),
        name="nsa_compress",
    )(a, pe.reshape(1, 2 * width), w1.astype(BF16), w2.astype(BF16), gain.reshape(1, HEAD_DIM))


def _topk_mask(score, index_f, k, axis):
    n = score.shape[axis]
    sel = jnp.zeros(score.shape, dtype=jnp.bool_)
    for _ in range(k):
        m = jnp.max(score, axis=axis, keepdims=True)
        first = jnp.min(jnp.where(score == m, index_f, float(n)), axis=axis, keepdims=True)
        hit = index_f == first
        sel = jnp.logical_or(sel, hit)
        score = jnp.where(hit, BELOW_NEG, score)
    return sel


def _nsa_cmp_body(q_ref, kt_ref, v_ref, ovt_ref, o_ref, sbt_ref, *, tq, n_slc, cw):
    qi = pl.program_id(1)
    q0 = qi * tq
    n_pad = kt_ref.shape[-1]
    q = jnp.concatenate([q_ref[:, r * HEAD_DIM:(r + 1) * HEAD_DIM] for r in range(NSA_REP)], axis=0)
    t1 = q0 + lax.broadcasted_iota(jnp.int32, (tq, 1), 0)
    t = jnp.concatenate([t1] * NSA_REP, axis=0)

    def attend(width):
        s = jnp.dot(q, kt_ref[0, :, :width], preferred_element_type=F32)
        cmp_end = lax.broadcasted_iota(jnp.int32, (1, width), 1) * CMP_STRIDE + (CMP_LEN - 1)
        s = jnp.where(cmp_end <= t, s, NEG)
        m = jnp.max(s, axis=-1, keepdims=True)
        e = jnp.exp2(s - m)
        inv = jnp.where(m > 0.5 * NEG, 1.0 / jnp.sum(e, axis=-1, keepdims=True), 0.0)
        p = e * inv
        o = jnp.dot(p.astype(BF16), v_ref[0, :width, :], preferred_element_type=F32)
        for r in range(NSA_REP):
            o_ref[:, r * HEAD_DIM:(r + 1) * HEAD_DIM] = o[r * tq:(r + 1) * tq]
        ps = p[0:tq]
        for r in range(1, NSA_REP):
            ps = ps + p[r * tq:(r + 1) * tq]
        ps_hi = ps.astype(BF16)
        ps_lo = (ps - ps_hi.astype(F32)).astype(BF16)
        nt = (((1,), (1,)), ((), ()))
        n_rows = width * CMP_STRIDE // SLC_LEN
        ovt = ovt_ref[:n_rows, :width]
        imp = (lax.dot_general(ovt, ps_hi, nt, preferred_element_type=F32)
               + lax.dot_general(ovt, ps_lo, nt, preferred_element_type=F32))
        j = lax.broadcasted_iota(jnp.int32, imp.shape, 0)
        cur = (q0 + lax.broadcasted_iota(jnp.int32, (1, tq), 1)) // SLC_LEN
        forced = (j == 0) | (j == cur) | (j == cur - 1)
        allowed = j <= cur
        score = jnp.where(allowed & jnp.logical_not(forced), imp, NEG)
        score = jnp.where(j < n_slc, score, BELOW_NEG)
        sel = _topk_mask(score, j.astype(F32), min(SLC_TOPK, n_slc) - 3, 0)
        sbt_ref[0, :n_rows, :] = jnp.where((sel | forced) & allowed, 0.0, NEG).astype(BF16)
        if n_rows < sbt_ref.shape[1]:
            sbt_ref[0, n_rows:, :] = jnp.full((sbt_ref.shape[1] - n_rows, tq), NEG, BF16)

    n_chunks = ((q0 + tq) // CMP_STRIDE - 1 + cw - 1) // cw
    for v in range(1, n_pad // cw + 1):
        pl.when(n_chunks == v)(functools.partial(attend, v * cw))


def _nsa_cmp(q_c, kc_t, vc, overlap_t, n_slc, tq=256):
    s_len = q_c.shape[0]
    tq = min(tq, s_len)
    g, _, n_pad = kc_t.shape
    n_slc_pad = overlap_t.shape[0]
    gw = NSA_REP * HEAD_DIM
    cw = min(256, n_pad)
    assert n_pad % cw == 0, (n_pad, cw)
    return pl.pallas_call(
        functools.partial(_nsa_cmp_body, tq=tq, n_slc=n_slc, cw=cw),
        grid=(g, s_len // tq),
        in_specs=[pl.BlockSpec((tq, gw), lambda gi, qi: (qi, gi)),
                  pl.BlockSpec((1, HEAD_DIM, n_pad), lambda gi, qi: (gi, 0, 0)),
                  pl.BlockSpec((1, n_pad, HEAD_DIM), lambda gi, qi: (gi, 0, 0)),
                  pl.BlockSpec((n_slc_pad, n_pad), lambda gi, qi: (0, 0))],
        out_specs=[pl.BlockSpec((tq, gw), lambda gi, qi: (qi, gi)),
                   pl.BlockSpec((1, n_slc_pad, tq), lambda gi, qi: (gi, 0, qi))],
        out_shape=[jax.ShapeDtypeStruct((s_len, g * gw), F32),
                   jax.ShapeDtypeStruct((g, n_slc_pad, s_len), BF16)],
        compiler_params=_cparams(("parallel", "parallel")),
        name="nsa_cmp_select",
    )(q_c, kc_t, vc, overlap_t)


def _flash_t_body(qt_ref, bt_ref, k_ref, oh_ref, vt_ref, o_ref, qa_s, m_s, l_s, acc_s, s_s, *,
                  rep, tq, tk, keys_per_var, n_var, n_split):
    width = rep * tq
    qi = pl.program_id(1)
    q0 = qi * tq
    for var in range(n_var):
        bt = bt_ref[0, var * LANES:(var + 1) * LANES, :]
        qa_s[var, :HEAD_DIM, :] = jnp.concatenate(
            [qt_ref[r * HEAD_DIM:(r + 1) * HEAD_DIM, :] for r in range(rep)], axis=1)
        qa_s[var, HEAD_DIM:, :] = jnp.concatenate([bt] * rep, axis=1)
    m_s[...] = jnp.full(m_s.shape, NEG, F32)
    l_s[...] = jnp.zeros(l_s.shape, F32)
    acc_s[...] = jnp.zeros(acc_s.shape, F32)
    t1 = q0 + lax.broadcasted_iota(jnp.int32, (1, tq), 1)
    t = jnp.concatenate([t1] * rep, axis=1)
    cw = width // n_split

    def key_tile(kt):
        start = pl.multiple_of(kt * tk, tk)
        return jnp.concatenate([k_ref[pl.ds(start, tk), :], oh_ref[pl.ds(start, tk), :]], axis=1)

    def scores(k_tile, kt, c):
        var = (kt * tk) // keys_per_var if n_var > 1 else 0
        return jnp.dot(k_tile, qa_s[var, :, c * cw:(c + 1) * cw], preferred_element_type=F32)

    def consume(kt, c, s, causal):
        cs = slice(c * cw, (c + 1) * cw)
        if causal:
            pos = kt * tk + lax.broadcasted_iota(jnp.int32, (tk, 1), 0)
            s = jnp.where(pos <= t[:, cs], s, NEG)
        m_old = m_s[:, cs]
        m_new = jnp.maximum(m_old, jnp.max(s, axis=0, keepdims=True))
        alpha = jnp.exp2(m_old - m_new)
        p = jnp.exp2(s - m_new)
        l_s[:, cs] = alpha * l_s[:, cs] + jnp.sum(p, axis=0, keepdims=True)
        m_s[:, cs] = m_new
        return alpha, p.astype(BF16)

    def step(kt, causal, prefetch, rd):
        v_t = vt_ref[0, kt]
        if prefetch:
            k_next = key_tile(kt + 1)
            for c in range(n_split):
                s_s[1 - rd, :, c * cw:(c + 1) * cw] = scores(k_next, kt + 1, c)
        for c in range(n_split):
            cs = slice(c * cw, (c + 1) * cw)
            alpha, p = consume(kt, c, s_s[rd, :, cs], causal)
            acc_s[:, cs] = alpha * acc_s[:, cs] + jnp.dot(v_t, p, preferred_element_type=F32)

    n_full = q0 // tk
    n_diag = max(1, tq // tk)
    k_first = key_tile(0)
    for c in range(n_split):
        s_s[0, :, c * cw:(c + 1) * cw] = scores(k_first, 0, c)

    def two_steps(j, carry):
        step(2 * j, False, True, 0)
        step(2 * j + 1, False, True, 1)
        return carry

    def diagonal(rd):
        for d in range(n_diag):
            step(n_full + d, True, d < n_diag - 1, (rd + d) % 2)

    lax.fori_loop(0, n_full // 2, two_steps, 0)

    @pl.when(n_full % 2 == 1)
    def _():
        step(n_full - 1, False, True, 0)
        diagonal(1)

    @pl.when(n_full % 2 == 0)
    def _():
        diagonal(0)

    o_t = acc_s[...] / l_s[...]
    for r in range(rep):
        o_ref[:, r * HEAD_DIM:(r + 1) * HEAD_DIM] = o_t[:, r * tq:(r + 1) * tq].T.astype(o_ref.dtype)


def _flash_t(q_t, bias_t, k, block_onehot, v_t, *, rep, tq, keys_per_var, out_dtype, name, n_split):
    s_len = q_t.shape[1]
    g, n_kt, _, tk = v_t.shape
    gw = rep * HEAD_DIM
    width = rep * tq
    n_var = bias_t.shape[1] // LANES
    assert (tk % tq == 0 or tq % tk == 0) and n_split % 2 == 0 and width % (n_split * LANES) == 0, \
        (tq, tk, width, n_split)
    return pl.pallas_call(
        functools.partial(_flash_t_body, rep=rep, tq=tq, tk=tk, keys_per_var=keys_per_var,
                          n_var=n_var, n_split=n_split),
        grid=(g, s_len // tq),
        in_specs=[pl.BlockSpec((gw, tq), lambda gi, qi: (gi, qi)),
                  pl.BlockSpec((1, n_var * LANES, tq), lambda gi, qi: (gi, 0, qi)),
                  pl.BlockSpec((s_len, HEAD_DIM), lambda gi, qi: (0, gi)),
                  pl.BlockSpec((s_len, LANES), lambda gi, qi: (0, 0)),
                  pl.BlockSpec((1, n_kt, HEAD_DIM, tk), lambda gi, qi: (gi, 0, 0, 0))],
        out_specs=pl.BlockSpec((tq, gw), lambda gi, qi: (qi, gi)),
        out_shape=jax.ShapeDtypeStruct((s_len, g * gw), out_dtype),
        scratch_shapes=[pltpu.VMEM((n_var, 2 * HEAD_DIM, width), BF16),
                        pltpu.VMEM((1, width), F32),
                        pltpu.VMEM((1, width), F32),
                        pltpu.VMEM((HEAD_DIM, width), F32),
                        pltpu.VMEM((2, tk, width), F32)],
        compiler_params=_cparams(("parallel", "arbitrary")),
        name=name,
    )(q_t, bias_t, k, block_onehot, v_t)


def _moba_select_body(q_ref, km_ref, bt_ref, *, tq, n_blk):
    q0 = pl.program_id(1) * tq
    nt = (((1,), (1,)), ((), ()))
    sg = lax.dot_general(km_ref[0], q_ref[...], nt, preferred_element_type=F32)
    j = lax.broadcasted_iota(jnp.int32, sg.shape, 0)
    cur = (q0 + lax.broadcasted_iota(jnp.int32, (1, tq), 1)) // MOBA_BLOCK
    past = j < cur
    score = jnp.where(j < n_blk, jnp.where(past, sg, NEG), BELOW_NEG)
    sel = _topk_mask(score, j.astype(F32), min(MOBA_TOPK, n_blk), 0)
    bt_ref[0] = jnp.where((sel & past) | (j == cur), 0.0, NEG).astype(BF16)


def _moba_select(q, k_mean, n_blk, tq=512):
    s_len = q.shape[0]
    tq = min(tq, s_len)
    n_heads = k_mean.shape[0]
    return pl.pallas_call(
        functools.partial(_moba_select_body, tq=tq, n_blk=n_blk),
        grid=(n_heads, s_len // tq),
        in_specs=[pl.BlockSpec((tq, HEAD_DIM), lambda hi, qi: (qi, hi)),
                  pl.BlockSpec((1, LANES, HEAD_DIM), lambda hi, qi: (hi, 0, 0))],
        out_specs=pl.BlockSpec((1, LANES, tq), lambda hi, qi: (hi, 0, qi)),
        out_shape=jax.ShapeDtypeStruct((n_heads, LANES, s_len), BF16),
        compiler_params=_cparams(("parallel", "parallel")),
        name="moba_select",
    )(q, k_mean)


def _window_body(q_ref, kt_ref, v_ref, oc_ref, os_ref, g_ref, o_ref, *, tq):
    qi = pl.program_id(1)
    q0 = qi * tq
    n_past = WINDOW // tq
    q = jnp.concatenate([q_ref[:, r * HEAD_DIM:(r + 1) * HEAD_DIM] for r in range(NSA_REP)], axis=0)
    t1 = q0 + lax.broadcasted_iota(jnp.int32, (tq, 1), 0)
    t = jnp.concatenate([t1] * NSA_REP, axis=0)
    lane = lax.broadcasted_iota(jnp.int32, (1, tq), 1)
    ones = jnp.ones((tq, HEAD_DIM), BF16)
    scores, tiles = [], []
    for i in range(n_past + 1):
        raw = qi - n_past + i
        idx = jnp.maximum(raw, 0)
        tiles.append(idx)
        s = jnp.dot(q, kt_ref[0, idx], preferred_element_type=F32)
        pos = raw * tq + lane
        if i == 0:
            s = jnp.where(pos > t - WINDOW, s, NEG)
        if i == n_past:
            s = jnp.where(pos <= t, s, NEG)
        else:
            s = s + jnp.where(raw >= 0, 0.0, NEG)
        scores.append(s)
    m = jnp.max(scores[0], axis=-1, keepdims=True)
    for s in scores[1:]:
        m = jnp.maximum(m, jnp.max(s, axis=-1, keepdims=True))
    acc = jnp.zeros((NSA_REP * tq, 2 * HEAD_DIM), F32)
    for i, s in enumerate(scores):
        start = pl.multiple_of(tiles[i] * tq, tq)
        v_aug = jnp.concatenate([v_ref[pl.ds(start, tq), :], ones], axis=1)
        acc = acc + jnp.dot(jnp.exp2(s - m).astype(BF16), v_aug, preferred_element_type=F32)
    o_win = acc[:, :HEAD_DIM] / acc[:, HEAD_DIM:]
    gate = g_ref[0]
    for r in range(NSA_REP):
        sl = slice(r * HEAD_DIM, (r + 1) * HEAD_DIM)
        o = (gate[:, r:r + 1] * oc_ref[:, sl]
             + gate[:, NSA_REP + r:NSA_REP + r + 1] * os_ref[:, sl]
             + gate[:, 2 * NSA_REP + r:2 * NSA_REP + r + 1] * o_win[r * tq:(r + 1) * tq])
        o_ref[:, sl] = o.astype(o_ref.dtype)


def _window_combine(q_r, kw_t, v, v_block0, o_cmp, o_slc, gates):
    s_len = q_r.shape[0]
    g, n_kt, _, tq = kw_t.shape
    assert WINDOW % tq == 0, tq
    gw = NSA_REP * HEAD_DIM
    tile = pl.BlockSpec((tq, gw), lambda gi, qi: (qi, gi))
    return pl.pallas_call(
        functools.partial(_window_body, tq=tq),
        grid=(g, s_len // tq),
        in_specs=[tile,
                  pl.BlockSpec((1, n_kt, HEAD_DIM, tq), lambda gi, qi: (gi, 0, 0, 0)),
                  pl.BlockSpec((s_len, HEAD_DIM), lambda gi, qi: (0, v_block0 + gi)),
                  tile, tile,
                  pl.BlockSpec((1, tq, LANES), lambda gi, qi: (gi, qi, 0))],
        out_specs=tile,
        out_shape=jax.ShapeDtypeStruct((s_len, g * gw), BF16),
        compiler_params=_cparams(("parallel", "parallel")),
        name="nsa_window_combine",
    )(q_r, kw_t, v, o_cmp, o_slc, gates)


def _sgu_body(zu_ref, zv_ref, gain_ref, w_ref, bt_ref, o_ref, *, tm):
    gd = zu_ref.shape[1] // SGU_GROUPS
    row = lax.broadcasted_iota(jnp.int32, (SGU_CHUNK, SGU_CHUNK), 0)
    col = lax.broadcasted_iota(jnp.int32, (SGU_CHUNK, SGU_CHUNK), 1)
    bt = bt_ref[...]
    for g in range(SGU_GROUPS):
        sl = slice(g * gd, (g + 1) * gd)
        v = _gelu(zv_ref[:, sl])
        v = (v * lax.rsqrt(jnp.mean(v * v, axis=-1, keepdims=True) + NORM_EPS) * gain_ref[:, sl]).astype(BF16)
        w = jnp.where(col <= row, w_ref[g], 0.0).astype(BF16)
        for c in range(tm // SGU_CHUNK):
            rs = slice(c * SGU_CHUNK, (c + 1) * SGU_CHUNK)
            mixed = jnp.dot(w, v[rs], preferred_element_type=F32) + bt[:, g:g + 1]
            o_ref[rs, sl] = (_gelu(zu_ref[rs, sl]) * mixed).astype(o_ref.dtype)


def _sgu(z_uv, gain, w_s, b_s, tm=512):
    s_len = z_uv.shape[0]
    width = z_uv.shape[1] // 2
    tm = min(tm, s_len)
    return pl.pallas_call(
        functools.partial(_sgu_body, tm=tm),
        grid=(s_len // tm,),
        in_specs=[pl.BlockSpec((tm, width), lambda i: (i, 0)),
                  pl.BlockSpec((tm, width), lambda i: (i, 1)),
                  pl.BlockSpec((1, width), lambda i: (0, 0)),
                  pl.BlockSpec((SGU_GROUPS, SGU_CHUNK, SGU_CHUNK), lambda i: (0, 0, 0)),
                  pl.BlockSpec((SGU_CHUNK, SGU_GROUPS), lambda i: (0, 0))],
        out_specs=pl.BlockSpec((tm, width), lambda i: (i, 0)),
        out_shape=jax.ShapeDtypeStruct((s_len, width), BF16),
        compiler_params=_cparams(("parallel",)),
        name="sgu",
    )(z_uv, z_uv, gain.reshape(1, width), w_s, b_s.T)


def _merge_body(oa_ref, ob_ref, oc_ref, pa_ref, pb_ref, pc_ref, ga_ref, gb_ref, gc_ref, y_ref):
    y = ga_ref[...] * jnp.dot(oa_ref[...], pa_ref[...], preferred_element_type=F32)
    y = y + gb_ref[...] * jnp.dot(ob_ref[...], pb_ref[...], preferred_element_type=F32)
    y = y + gc_ref[...] * jnp.dot(oc_ref[...], pc_ref[...], preferred_element_type=F32)
    y_ref[...] = y.astype(y_ref.dtype)


def _merge(o_a, o_b, o_c, p_a, p_b, p_c, layer, gm, tm=512, tn=1024):
    s_len = o_a.shape[0]
    d = p_a.shape[2]
    tm, tn = min(tm, s_len), _tile(d, tn)
    nj = d // tn

    def rows(w):
        return pl.BlockSpec((tm, w), lambda i, j: (i, 0))

    def cols(kdim):
        return pl.BlockSpec((None, kdim, tn), lambda i, j: (layer, 0, j))

    def gate(off):
        return pl.BlockSpec((tm, tn), lambda i, j: (i, off * nj + j))

    return pl.pallas_call(
        _merge_body,
        grid=(s_len // tm, nj),
        in_specs=[rows(o_a.shape[1]), rows(o_b.shape[1]), rows(o_c.shape[1]),
                  cols(p_a.shape[1]), cols(p_b.shape[1]), cols(p_c.shape[1]),
                  gate(0), gate(1), gate(2)],
        out_specs=pl.BlockSpec((tm, tn), lambda i, j: (i, j)),
        out_shape=jax.ShapeDtypeStruct((s_len, d), BF16),
        compiler_params=_cparams(("parallel", "parallel")),
        name="gated_merge",
    )(o_a, o_b, o_c, p_a, p_b, p_c, gm, gm, gm)


def _block_onehot(s_len, block):
    key = np.arange(s_len)
    return jnp.asarray(((key // block) % LANES)[:, None] == np.arange(LANES)[None, :], dtype=BF16)


def _tiles_t(v, n_heads, tk):
    s_len = v.shape[0]
    return v.reshape(s_len // tk, tk, n_heads, HEAD_DIM).transpose(2, 0, 3, 1)


def _overlap_t(n_pad, n_slc_pad):
    i = np.arange(n_pad)[None, :]
    j = np.arange(n_slc_pad)[:, None]
    ov = (i * CMP_STRIDE <= j * SLC_LEN + SLC_LEN - 1) & (i * CMP_STRIDE + CMP_LEN - 1 >= j * SLC_LEN)
    return jnp.asarray(ov, dtype=BF16)


def _layer(x, cos, sin, p, big, layer):
    s_len, d_model = x.shape
    scale = HEAD_DIM ** -0.5 * math.log2(math.e)
    nsa_w = big["proj_a"].shape[1]
    sgu_w = big["proj_b"].shape[1]
    moba_w = big["proj_c"].shape[1]
    n_heads = nsa_w // HEAD_DIM
    n_groups = n_heads // NSA_REP
    kv_w = n_groups * HEAD_DIM
    moba_heads = moba_w // HEAD_DIM
    sizes = (nsa_w, kv_w, kv_w, kv_w, kv_w, kv_w, kv_w, 3 * n_heads, sgu_w, sgu_w,
             moba_w, moba_w, moba_w, d_model, d_model, d_model)
    offs = np.concatenate([[0], np.cumsum(sizes)])

    w_head, w_tail = big["w_head"], big["w_tail"]

    def head(a):
        return dict(b_layer=layer, col0=int(offs[a]))

    def tail(a):
        return dict(b_layer=layer, col0=int(offs[a] - offs[8]))

    def tile_gain(gain, reps):
        return jnp.tile(gain, reps).reshape(1, reps * HEAD_DIM)

    h = _rmsnorm(x, p["norm_mix"])
    rope_extras = (cos, sin)
    rope_specs = (_spec_rope, _spec_rope)

    q_c, q_r = _matmul(
        h, w_head, functools.partial(_ep_q, scale=scale),
        [jax.ShapeDtypeStruct((s_len, nsa_w), BF16)] * 2, [_spec_tile, _spec_tile],
        extras=(tile_gain(p["nsa_q_norm"], n_heads),) + rope_extras,
        extra_specs=(_spec_col,) + rope_specs, ts=HEAVY_EPILOGUE_TS, n=nsa_w, **head(0), name="proj_nsa_q")
    (kcvc,) = _matmul(h, w_head, _ep_cast, [jax.ShapeDtypeStruct((s_len, 2 * kv_w), F32)],
                      [_spec_tile], n=2 * kv_w, **head(1), name="proj_nsa_cmp_kv")
    (kskw,) = _matmul(
        h, w_head, functools.partial(_ep_krot, block_mean=False),
        [jax.ShapeDtypeStruct((s_len, 2 * kv_w), BF16)], [_spec_tile],
        extras=(jnp.concatenate([tile_gain(p["nsa_ks_norm"], n_groups),
                                 tile_gain(p["nsa_kw_norm"], n_groups)], axis=1),) + rope_extras,
        extra_specs=(_spec_col,) + rope_specs, ts=HEAVY_EPILOGUE_TS,
        n=2 * kv_w, tn=kv_w, col_step=2, **head(3), name="proj_nsa_k")
    (vsvw,) = _matmul(h, w_head, _ep_cast, [jax.ShapeDtypeStruct((s_len, 2 * kv_w), BF16)], [_spec_tile],
                      n=2 * kv_w, tn=kv_w, col_step=2, **head(4), name="proj_nsa_v")
    n_gate = 3 * n_heads
    b_gate = jnp.pad(p["nsa_gate_b"], (0, LANES - n_gate)).reshape(1, LANES)
    (gates,) = _matmul(h, w_head, _ep_sigmoid_bias, [jax.ShapeDtypeStruct((s_len, LANES), F32)],
                       [_spec_tile], extras=(b_gate,), extra_specs=(_spec_col,),
                       n=LANES, **head(7), name="proj_nsa_gates")

    n_chunk = s_len // CMP_STRIDE
    chunks = kcvc.reshape(n_chunk, CMP_STRIDE, 2, n_groups, HEAD_DIM).transpose(2, 3, 0, 1, 4)
    chunks = chunks.reshape(2, n_groups, n_chunk, CMP_STRIDE * HEAD_DIM)
    kc = _compress(chunks[0], p["phi_pe_k"], p["phi_w1_k"], p["phi_w2_k"], p["nsa_kc_norm"], True)
    vc = _compress(chunks[1], p["phi_pe_v"], p["phi_w1_v"], p["phi_w2_v"], p["nsa_kc_norm"], False)

    n_slc = s_len // SLC_LEN
    n_slc_pad = -(-n_slc // LANES) * LANES
    o_cmp, sel_bias_t = _nsa_cmp(q_c, jnp.swapaxes(kc, 1, 2), vc, _overlap_t(n_chunk, n_slc_pad), n_slc)
    tk = min(512, s_len)
    o_slc = _flash_t(q_r.T, sel_bias_t, kskw, _block_onehot(s_len, SLC_LEN),
                     _tiles_t(vsvw[:, :kv_w], n_groups, tk), rep=NSA_REP, tq=min(512, s_len),
                     keys_per_var=LANES * SLC_LEN, out_dtype=F32, name="nsa_selected", n_split=8)
    kw_t = _tiles_t(kskw[:, kv_w:], n_groups, min(256, s_len))
    group_gates = gates[:, :n_gate].reshape(s_len, 3, n_groups, NSA_REP).transpose(2, 0, 1, 3)
    group_gates = jnp.pad(group_gates.reshape(n_groups, s_len, 3 * NSA_REP),
                          ((0, 0), (0, 0), (0, LANES - 3 * NSA_REP)))
    o_a = _window_combine(q_r, kw_t, vsvw, n_groups, o_cmp, o_slc, group_gates)

    (z_uv,) = _matmul(h, w_tail, _ep_cast, [jax.ShapeDtypeStruct((s_len, 2 * sgu_w), F32)],
                      [_spec_tile], n=2 * sgu_w, **tail(8), name="proj_sgu")
    o_b = _sgu(z_uv, p["sgu_norm"], p["sgu_w"], p["sgu_b"])

    (mq,) = _matmul(
        h, w_tail, functools.partial(_ep_qrot, scale=scale),
        [jax.ShapeDtypeStruct((s_len, moba_w), BF16)], [_spec_tile],
        extras=(tile_gain(p["moba_q_norm"], moba_heads),) + rope_extras,
        extra_specs=(_spec_col,) + rope_specs, ts=HEAVY_EPILOGUE_TS, n=moba_w, **tail(10), name="proj_moba_q")
    tm_k = min(1024, s_len)
    mk, mk_mean = _matmul(
        h, w_tail, functools.partial(_ep_krot, block_mean=True),
        [jax.ShapeDtypeStruct((s_len, moba_w), BF16),
         jax.ShapeDtypeStruct((s_len // tm_k, tm_k // MOBA_BLOCK, moba_w), F32)],
        [_spec_tile, _spec_blockmean],
        extras=(tile_gain(p["moba_k_norm"], moba_heads),) + rope_extras,
        extra_specs=(_spec_col,) + rope_specs, tm=tm_k, ts=HEAVY_EPILOGUE_TS,
        n=moba_w, **tail(11), name="proj_moba_k")
    (mv,) = _matmul(h, w_tail, _ep_cast, [jax.ShapeDtypeStruct((s_len, moba_w), BF16)],
                    [_spec_tile], n=moba_w, **tail(12), name="proj_moba_v")
    n_blk = s_len // MOBA_BLOCK
    assert n_blk <= LANES, n_blk
    k_mean = mk_mean.reshape(n_blk, moba_heads, HEAD_DIM).transpose(1, 0, 2)
    k_mean = jnp.pad(k_mean, ((0, 0), (0, LANES - n_blk), (0, 0))).astype(BF16)
    o_c = _flash_t(mq.T, _moba_select(mq, k_mean, n_blk), mk, _block_onehot(s_len, MOBA_BLOCK),
                   _tiles_t(mv, moba_heads, min(1024, s_len)), rep=1, tq=min(1024, s_len),
                   keys_per_var=LANES * MOBA_BLOCK, out_dtype=BF16, name="moba", n_split=4)

    (gm,) = _matmul(h, w_tail, _ep_sigmoid, [jax.ShapeDtypeStruct((s_len, 3 * d_model), F32)],
                    [_spec_tile], n=3 * d_model, **tail(13), name="proj_merge_gates")
    y = _merge(o_a, o_b, o_c, big["proj_a"], big["proj_b"], big["proj_c"], layer, gm)
    (x,) = _matmul(y, big["w_out"], _ep_residual, [jax.ShapeDtypeStruct((s_len, d_model), F32)],
                   [_spec_tile], extras=(x,), extra_specs=(_spec_tile,), b_layer=layer, name="out_proj")

    h2 = _rmsnorm(x, p["norm_mlp"])
    (hid,) = _matmul(h2, big["mlp_w1"], _ep_relu2,
                     [jax.ShapeDtypeStruct((s_len, big["mlp_w1"].shape[2]), BF16)], [_spec_tile],
                     b_layer=layer, name="mlp_up")
    (x,) = _matmul(hid, big["mlp_w2"], _ep_residual, [jax.ShapeDtypeStruct((s_len, d_model), F32)],
                   [_spec_tile], extras=(x,), extra_specs=(_spec_tile,), b_layer=layer, name="mlp_down")
    return x


_LAYER_PARAMS = ("norm_mix", "norm_mlp", "w_in", "nsa_gate_b", "nsa_q_norm", "nsa_kc_norm", "nsa_ks_norm",
                 "nsa_kw_norm", "phi_pe_k", "phi_w1_k", "phi_w2_k", "phi_pe_v", "phi_w1_v", "phi_w2_v",
                 "sgu_norm", "sgu_w", "sgu_b", "moba_q_norm", "moba_k_norm", "proj_a", "proj_b", "proj_c",
                 "w_out", "mlp_w1", "mlp_w2")
_BIG_PARAMS = ("proj_a", "proj_b", "proj_c", "w_out", "mlp_w1", "mlp_w2")


def kernel(x, positions, norm_mix, norm_mlp, w_in, nsa_gate_b, nsa_q_norm, nsa_kc_norm, nsa_ks_norm, nsa_kw_norm, phi_pe_k, phi_w1_k, phi_w2_k, phi_pe_v, phi_w1_v, phi_w2_v, sgu_norm, sgu_w, sgu_b, moba_q_norm, moba_k_norm, proj_a, proj_b, proj_c, w_out, mlp_w1, mlp_w2):
    stacked = dict(zip(_LAYER_PARAMS, (norm_mix, norm_mlp, w_in, nsa_gate_b, nsa_q_norm, nsa_kc_norm,
                                       nsa_ks_norm, nsa_kw_norm, phi_pe_k, phi_w1_k, phi_w2_k, phi_pe_v,
                                       phi_w1_v, phi_w2_v, sgu_norm, sgu_w, sgu_b, moba_q_norm, moba_k_norm,
                                       proj_a, proj_b, proj_c, w_out, mlp_w1, mlp_w2)))
    depth = w_in.shape[0]
    big = {k: stacked.pop(k).astype(BF16) for k in _BIG_PARAMS}
    nsa_w = proj_a.shape[1]
    n_gate_end = nsa_w + 6 * (nsa_w // NSA_REP) + 3 * (nsa_w // HEAD_DIM)
    del stacked["w_in"]
    big["w_head"] = w_in[:, :, :-(-n_gate_end // LANES) * LANES].astype(BF16)
    big["w_tail"] = w_in[:, :, n_gate_end:].astype(BF16)
    inv = ROPE_THETA ** (-jnp.arange(0, HEAD_DIM, 2, dtype=F32) / HEAD_DIM)
    batch, s_len, d_model = x.shape
    outs = []
    for b in range(batch):
        ang = positions[b].astype(F32)[:, None] * inv
        cos = jnp.concatenate([jnp.cos(ang), jnp.cos(ang)], axis=-1)
        sin = jnp.concatenate([-jnp.sin(ang), jnp.sin(ang)], axis=-1)
        xb = x.reshape(s_len, d_model) if batch == 1 else x[b]
        for l in range(depth):
            xb = _layer(xb, cos, sin, {k: v[l] for k, v in stacked.items()}, big, l)
        outs.append(xb)
    return outs[0].reshape(1, s_len, d_model) if batch == 1 else jnp.stack(outs)
```

```python
import functools
import math

import jax
import jax.numpy as jnp
import numpy as np
from jax import lax
from jax.experimental import pallas as pl
from jax.experimental.pallas import tpu as pltpu

F32 = jnp.float32
BF16 = jnp.bfloat16

HEAD_DIM = 128
LANES = 128
ROPE_THETA = 10000.0
NORM_EPS = 1e-6
NEG = -1e30
BELOW_NEG = -3e38

NSA_REP = 4
CMP_LEN = 32
CMP_STRIDE = 16
SLC_LEN = 64
SLC_TOPK = 16
WINDOW = 512
SGU_GROUPS = 8
SGU_CHUNK = 128
MOBA_BLOCK = 256
MOBA_TOPK = 3

LIGHT_EPILOGUE_TS = 512
HEAVY_EPILOGUE_TS = 1024

MIB = 1024 * 1024
VMEM_LIMIT = 52 * MIB


def _cparams(sem, vmem=VMEM_LIMIT):
    return pltpu.CompilerParams(dimension_semantics=sem, vmem_limit_bytes=vmem)


def _tile(n, pref):
    if n <= pref:
        return n
    t = (pref // LANES) * LANES
    while t >= LANES:
        if n % t == 0:
            return t
        t -= LANES
    raise ValueError(f"no 128-multiple tile divides {n}")


def _gelu(x):
    c = math.sqrt(2.0 / math.pi)
    return 0.5 * x * (1.0 + jnp.tanh(c * (x + 0.044715 * (x * x * x))))


def _sigmoid(x):
    return 1.0 / (1.0 + jnp.exp(-x))


def _head_norm(x, gain):
    return x * lax.rsqrt(jnp.mean(x * x, axis=-1, keepdims=True) + NORM_EPS) * gain


def _rope(x, cos, sin_signed):
    return x * cos + pltpu.roll(x, HEAD_DIM // 2, 1) * sin_signed


def _rmsnorm_body(x_ref, g_ref, o_ref):
    x = x_ref[...]
    y = x * lax.rsqrt(jnp.mean(x * x, axis=-1, keepdims=True) + NORM_EPS)
    o_ref[...] = (y * g_ref[...]).astype(o_ref.dtype)


def _rmsnorm(x, gain, tm=256):
    m, d = x.shape
    tm = min(tm, m)
    return pl.pallas_call(
        _rmsnorm_body,
        grid=(m // tm,),
        in_specs=[pl.BlockSpec((tm, d), lambda i: (i, 0)),
                  pl.BlockSpec((1, d), lambda i: (0, 0))],
        out_specs=pl.BlockSpec((tm, d), lambda i: (i, 0)),
        out_shape=jax.ShapeDtypeStruct((m, d), BF16),
        compiler_params=_cparams(("parallel",)),
        name="rmsnorm",
    )(x, gain.reshape(1, d))


def _mm_body(*refs, n_extra, n_out, nk, tn, ts, epilogue):
    a_ref, b_ref = refs[0], refs[1]
    extra = refs[2:2 + n_extra]
    outs = refs[2 + n_extra:2 + n_extra + n_out]
    acc_ref = refs[-1] if nk > 1 else None

    def finish():
        for c0 in range(0, tn, ts):
            part = jnp.dot(a_ref[...], b_ref[:, c0:c0 + ts], preferred_element_type=F32)
            if nk > 1:
                part = part + acc_ref[:, c0:c0 + ts]
            epilogue(part, extra, outs, slice(c0, c0 + ts))

    if nk == 1:
        finish()
        return
    k = pl.program_id(2)
    split_finish = ts < tn

    @pl.when(k == 0)
    def _():
        acc_ref[...] = jnp.dot(a_ref[...], b_ref[...], preferred_element_type=F32)

    @pl.when((k > 0) & (k < nk - 1) if split_finish else k > 0)
    def _():
        acc_ref[...] += jnp.dot(a_ref[...], b_ref[...], preferred_element_type=F32)

    @pl.when(k == nk - 1)
    def _():
        if split_finish:
            finish()
        else:
            epilogue(acc_ref[...], extra, outs, slice(0, tn))


def _matmul(a, b, epilogue, out_shapes, out_specs, extras=(), extra_specs=(),
            tm=1024, tn=1024, tk=2048, ts=LIGHT_EPILOGUE_TS, b_layer=0, name="matmul"):
    m, kdim = a.shape
    n = b.shape[-1]
    tm, tn, tk = min(tm, m), _tile(n, tn), _tile(kdim, tk)
    ts = min(ts, tn)
    if b.ndim == 3:
        b_spec = pl.BlockSpec((None, tk, tn), lambda i, j, k: (b_layer, k, j))
    else:
        b_spec = pl.BlockSpec((tk, tn), lambda i, j, k: (k, j))
    nk = kdim // tk
    body = functools.partial(_mm_body, n_extra=len(extras), n_out=len(out_shapes), nk=nk, tn=tn, ts=ts,
                             epilogue=epilogue)
    scratch = [] if nk == 1 else [pltpu.VMEM((tm, tn), F32)]
    return pl.pallas_call(
        body,
        grid=(m // tm, n // tn, nk),
        in_specs=[pl.BlockSpec((tm, tk), lambda i, j, k: (i, k)),
                  b_spec]
                 + [s(tm, tn) for s in extra_specs],
        out_specs=[s(tm, tn) for s in out_specs],
        out_shape=out_shapes,
        scratch_shapes=scratch,
        compiler_params=_cparams(("parallel", "parallel", "arbitrary")),
        name=name,
    )(a, b, *extras)


def _spec_tile(tm, tn):
    return pl.BlockSpec((tm, tn), lambda i, j, k: (i, j))


def _spec_col(tm, tn):
    return pl.BlockSpec((1, tn), lambda i, j, k: (0, j))


def _spec_rope(tm, tn):
    return pl.BlockSpec((tm, HEAD_DIM), lambda i, j, k: (i, 0))


def _spec_blockmean(tm, tn):
    return pl.BlockSpec((1, tm // MOBA_BLOCK, tn), lambda i, j, k: (i, 0, j))


def _ep_cast(acc, extra, outs, cols):
    outs[0][:, cols] = acc.astype(outs[0].dtype)


def _ep_sigmoid_bias(acc, extra, outs, cols):
    outs[0][:, cols] = _sigmoid(acc + extra[0][:, cols])


def _ep_sigmoid(acc, extra, outs, cols):
    outs[0][:, cols] = _sigmoid(acc)


def _ep_relu2(acc, extra, outs, cols):
    r = jnp.maximum(acc, 0.0)
    outs[0][:, cols] = (r * r).astype(outs[0].dtype)


def _ep_residual(acc, extra, outs, cols):
    outs[0][:, cols] = extra[0][:, cols] + acc


def _heads(cols):
    return [(slice(c - cols.start, c - cols.start + HEAD_DIM), slice(c, c + HEAD_DIM))
            for c in range(cols.start, cols.stop, HEAD_DIM)]


def _ep_q(acc, extra, outs, cols, *, scale):
    cos, sin = extra[1][...], extra[2][...]
    for a_sl, t_sl in _heads(cols):
        y = _head_norm(acc[:, a_sl], extra[0][:, t_sl])
        outs[0][:, t_sl] = (y * scale).astype(BF16)
        outs[1][:, t_sl] = (_rope(y, cos, sin) * scale).astype(BF16)


def _ep_qrot(acc, extra, outs, cols, *, scale):
    cos, sin = extra[1][...], extra[2][...]
    for a_sl, t_sl in _heads(cols):
        y = _head_norm(acc[:, a_sl], extra[0][:, t_sl])
        outs[0][:, t_sl] = (_rope(y, cos, sin) * scale).astype(BF16)


def _ep_krot(acc, extra, outs, cols, *, block_mean):
    cos, sin = extra[1][...], extra[2][...]
    for a_sl, t_sl in _heads(cols):
        y = _rope(_head_norm(acc[:, a_sl], extra[0][:, t_sl]), cos, sin)
        outs[0][:, t_sl] = y.astype(BF16)
        if block_mean:
            for blk in range(acc.shape[0] // MOBA_BLOCK):
                rows = y[blk * MOBA_BLOCK:(blk + 1) * MOBA_BLOCK]
                outs[1][0, blk:blk + 1, t_sl] = jnp.mean(rows, axis=0, keepdims=True)


def _compress_body(a_ref, pe_ref, w1_ref, w2_ref, g_ref, o_ref, *, norm):
    half = CMP_STRIDE * HEAD_DIM
    a = a_ref[0]
    n_chunk = a.shape[0]
    x1 = (a + pe_ref[:, :half]).astype(BF16)
    x2 = (a + pe_ref[:, half:]).astype(BF16)
    p1 = jnp.dot(x1, w1_ref[:half, :], preferred_element_type=F32)
    p2 = jnp.dot(x2, w1_ref[half:, :], preferred_element_type=F32)
    h = _gelu(p1 + pltpu.roll(p2, n_chunk - 1, 0))
    o = jnp.dot(h.astype(BF16), w2_ref[...], preferred_element_type=F32)
    if norm:
        o = _head_norm(o, g_ref[...])
    o_ref[0] = o.astype(o_ref.dtype)


def _compress(a, pe, w1, w2, gain, norm):
    g, n_chunk, width = a.shape
    hidden = w1.shape[1]
    return pl.pallas_call(
        functools.partial(_compress_body, norm=norm),
        grid=(g,),
        in_specs=[pl.BlockSpec((1, n_chunk, width), lambda i: (i, 0, 0)),
                  pl.BlockSpec((1, 2 * width), lambda i: (0, 0)),
                  pl.BlockSpec((2 * width, hidden), lambda i: (0, 0)),
                  pl.BlockSpec((hidden, HEAD_DIM), lambda i: (0, 0)),
                  pl.BlockSpec((1, HEAD_DIM), lambda i: (0, 0))],
        out_specs=pl.BlockSpec((1, n_chunk, HEAD_DIM), lambda i: (i, 0, 0)),
        out_shape=jax.ShapeDtypeStruct((g, n_chunk, HEAD_DIM), BF16),
        compiler_params=_cparams(("parallel",)),
        name="nsa_compress",
    )(a, pe.reshape(1, 2 * width), w1.astype(BF16), w2.astype(BF16), gain.reshape(1, HEAD_DIM))


def _topk_mask(score, index_f, k, axis):
    n = score.shape[axis]
    sel = jnp.zeros(score.shape, dtype=jnp.bool_)
    for _ in range(k):
        m = jnp.max(score, axis=axis, keepdims=True)
        first = jnp.min(jnp.where(score == m, index_f, float(n)), axis=axis, keepdims=True)
        hit = index_f == first
        sel = jnp.logical_or(sel, hit)
        score = jnp.where(hit, BELOW_NEG, score)
    return sel


def _nsa_cmp_body(q_ref, kt_ref, v_ref, ovt_ref, o_ref, sbt_ref, *, tq, n_slc, cw):
    qi = pl.program_id(1)
    q0 = qi * tq
    n_pad = kt_ref.shape[-1]
    q = jnp.concatenate([q_ref[:, r * HEAD_DIM:(r + 1) * HEAD_DIM] for r in range(NSA_REP)], axis=0)
    t1 = q0 + lax.broadcasted_iota(jnp.int32, (tq, 1), 0)
    t = jnp.concatenate([t1] * NSA_REP, axis=0)

    def attend(width):
        s = jnp.dot(q, kt_ref[0, :, :width], preferred_element_type=F32)
        cmp_end = lax.broadcasted_iota(jnp.int32, (1, width), 1) * CMP_STRIDE + (CMP_LEN - 1)
        s = jnp.where(cmp_end <= t, s, NEG)
        m = jnp.max(s, axis=-1, keepdims=True)
        e = jnp.exp2(s - m)
        inv = jnp.where(m > 0.5 * NEG, 1.0 / jnp.sum(e, axis=-1, keepdims=True), 0.0)
        p = e * inv
        o = jnp.dot(p.astype(BF16), v_ref[0, :width, :], preferred_element_type=F32)
        for r in range(NSA_REP):
            o_ref[:, r * HEAD_DIM:(r + 1) * HEAD_DIM] = o[r * tq:(r + 1) * tq]
        ps = p[0:tq]
        for r in range(1, NSA_REP):
            ps = ps + p[r * tq:(r + 1) * tq]
        ps_hi = ps.astype(BF16)
        ps_lo = (ps - ps_hi.astype(F32)).astype(BF16)
        nt = (((1,), (1,)), ((), ()))
        n_rows = width * CMP_STRIDE // SLC_LEN
        ovt = ovt_ref[:n_rows, :width]
        imp = (lax.dot_general(ovt, ps_hi, nt, preferred_element_type=F32)
               + lax.dot_general(ovt, ps_lo, nt, preferred_element_type=F32))
        j = lax.broadcasted_iota(jnp.int32, imp.shape, 0)
        cur = (q0 + lax.broadcasted_iota(jnp.int32, (1, tq), 1)) // SLC_LEN
        forced = (j == 0) | (j == cur) | (j == cur - 1)
        allowed = j <= cur
        score = jnp.where(allowed & jnp.logical_not(forced), imp, NEG)
        score = jnp.where(j < n_slc, score, BELOW_NEG)
        sel = _topk_mask(score, j.astype(F32), min(SLC_TOPK, n_slc) - 3, 0)
        sbt_ref[0, :n_rows, :] = jnp.where((sel | forced) & allowed, 0.0, NEG).astype(BF16)
        if n_rows < sbt_ref.shape[1]:
            sbt_ref[0, n_rows:, :] = jnp.full((sbt_ref.shape[1] - n_rows, tq), NEG, BF16)

    n_chunks = ((q0 + tq) // CMP_STRIDE - 1 + cw - 1) // cw
    for v in range(1, n_pad // cw + 1):
        pl.when(n_chunks == v)(functools.partial(attend, v * cw))


def _nsa_cmp(q_c, kc_t, vc, overlap_t, n_slc, tq=256):
    s_len = q_c.shape[0]
    tq = min(tq, s_len)
    g, _, n_pad = kc_t.shape
    n_slc_pad = overlap_t.shape[0]
    gw = NSA_REP * HEAD_DIM
    cw = min(256, n_pad)
    assert n_pad % cw == 0, (n_pad, cw)
    return pl.pallas_call(
        functools.partial(_nsa_cmp_body, tq=tq, n_slc=n_slc, cw=cw),
        grid=(g, s_len // tq),
        in_specs=[pl.BlockSpec((tq, gw), lambda gi, qi: (qi, gi)),
                  pl.BlockSpec((1, HEAD_DIM, n_pad), lambda gi, qi: (gi, 0, 0)),
                  pl.BlockSpec((1, n_pad, HEAD_DIM), lambda gi, qi: (gi, 0, 0)),
                  pl.BlockSpec((n_slc_pad, n_pad), lambda gi, qi: (0, 0))],
        out_specs=[pl.BlockSpec((tq, gw), lambda gi, qi: (qi, gi)),
                   pl.BlockSpec((1, n_slc_pad, tq), lambda gi, qi: (gi, 0, qi))],
        out_shape=[jax.ShapeDtypeStruct((s_len, g * gw), F32),
                   jax.ShapeDtypeStruct((g, n_slc_pad, s_len), BF16)],
        compiler_params=_cparams(("parallel", "parallel")),
        name="nsa_cmp_select",
    )(q_c, kc_t, vc, overlap_t)


def _flash_t_body(qt_ref, bt_ref, k_ref, oh_ref, vt_ref, o_ref, qa_s, m_s, l_s, acc_s, s_s, *,
                  rep, tq, tk, keys_per_var, n_var, n_split):
    width = rep * tq
    qi = pl.program_id(1)
    q0 = qi * tq
    for var in range(n_var):
        bt = bt_ref[0, var * LANES:(var + 1) * LANES, :]
        qa_s[var, :HEAD_DIM, :] = jnp.concatenate(
            [qt_ref[r * HEAD_DIM:(r + 1) * HEAD_DIM, :] for r in range(rep)], axis=1)
        qa_s[var, HEAD_DIM:, :] = jnp.concatenate([bt] * rep, axis=1)
    m_s[...] = jnp.full(m_s.shape, NEG, F32)
    l_s[...] = jnp.zeros(l_s.shape, F32)
    acc_s[...] = jnp.zeros(acc_s.shape, F32)
    t1 = q0 + lax.broadcasted_iota(jnp.int32, (1, tq), 1)
    t = jnp.concatenate([t1] * rep, axis=1)
    cw = width // n_split

    def key_tile(kt):
        start = pl.multiple_of(kt * tk, tk)
        return jnp.concatenate([k_ref[pl.ds(start, tk), :], oh_ref[pl.ds(start, tk), :]], axis=1)

    def scores(k_tile, kt, c):
        var = (kt * tk) // keys_per_var if n_var > 1 else 0
        return jnp.dot(k_tile, qa_s[var, :, c * cw:(c + 1) * cw], preferred_element_type=F32)

    def consume(kt, c, s, causal):
        cs = slice(c * cw, (c + 1) * cw)
        if causal:
            pos = kt * tk + lax.broadcasted_iota(jnp.int32, (tk, 1), 0)
            s = jnp.where(pos <= t[:, cs], s, NEG)
        m_old = m_s[:, cs]
        m_new = jnp.maximum(m_old, jnp.max(s, axis=0, keepdims=True))
        alpha = jnp.exp2(m_old - m_new)
        p = jnp.exp2(s - m_new)
        l_s[:, cs] = alpha * l_s[:, cs] + jnp.sum(p, axis=0, keepdims=True)
        m_s[:, cs] = m_new
        return alpha, p.astype(BF16)

    def step(kt, causal, prefetch, rd):
        v_t = vt_ref[0, kt]
        if prefetch:
            k_next = key_tile(kt + 1)
            for c in range(n_split):
                s_s[1 - rd, :, c * cw:(c + 1) * cw] = scores(k_next, kt + 1, c)
        for c in range(n_split):
            cs = slice(c * cw, (c + 1) * cw)
            alpha, p = consume(kt, c, s_s[rd, :, cs], causal)
            acc_s[:, cs] = alpha * acc_s[:, cs] + jnp.dot(v_t, p, preferred_element_type=F32)

    n_full = q0 // tk
    n_diag = max(1, tq // tk)
    k_first = key_tile(0)
    for c in range(n_split):
        s_s[0, :, c * cw:(c + 1) * cw] = scores(k_first, 0, c)

    def two_steps(j, carry):
        step(2 * j, False, True, 0)
        step(2 * j + 1, False, True, 1)
        return carry

    def diagonal(rd):
        for d in range(n_diag):
            step(n_full + d, True, d < n_diag - 1, (rd + d) % 2)

    lax.fori_loop(0, n_full // 2, two_steps, 0)

    @pl.when(n_full % 2 == 1)
    def _():
        step(n_full - 1, False, True, 0)
        diagonal(1)

    @pl.when(n_full % 2 == 0)
    def _():
        diagonal(0)

    o_t = acc_s[...] / l_s[...]
    for r in range(rep):
        o_ref[:, r * HEAD_DIM:(r + 1) * HEAD_DIM] = o_t[:, r * tq:(r + 1) * tq].T.astype(o_ref.dtype)


def _flash_t(q_t, bias_t, k, block_onehot, v_t, *, rep, tq, keys_per_var, out_dtype, name, n_split):
    s_len = q_t.shape[1]
    g, n_kt, _, tk = v_t.shape
    gw = rep * HEAD_DIM
    width = rep * tq
    n_var = bias_t.shape[1] // LANES
    assert (tk % tq == 0 or tq % tk == 0) and n_split % 2 == 0 and width % (n_split * LANES) == 0, \
        (tq, tk, width, n_split)
    return pl.pallas_call(
        functools.partial(_flash_t_body, rep=rep, tq=tq, tk=tk, keys_per_var=keys_per_var,
                          n_var=n_var, n_split=n_split),
        grid=(g, s_len // tq),
        in_specs=[pl.BlockSpec((gw, tq), lambda gi, qi: (gi, qi)),
                  pl.BlockSpec((1, n_var * LANES, tq), lambda gi, qi: (gi, 0, qi)),
                  pl.BlockSpec((s_len, HEAD_DIM), lambda gi, qi: (0, gi)),
                  pl.BlockSpec((s_len, LANES), lambda gi, qi: (0, 0)),
                  pl.BlockSpec((1, n_kt, HEAD_DIM, tk), lambda gi, qi: (gi, 0, 0, 0))],
        out_specs=pl.BlockSpec((tq, gw), lambda gi, qi: (qi, gi)),
        out_shape=jax.ShapeDtypeStruct((s_len, g * gw), out_dtype),
        scratch_shapes=[pltpu.VMEM((n_var, 2 * HEAD_DIM, width), BF16),
                        pltpu.VMEM((1, width), F32),
                        pltpu.VMEM((1, width), F32),
                        pltpu.VMEM((HEAD_DIM, width), F32),
                        pltpu.VMEM((2, tk, width), F32)],
        compiler_params=_cparams(("parallel", "arbitrary")),
        name=name,
    )(q_t, bias_t, k, block_onehot, v_t)


def _moba_select_body(q_ref, km_ref, bt_ref, *, tq, n_blk):
    q0 = pl.program_id(1) * tq
    nt = (((1,), (1,)), ((), ()))
    sg = lax.dot_general(km_ref[0], q_ref[...], nt, preferred_element_type=F32)
    j = lax.broadcasted_iota(jnp.int32, sg.shape, 0)
    cur = (q0 + lax.broadcasted_iota(jnp.int32, (1, tq), 1)) // MOBA_BLOCK
    past = j < cur
    score = jnp.where(j < n_blk, jnp.where(past, sg, NEG), BELOW_NEG)
    sel = _topk_mask(score, j.astype(F32), min(MOBA_TOPK, n_blk), 0)
    bt_ref[0] = jnp.where((sel & past) | (j == cur), 0.0, NEG).astype(BF16)


def _moba_select(q, k_mean, n_blk, tq=512):
    s_len = q.shape[0]
    tq = min(tq, s_len)
    n_heads = k_mean.shape[0]
    return pl.pallas_call(
        functools.partial(_moba_select_body, tq=tq, n_blk=n_blk),
        grid=(n_heads, s_len // tq),
        in_specs=[pl.BlockSpec((tq, HEAD_DIM), lambda hi, qi: (qi, hi)),
                  pl.BlockSpec((1, LANES, HEAD_DIM), lambda hi, qi: (hi, 0, 0))],
        out_specs=pl.BlockSpec((1, LANES, tq), lambda hi, qi: (hi, 0, qi)),
        out_shape=jax.ShapeDtypeStruct((n_heads, LANES, s_len), BF16),
        compiler_params=_cparams(("parallel", "parallel")),
        name="moba_select",
    )(q, k_mean)


def _window_body(q_ref, kt_ref, v_ref, oc_ref, os_ref, g_ref, o_ref, *, tq):
    qi = pl.program_id(1)
    q0 = qi * tq
    n_past = WINDOW // tq
    q = jnp.concatenate([q_ref[:, r * HEAD_DIM:(r + 1) * HEAD_DIM] for r in range(NSA_REP)], axis=0)
    t1 = q0 + lax.broadcasted_iota(jnp.int32, (tq, 1), 0)
    t = jnp.concatenate([t1] * NSA_REP, axis=0)
    lane = lax.broadcasted_iota(jnp.int32, (1, tq), 1)
    ones = jnp.ones((tq, HEAD_DIM), BF16)
    scores, tiles = [], []
    for i in range(n_past + 1):
        raw = qi - n_past + i
        idx = jnp.maximum(raw, 0)
        tiles.append(idx)
        s = jnp.dot(q, kt_ref[0, idx], preferred_element_type=F32)
        pos = raw * tq + lane
        if i == 0:
            s = jnp.where(pos > t - WINDOW, s, NEG)
        if i == n_past:
            s = jnp.where(pos <= t, s, NEG)
        else:
            s = s + jnp.where(raw >= 0, 0.0, NEG)
        scores.append(s)
    m = jnp.max(scores[0], axis=-1, keepdims=True)
    for s in scores[1:]:
        m = jnp.maximum(m, jnp.max(s, axis=-1, keepdims=True))
    acc = jnp.zeros((NSA_REP * tq, 2 * HEAD_DIM), F32)
    for i, s in enumerate(scores):
        start = pl.multiple_of(tiles[i] * tq, tq)
        v_aug = jnp.concatenate([v_ref[pl.ds(start, tq), :], ones], axis=1)
        acc = acc + jnp.dot(jnp.exp2(s - m).astype(BF16), v_aug, preferred_element_type=F32)
    o_win = acc[:, :HEAD_DIM] / acc[:, HEAD_DIM:]
    gate = g_ref[0]
    for r in range(NSA_REP):
        sl = slice(r * HEAD_DIM, (r + 1) * HEAD_DIM)
        o = (gate[:, r:r + 1] * oc_ref[:, sl]
             + gate[:, NSA_REP + r:NSA_REP + r + 1] * os_ref[:, sl]
             + gate[:, 2 * NSA_REP + r:2 * NSA_REP + r + 1] * o_win[r * tq:(r + 1) * tq])
        o_ref[:, sl] = o.astype(o_ref.dtype)


def _window_combine(q_r, kw_t, v, v_block0, o_cmp, o_slc, gates):
    s_len = q_r.shape[0]
    g, n_kt, _, tq = kw_t.shape
    assert WINDOW % tq == 0, tq
    gw = NSA_REP * HEAD_DIM
    tile = pl.BlockSpec((tq, gw), lambda gi, qi: (qi, gi))
    return pl.pallas_call(
        functools.partial(_window_body, tq=tq),
        grid=(g, s_len // tq),
        in_specs=[tile,
                  pl.BlockSpec((1, n_kt, HEAD_DIM, tq), lambda gi, qi: (gi, 0, 0, 0)),
                  pl.BlockSpec((s_len, HEAD_DIM), lambda gi, qi: (0, v_block0 + gi)),
                  tile, tile,
                  pl.BlockSpec((1, tq, LANES), lambda gi, qi: (gi, qi, 0))],
        out_specs=tile,
        out_shape=jax.ShapeDtypeStruct((s_len, g * gw), BF16),
        compiler_params=_cparams(("parallel", "parallel")),
        name="nsa_window_combine",
    )(q_r, kw_t, v, o_cmp, o_slc, gates)


def _sgu_body(zu_ref, zv_ref, gain_ref, w_ref, bt_ref, o_ref, *, tm):
    gd = zu_ref.shape[1] // SGU_GROUPS
    row = lax.broadcasted_iota(jnp.int32, (SGU_CHUNK, SGU_CHUNK), 0)
    col = lax.broadcasted_iota(jnp.int32, (SGU_CHUNK, SGU_CHUNK), 1)
    bt = bt_ref[...]
    for g in range(SGU_GROUPS):
        sl = slice(g * gd, (g + 1) * gd)
        v = _gelu(zv_ref[:, sl])
        v = (v * lax.rsqrt(jnp.mean(v * v, axis=-1, keepdims=True) + NORM_EPS) * gain_ref[:, sl]).astype(BF16)
        w = jnp.where(col <= row, w_ref[g], 0.0).astype(BF16)
        for c in range(tm // SGU_CHUNK):
            rs = slice(c * SGU_CHUNK, (c + 1) * SGU_CHUNK)
            mixed = jnp.dot(w, v[rs], preferred_element_type=F32) + bt[:, g:g + 1]
            o_ref[rs, sl] = (_gelu(zu_ref[rs, sl]) * mixed).astype(o_ref.dtype)


def _sgu(z_uv, gain, w_s, b_s, tm=512):
    s_len = z_uv.shape[0]
    width = z_uv.shape[1] // 2
    tm = min(tm, s_len)
    return pl.pallas_call(
        functools.partial(_sgu_body, tm=tm),
        grid=(s_len // tm,),
        in_specs=[pl.BlockSpec((tm, width), lambda i: (i, 0)),
                  pl.BlockSpec((tm, width), lambda i: (i, 1)),
                  pl.BlockSpec((1, width), lambda i: (0, 0)),
                  pl.BlockSpec((SGU_GROUPS, SGU_CHUNK, SGU_CHUNK), lambda i: (0, 0, 0)),
                  pl.BlockSpec((SGU_CHUNK, SGU_GROUPS), lambda i: (0, 0))],
        out_specs=pl.BlockSpec((tm, width), lambda i: (i, 0)),
        out_shape=jax.ShapeDtypeStruct((s_len, width), BF16),
        compiler_params=_cparams(("parallel",)),
        name="sgu",
    )(z_uv, z_uv, gain.reshape(1, width), w_s, b_s.T)


def _merge_body(oa_ref, ob_ref, oc_ref, pa_ref, pb_ref, pc_ref, ga_ref, gb_ref, gc_ref, y_ref):
    y = ga_ref[...] * jnp.dot(oa_ref[...], pa_ref[...], preferred_element_type=F32)
    y = y + gb_ref[...] * jnp.dot(ob_ref[...], pb_ref[...], preferred_element_type=F32)
    y = y + gc_ref[...] * jnp.dot(oc_ref[...], pc_ref[...], preferred_element_type=F32)
    y_ref[...] = y.astype(y_ref.dtype)


def _merge(o_a, o_b, o_c, p_a, p_b, p_c, layer, gm, tm=1024, tn=512):
    s_len = o_a.shape[0]
    d = p_a.shape[2]
    tm, tn = min(tm, s_len), _tile(d, tn)
    nj = d // tn

    def rows(w):
        return pl.BlockSpec((tm, w), lambda i, j: (i, 0))

    def cols(kdim):
        return pl.BlockSpec((None, kdim, tn), lambda i, j: (layer, 0, j))

    def gate(off):
        return pl.BlockSpec((tm, tn), lambda i, j: (i, off * nj + j))

    return pl.pallas_call(
        _merge_body,
        grid=(s_len // tm, nj),
        in_specs=[rows(o_a.shape[1]), rows(o_b.shape[1]), rows(o_c.shape[1]),
                  cols(p_a.shape[1]), cols(p_b.shape[1]), cols(p_c.shape[1]),
                  gate(0), gate(1), gate(2)],
        out_specs=pl.BlockSpec((tm, tn), lambda i, j: (i, j)),
        out_shape=jax.ShapeDtypeStruct((s_len, d), BF16),
        compiler_params=_cparams(("parallel", "parallel")),
        name="gated_merge",
    )(o_a, o_b, o_c, p_a, p_b, p_c, gm, gm, gm)


def _block_onehot(s_len, block):
    key = np.arange(s_len)
    return jnp.asarray(((key // block) % LANES)[:, None] == np.arange(LANES)[None, :], dtype=BF16)


def _tiles_t(v, n_heads, tk):
    s_len = v.shape[0]
    return v.reshape(s_len // tk, tk, n_heads, HEAD_DIM).transpose(2, 0, 3, 1)


def _overlap_t(n_pad, n_slc_pad):
    i = np.arange(n_pad)[None, :]
    j = np.arange(n_slc_pad)[:, None]
    ov = (i * CMP_STRIDE <= j * SLC_LEN + SLC_LEN - 1) & (i * CMP_STRIDE + CMP_LEN - 1 >= j * SLC_LEN)
    return jnp.asarray(ov, dtype=BF16)


def _layer(x, cos, sin, p, big, layer):
    s_len, d_model = x.shape
    scale = HEAD_DIM ** -0.5 * math.log2(math.e)
    w_in = p["w_in"]
    nsa_w = big["proj_a"].shape[1]
    sgu_w = big["proj_b"].shape[1]
    moba_w = big["proj_c"].shape[1]
    n_heads = nsa_w // HEAD_DIM
    n_groups = n_heads // NSA_REP
    kv_w = n_groups * HEAD_DIM
    moba_heads = moba_w // HEAD_DIM
    sizes = (nsa_w, kv_w, kv_w, kv_w, kv_w, kv_w, kv_w, 3 * n_heads, sgu_w, sgu_w,
             moba_w, moba_w, moba_w, d_model, d_model, d_model)
    offs = np.concatenate([[0], np.cumsum(sizes)])

    def seg(a, b):
        return w_in[:, offs[a]:offs[b]].astype(BF16)

    def tile_gain(gain, reps):
        return jnp.tile(gain, reps).reshape(1, reps * HEAD_DIM)

    h = _rmsnorm(x, p["norm_mix"])
    rope_extras = (cos, sin)
    rope_specs = (_spec_rope, _spec_rope)

    q_c, q_r = _matmul(
        h, seg(0, 1), functools.partial(_ep_q, scale=scale),
        [jax.ShapeDtypeStruct((s_len, nsa_w), BF16)] * 2, [_spec_tile, _spec_tile],
        extras=(tile_gain(p["nsa_q_norm"], n_heads),) + rope_extras,
        extra_specs=(_spec_col,) + rope_specs, ts=HEAVY_EPILOGUE_TS, name="proj_nsa_q")
    (kcvc,) = _matmul(h, seg(1, 3), _ep_cast, [jax.ShapeDtypeStruct((s_len, 2 * kv_w), F32)],
                      [_spec_tile], name="proj_nsa_cmp_kv")
    (kskw,) = _matmul(
        h, jnp.concatenate([seg(3, 4), seg(5, 6)], axis=1), functools.partial(_ep_krot, block_mean=False),
        [jax.ShapeDtypeStruct((s_len, 2 * kv_w), BF16)], [_spec_tile],
        extras=(jnp.concatenate([tile_gain(p["nsa_ks_norm"], n_groups),
                                 tile_gain(p["nsa_kw_norm"], n_groups)], axis=1),) + rope_extras,
        extra_specs=(_spec_col,) + rope_specs, ts=HEAVY_EPILOGUE_TS, name="proj_nsa_k")
    (vsvw,) = _matmul(h, jnp.concatenate([seg(4, 5), seg(6, 7)], axis=1), _ep_cast,
                      [jax.ShapeDtypeStruct((s_len, 2 * kv_w), BF16)], [_spec_tile], name="proj_nsa_v")
    n_gate = 3 * n_heads
    w_gate = jnp.pad(seg(7, 8), ((0, 0), (0, LANES - n_gate)))
    b_gate = jnp.pad(p["nsa_gate_b"], (0, LANES - n_gate)).reshape(1, LANES)
    (gates,) = _matmul(h, w_gate, _ep_sigmoid_bias, [jax.ShapeDtypeStruct((s_len, LANES), F32)],
                       [_spec_tile], extras=(b_gate,), extra_specs=(_spec_col,), name="proj_nsa_gates")

    n_chunk = s_len // CMP_STRIDE
    chunks = kcvc.reshape(n_chunk, CMP_STRIDE, 2, n_groups, HEAD_DIM).transpose(2, 3, 0, 1, 4)
    chunks = chunks.reshape(2, n_groups, n_chunk, CMP_STRIDE * HEAD_DIM)
    kc = _compress(chunks[0], p["phi_pe_k"], p["phi_w1_k"], p["phi_w2_k"], p["nsa_kc_norm"], True)
    vc = _compress(chunks[1], p["phi_pe_v"], p["phi_w1_v"], p["phi_w2_v"], p["nsa_kc_norm"], False)

    n_slc = s_len // SLC_LEN
    n_slc_pad = -(-n_slc // LANES) * LANES
    o_cmp, sel_bias_t = _nsa_cmp(q_c, jnp.swapaxes(kc, 1, 2), vc, _overlap_t(n_chunk, n_slc_pad), n_slc)
    tk = min(512, s_len)
    o_slc = _flash_t(q_r.T, sel_bias_t, kskw, _block_onehot(s_len, SLC_LEN),
                     _tiles_t(vsvw[:, :kv_w], n_groups, tk), rep=NSA_REP, tq=min(512, s_len),
                     keys_per_var=LANES * SLC_LEN, out_dtype=F32, name="nsa_selected", n_split=8)
    kw_t = _tiles_t(kskw[:, kv_w:], n_groups, min(256, s_len))
    group_gates = gates[:, :n_gate].reshape(s_len, 3, n_groups, NSA_REP).transpose(2, 0, 1, 3)
    group_gates = jnp.pad(group_gates.reshape(n_groups, s_len, 3 * NSA_REP),
                          ((0, 0), (0, 0), (0, LANES - 3 * NSA_REP)))
    o_a = _window_combine(q_r, kw_t, vsvw, n_groups, o_cmp, o_slc, group_gates)

    (z_uv,) = _matmul(h, seg(8, 10), _ep_cast, [jax.ShapeDtypeStruct((s_len, 2 * sgu_w), F32)],
                      [_spec_tile], name="proj_sgu")
    o_b = _sgu(z_uv, p["sgu_norm"], p["sgu_w"], p["sgu_b"])

    (mq,) = _matmul(
        h, seg(10, 11), functools.partial(_ep_qrot, scale=scale),
        [jax.ShapeDtypeStruct((s_len, moba_w), BF16)], [_spec_tile],
        extras=(tile_gain(p["moba_q_norm"], moba_heads),) + rope_extras,
        extra_specs=(_spec_col,) + rope_specs, ts=HEAVY_EPILOGUE_TS, name="proj_moba_q")
    tm_k = min(1024, s_len)
    mk, mk_mean = _matmul(
        h, seg(11, 12), functools.partial(_ep_krot, block_mean=True),
        [jax.ShapeDtypeStruct((s_len, moba_w), BF16),
         jax.ShapeDtypeStruct((s_len // tm_k, tm_k // MOBA_BLOCK, moba_w), F32)],
        [_spec_tile, _spec_blockmean],
        extras=(tile_gain(p["moba_k_norm"], moba_heads),) + rope_extras,
        extra_specs=(_spec_col,) + rope_specs, tm=tm_k, ts=HEAVY_EPILOGUE_TS, name="proj_moba_k")
    (mv,) = _matmul(h, seg(12, 13), _ep_cast, [jax.ShapeDtypeStruct((s_len, moba_w), BF16)],
                    [_spec_tile], name="proj_moba_v")
    n_blk = s_len // MOBA_BLOCK
    assert n_blk <= LANES, n_blk
    k_mean = mk_mean.reshape(n_blk, moba_heads, HEAD_DIM).transpose(1, 0, 2)
    k_mean = jnp.pad(k_mean, ((0, 0), (0, LANES - n_blk), (0, 0))).astype(BF16)
    o_c = _flash_t(mq.T, _moba_select(mq, k_mean, n_blk), mk, _block_onehot(s_len, MOBA_BLOCK),
                   _tiles_t(mv, moba_heads, min(1024, s_len)), rep=1, tq=min(1024, s_len),
                   keys_per_var=LANES * MOBA_BLOCK, out_dtype=BF16, name="moba", n_split=4)

    (gm,) = _matmul(h, seg(13, 16), _ep_sigmoid, [jax.ShapeDtypeStruct((s_len, 3 * d_model), F32)],
                    [_spec_tile], name="proj_merge_gates")
    y = _merge(o_a, o_b, o_c, big["proj_a"], big["proj_b"], big["proj_c"], layer, gm)
    (x,) = _matmul(y, big["w_out"], _ep_residual, [jax.ShapeDtypeStruct((s_len, d_model), F32)],
                   [_spec_tile], extras=(x,), extra_specs=(_spec_tile,), b_layer=layer, name="out_proj")

    h2 = _rmsnorm(x, p["norm_mlp"])
    (hid,) = _matmul(h2, big["mlp_w1"], _ep_relu2,
                     [jax.ShapeDtypeStruct((s_len, big["mlp_w1"].shape[2]), BF16)], [_spec_tile],
                     b_layer=layer, name="mlp_up")
    (x,) = _matmul(hid, big["mlp_w2"], _ep_residual, [jax.ShapeDtypeStruct((s_len, d_model), F32)],
                   [_spec_tile], extras=(x,), extra_specs=(_spec_tile,), b_layer=layer, name="mlp_down")
    return x


_LAYER_PARAMS = ("norm_mix", "norm_mlp", "w_in", "nsa_gate_b", "nsa_q_norm", "nsa_kc_norm", "nsa_ks_norm",
                 "nsa_kw_norm", "phi_pe_k", "phi_w1_k", "phi_w2_k", "phi_pe_v", "phi_w1_v", "phi_w2_v",
                 "sgu_norm", "sgu_w", "sgu_b", "moba_q_norm", "moba_k_norm", "proj_a", "proj_b", "proj_c",
                 "w_out", "mlp_w1", "mlp_w2")
_BIG_PARAMS = ("proj_a", "proj_b", "proj_c", "w_out", "mlp_w1", "mlp_w2")


def kernel(x, positions, norm_mix, norm_mlp, w_in, nsa_gate_b, nsa_q_norm, nsa_kc_norm, nsa_ks_norm, nsa_kw_norm, phi_pe_k, phi_w1_k, phi_w2_k, phi_pe_v, phi_w1_v, phi_w2_v, sgu_norm, sgu_w, sgu_b, moba_q_norm, moba_k_norm, proj_a, proj_b, proj_c, w_out, mlp_w1, mlp_w2):
    stacked = dict(zip(_LAYER_PARAMS, (norm_mix, norm_mlp, w_in, nsa_gate_b, nsa_q_norm, nsa_kc_norm,
                                       nsa_ks_norm, nsa_kw_norm, phi_pe_k, phi_w1_k, phi_w2_k, phi_pe_v,
                                       phi_w1_v, phi_w2_v, sgu_norm, sgu_w, sgu_b, moba_q_norm, moba_k_norm,
                                       proj_a, proj_b, proj_c, w_out, mlp_w1, mlp_w2)))
    depth = w_in.shape[0]
    big = {k: stacked.pop(k).astype(BF16) for k in _BIG_PARAMS}
    inv = ROPE_THETA ** (-jnp.arange(0, HEAD_DIM, 2, dtype=F32) / HEAD_DIM)
    batch, s_len, d_model = x.shape
    outs = []
    for b in range(batch):
        ang = positions[b].astype(F32)[:, None] * inv
        cos = jnp.concatenate([jnp.cos(ang), jnp.cos(ang)], axis=-1)
        sin = jnp.concatenate([-jnp.sin(ang), jnp.sin(ang)], axis=-1)
        xb = x.reshape(s_len, d_model) if batch == 1 else x[b]
        for l in range(depth):
            xb = _layer(xb, cos, sin, {k: v[l] for k, v in stacked.items()}, big, l)
        outs.append(xb)
    return outs[0].reshape(1, s_len, d_model) if batch == 1 else jnp.stack(outs)
```

```python
import functools
import math

import jax
import jax.numpy as jnp
import numpy as np
from jax import lax
from jax.experimental import pallas as pl
from jax.experimental.pallas import tpu as pltpu

F32 = jnp.float32
BF16 = jnp.bfloat16

HEAD_DIM = 128
LANES = 128
ROPE_THETA = 10000.0
NORM_EPS = 1e-6
NEG = -1e30
BELOW_NEG = -3e38

NSA_REP = 4
CMP_LEN = 32
CMP_STRIDE = 16
SLC_LEN = 64
SLC_TOPK = 16
WINDOW = 512
SGU_GROUPS = 8
SGU_CHUNK = 128
MOBA_BLOCK = 256
MOBA_TOPK = 3

LIGHT_EPILOGUE_TS = 512
HEAVY_EPILOGUE_TS = 1024
FULL_K = 4096

MIB = 1024 * 1024
VMEM_LIMIT = 52 * MIB


def _cparams(sem, vmem=VMEM_LIMIT):
    return pltpu.CompilerParams(dimension_semantics=sem, vmem_limit_bytes=vmem)


def _tile(n, pref):
    if n <= pref:
        return n
    t = (pref // LANES) * LANES
    while t >= LANES:
        if n % t == 0:
            return t
        t -= LANES
    raise ValueError(f"no 128-multiple tile divides {n}")


def _gelu(x):
    c = math.sqrt(2.0 / math.pi)
    return 0.5 * x * (1.0 + jnp.tanh(c * (x + 0.044715 * (x * x * x))))


def _sigmoid(x):
    return 1.0 / (1.0 + jnp.exp(-x))


def _head_norm(x, gain):
    return x * lax.rsqrt(jnp.mean(x * x, axis=-1, keepdims=True) + NORM_EPS) * gain


def _rope(x, cos, sin_signed):
    return x * cos + pltpu.roll(x, HEAD_DIM // 2, 1) * sin_signed


def _rmsnorm_body(x_ref, g_ref, o_ref):
    x = x_ref[...]
    y = x * lax.rsqrt(jnp.mean(x * x, axis=-1, keepdims=True) + NORM_EPS)
    o_ref[...] = (y * g_ref[...]).astype(o_ref.dtype)


def _rmsnorm(x, gain, tm=256):
    m, d = x.shape
    tm = min(tm, m)
    return pl.pallas_call(
        _rmsnorm_body,
        grid=(m // tm,),
        in_specs=[pl.BlockSpec((tm, d), lambda i: (i, 0)),
                  pl.BlockSpec((1, d), lambda i: (0, 0))],
        out_specs=pl.BlockSpec((tm, d), lambda i: (i, 0)),
        out_shape=jax.ShapeDtypeStruct((m, d), BF16),
        compiler_params=_cparams(("parallel",)),
        name="rmsnorm",
    )(x, gain.reshape(1, d))


def _mm_body(*refs, n_extra, n_out, nk, tn, ts, epilogue):
    a_ref, b_ref = refs[0], refs[1]
    extra = refs[2:2 + n_extra]
    outs = refs[2 + n_extra:2 + n_extra + n_out]
    acc_ref = refs[-1] if nk > 1 else None

    def finish():
        for c0 in range(0, tn, ts):
            part = jnp.dot(a_ref[...], b_ref[:, c0:c0 + ts], preferred_element_type=F32)
            if nk > 1:
                part = part + acc_ref[:, c0:c0 + ts]
            epilogue(part, extra, outs, slice(c0, c0 + ts))

    if nk == 1:
        finish()
        return
    k = pl.program_id(2)
    split_finish = ts < tn

    @pl.when(k == 0)
    def _():
        acc_ref[...] = jnp.dot(a_ref[...], b_ref[...], preferred_element_type=F32)

    @pl.when((k > 0) & (k < nk - 1) if split_finish else k > 0)
    def _():
        acc_ref[...] += jnp.dot(a_ref[...], b_ref[...], preferred_element_type=F32)

    @pl.when(k == nk - 1)
    def _():
        if split_finish:
            finish()
        else:
            epilogue(acc_ref[...], extra, outs, slice(0, tn))


def _matmul(a, b, epilogue, out_shapes, out_specs, extras=(), extra_specs=(),
            tm=1024, tn=1024, tk=2048, ts=LIGHT_EPILOGUE_TS, b_layer=0, name="matmul"):
    m, kdim = a.shape
    n = b.shape[-1]
    tm, tn, tk = min(tm, m), _tile(n, tn), _tile(kdim, tk)
    ts = min(ts, tn)
    if b.ndim == 3:
        b_spec = pl.BlockSpec((None, tk, tn), lambda i, j, k: (b_layer, k, j))
    else:
        b_spec = pl.BlockSpec((tk, tn), lambda i, j, k: (k, j))
    nk = kdim // tk
    body = functools.partial(_mm_body, n_extra=len(extras), n_out=len(out_shapes), nk=nk, tn=tn, ts=ts,
                             epilogue=epilogue)
    scratch = [] if nk == 1 else [pltpu.VMEM((tm, tn), F32)]
    return pl.pallas_call(
        body,
        grid=(m // tm, n // tn, nk),
        in_specs=[pl.BlockSpec((tm, tk), lambda i, j, k: (i, k)),
                  b_spec]
                 + [s(tm, tn) for s in extra_specs],
        out_specs=[s(tm, tn) for s in out_specs],
        out_shape=out_shapes,
        scratch_shapes=scratch,
        compiler_params=_cparams(("parallel", "parallel", "arbitrary")),
        name=name,
    )(a, b, *extras)


def _spec_tile(tm, tn):
    return pl.BlockSpec((tm, tn), lambda i, j, k: (i, j))


def _spec_col(tm, tn):
    return pl.BlockSpec((1, tn), lambda i, j, k: (0, j))


def _spec_rope(tm, tn):
    return pl.BlockSpec((tm, HEAD_DIM), lambda i, j, k: (i, 0))


def _spec_blockmean(tm, tn):
    return pl.BlockSpec((1, tm // MOBA_BLOCK, tn), lambda i, j, k: (i, 0, j))


def _ep_cast(acc, extra, outs, cols):
    outs[0][:, cols] = acc.astype(outs[0].dtype)


def _ep_sigmoid_bias(acc, extra, outs, cols):
    outs[0][:, cols] = _sigmoid(acc + extra[0][:, cols])


def _ep_sigmoid(acc, extra, outs, cols):
    outs[0][:, cols] = _sigmoid(acc)


def _ep_relu2(acc, extra, outs, cols):
    r = jnp.maximum(acc, 0.0)
    outs[0][:, cols] = (r * r).astype(outs[0].dtype)


def _ep_residual(acc, extra, outs, cols):
    outs[0][:, cols] = extra[0][:, cols] + acc


def _heads(cols):
    return [(slice(c - cols.start, c - cols.start + HEAD_DIM), slice(c, c + HEAD_DIM))
            for c in range(cols.start, cols.stop, HEAD_DIM)]


def _ep_q(acc, extra, outs, cols, *, scale):
    cos, sin = extra[1][...], extra[2][...]
    for a_sl, t_sl in _heads(cols):
        y = _head_norm(acc[:, a_sl], extra[0][:, t_sl])
        outs[0][:, t_sl] = (y * scale).astype(BF16)
        outs[1][:, t_sl] = (_rope(y, cos, sin) * scale).astype(BF16)


def _ep_qrot(acc, extra, outs, cols, *, scale):
    cos, sin = extra[1][...], extra[2][...]
    for a_sl, t_sl in _heads(cols):
        y = _head_norm(acc[:, a_sl], extra[0][:, t_sl])
        outs[0][:, t_sl] = (_rope(y, cos, sin) * scale).astype(BF16)


def _ep_krot(acc, extra, outs, cols, *, block_mean):
    cos, sin = extra[1][...], extra[2][...]
    for a_sl, t_sl in _heads(cols):
        y = _rope(_head_norm(acc[:, a_sl], extra[0][:, t_sl]), cos, sin)
        outs[0][:, t_sl] = y.astype(BF16)
        if block_mean:
            for blk in range(acc.shape[0] // MOBA_BLOCK):
                rows = y[blk * MOBA_BLOCK:(blk + 1) * MOBA_BLOCK]
                outs[1][0, blk:blk + 1, t_sl] = jnp.mean(rows, axis=0, keepdims=True)


def _compress_body(a_ref, pe_ref, w1_ref, w2_ref, g_ref, o_ref, *, norm):
    half = CMP_STRIDE * HEAD_DIM
    a = a_ref[0]
    n_chunk = a.shape[0]
    x1 = (a + pe_ref[:, :half]).astype(BF16)
    x2 = (a + pe_ref[:, half:]).astype(BF16)
    p1 = jnp.dot(x1, w1_ref[:half, :], preferred_element_type=F32)
    p2 = jnp.dot(x2, w1_ref[half:, :], preferred_element_type=F32)
    h = _gelu(p1 + pltpu.roll(p2, n_chunk - 1, 0))
    o = jnp.dot(h.astype(BF16), w2_ref[...], preferred_element_type=F32)
    if norm:
        o = _head_norm(o, g_ref[...])
    o_ref[0] = o.astype(o_ref.dtype)


def _compress(a, pe, w1, w2, gain, norm):
    g, n_chunk, width = a.shape
    hidden = w1.shape[1]
    return pl.pallas_call(
        functools.partial(_compress_body, norm=norm),
        grid=(g,),
        in_specs=[pl.BlockSpec((1, n_chunk, width), lambda i: (i, 0, 0)),
                  pl.BlockSpec((1, 2 * width), lambda i: (0, 0)),
                  pl.BlockSpec((2 * width, hidden), lambda i: (0, 0)),
                  pl.BlockSpec((hidden, HEAD_DIM), lambda i: (0, 0)),
                  pl.BlockSpec((1, HEAD_DIM), lambda i: (0, 0))],
        out_specs=pl.BlockSpec((1, n_chunk, HEAD_DIM), lambda i: (i, 0, 0)),
        out_shape=jax.ShapeDtypeStruct((g, n_chunk, HEAD_DIM), BF16),
        compiler_params=_cparams(("parallel",)),
        name="nsa_compress",
    )(a, pe.reshape(1, 2 * width), w1.astype(BF16), w2.astype(BF16), gain.reshape(1, HEAD_DIM))


def _topk_mask(score, index_f, k, axis):
    n = score.shape[axis]
    sel = jnp.zeros(score.shape, dtype=jnp.bool_)
    for _ in range(k):
        m = jnp.max(score, axis=axis, keepdims=True)
        first = jnp.min(jnp.where(score == m, index_f, float(n)), axis=axis, keepdims=True)
        hit = index_f == first
        sel = jnp.logical_or(sel, hit)
        score = jnp.where(hit, BELOW_NEG, score)
    return sel


def _nsa_cmp_body(q_ref, kt_ref, v_ref, ovt_ref, o_ref, sbt_ref, *, tq, n_slc, cw):
    qi = pl.program_id(1)
    q0 = qi * tq
    n_pad = kt_ref.shape[-1]
    q = jnp.concatenate([q_ref[:, r * HEAD_DIM:(r + 1) * HEAD_DIM] for r in range(NSA_REP)], axis=0)
    t1 = q0 + lax.broadcasted_iota(jnp.int32, (tq, 1), 0)
    t = jnp.concatenate([t1] * NSA_REP, axis=0)

    def attend(width):
        s = jnp.dot(q, kt_ref[0, :, :width], preferred_element_type=F32)
        cmp_end = lax.broadcasted_iota(jnp.int32, (1, width), 1) * CMP_STRIDE + (CMP_LEN - 1)
        s = jnp.where(cmp_end <= t, s, NEG)
        m = jnp.max(s, axis=-1, keepdims=True)
        e = jnp.exp2(s - m)
        inv = jnp.where(m > 0.5 * NEG, 1.0 / jnp.sum(e, axis=-1, keepdims=True), 0.0)
        p = e * inv
        o = jnp.dot(p.astype(BF16), v_ref[0, :width, :], preferred_element_type=F32)
        for r in range(NSA_REP):
            o_ref[:, r * HEAD_DIM:(r + 1) * HEAD_DIM] = o[r * tq:(r + 1) * tq]
        ps = p[0:tq]
        for r in range(1, NSA_REP):
            ps = ps + p[r * tq:(r + 1) * tq]
        ps_hi = ps.astype(BF16)
        ps_lo = (ps - ps_hi.astype(F32)).astype(BF16)
        nt = (((1,), (1,)), ((), ()))
        n_rows = width * CMP_STRIDE // SLC_LEN
        ovt = ovt_ref[:n_rows, :width]
        imp = (lax.dot_general(ovt, ps_hi, nt, preferred_element_type=F32)
               + lax.dot_general(ovt, ps_lo, nt, preferred_element_type=F32))
        j = lax.broadcasted_iota(jnp.int32, imp.shape, 0)
        cur = (q0 + lax.broadcasted_iota(jnp.int32, (1, tq), 1)) // SLC_LEN
        forced = (j == 0) | (j == cur) | (j == cur - 1)
        allowed = j <= cur
        score = jnp.where(allowed & jnp.logical_not(forced), imp, NEG)
        score = jnp.where(j < n_slc, score, BELOW_NEG)
        sel = _topk_mask(score, j.astype(F32), min(SLC_TOPK, n_slc) - 3, 0)
        sbt_ref[0, :n_rows, :] = jnp.where((sel | forced) & allowed, 0.0, NEG).astype(BF16)
        if n_rows < sbt_ref.shape[1]:
            sbt_ref[0, n_rows:, :] = jnp.full((sbt_ref.shape[1] - n_rows, tq), NEG, BF16)

    n_chunks = ((q0 + tq) // CMP_STRIDE - 1 + cw - 1) // cw
    for v in range(1, n_pad // cw + 1):
        pl.when(n_chunks == v)(functools.partial(attend, v * cw))


def _nsa_cmp(q_c, kc_t, vc, overlap_t, n_slc, tq=256):
    s_len = q_c.shape[0]
    tq = min(tq, s_len)
    g, _, n_pad = kc_t.shape
    n_slc_pad = overlap_t.shape[0]
    gw = NSA_REP * HEAD_DIM
    cw = min(256, n_pad)
    assert n_pad % cw == 0, (n_pad, cw)
    return pl.pallas_call(
        functools.partial(_nsa_cmp_body, tq=tq, n_slc=n_slc, cw=cw),
        grid=(g, s_len // tq),
        in_specs=[pl.BlockSpec((tq, gw), lambda gi, qi: (qi, gi)),
                  pl.BlockSpec((1, HEAD_DIM, n_pad), lambda gi, qi: (gi, 0, 0)),
                  pl.BlockSpec((1, n_pad, HEAD_DIM), lambda gi, qi: (gi, 0, 0)),
                  pl.BlockSpec((n_slc_pad, n_pad), lambda gi, qi: (0, 0))],
        out_specs=[pl.BlockSpec((tq, gw), lambda gi, qi: (qi, gi)),
                   pl.BlockSpec((1, n_slc_pad, tq), lambda gi, qi: (gi, 0, qi))],
        out_shape=[jax.ShapeDtypeStruct((s_len, g * gw), F32),
                   jax.ShapeDtypeStruct((g, n_slc_pad, s_len), BF16)],
        compiler_params=_cparams(("parallel", "parallel")),
        name="nsa_cmp_select",
    )(q_c, kc_t, vc, overlap_t)


def _flash_t_body(qt_ref, bt_ref, k_ref, oh_ref, vt_ref, o_ref, qa_s, m_s, l_s, acc_s, s_s, *,
                  rep, tq, tk, keys_per_var, n_var, n_split):
    width = rep * tq
    qi = pl.program_id(1)
    q0 = qi * tq
    for var in range(n_var):
        bt = bt_ref[0, var * LANES:(var + 1) * LANES, :]
        qa_s[var, :HEAD_DIM, :] = jnp.concatenate(
            [qt_ref[r * HEAD_DIM:(r + 1) * HEAD_DIM, :] for r in range(rep)], axis=1)
        qa_s[var, HEAD_DIM:, :] = jnp.concatenate([bt] * rep, axis=1)
    m_s[...] = jnp.full(m_s.shape, NEG, F32)
    l_s[...] = jnp.zeros(l_s.shape, F32)
    acc_s[...] = jnp.zeros(acc_s.shape, F32)
    t1 = q0 + lax.broadcasted_iota(jnp.int32, (1, tq), 1)
    t = jnp.concatenate([t1] * rep, axis=1)
    cw = width // n_split

    def key_tile(kt):
        start = pl.multiple_of(kt * tk, tk)
        return jnp.concatenate([k_ref[pl.ds(start, tk), :], oh_ref[pl.ds(start, tk), :]], axis=1)

    def scores(k_tile, kt, c):
        var = (kt * tk) // keys_per_var if n_var > 1 else 0
        return jnp.dot(k_tile, qa_s[var, :, c * cw:(c + 1) * cw], preferred_element_type=F32)

    def consume(kt, c, s, causal):
        cs = slice(c * cw, (c + 1) * cw)
        if causal:
            pos = kt * tk + lax.broadcasted_iota(jnp.int32, (tk, 1), 0)
            s = jnp.where(pos <= t[:, cs], s, NEG)
        m_old = m_s[:, cs]
        m_new = jnp.maximum(m_old, jnp.max(s, axis=0, keepdims=True))
        alpha = jnp.exp2(m_old - m_new)
        p = jnp.exp2(s - m_new)
        l_s[:, cs] = alpha * l_s[:, cs] + jnp.sum(p, axis=0, keepdims=True)
        m_s[:, cs] = m_new
        return alpha, p.astype(BF16)

    def step(kt, causal, prefetch, rd):
        v_t = vt_ref[0, kt]
        if prefetch:
            k_next = key_tile(kt + 1)
            for c in range(n_split):
                s_s[1 - rd, :, c * cw:(c + 1) * cw] = scores(k_next, kt + 1, c)
        for c in range(n_split):
            cs = slice(c * cw, (c + 1) * cw)
            alpha, p = consume(kt, c, s_s[rd, :, cs], causal)
            acc_s[:, cs] = alpha * acc_s[:, cs] + jnp.dot(v_t, p, preferred_element_type=F32)

    n_full = q0 // tk
    n_diag = max(1, tq // tk)
    k_first = key_tile(0)
    for c in range(n_split):
        s_s[0, :, c * cw:(c + 1) * cw] = scores(k_first, 0, c)

    def two_steps(j, carry):
        step(2 * j, False, True, 0)
        step(2 * j + 1, False, True, 1)
        return carry

    def diagonal(rd):
        for d in range(n_diag):
            step(n_full + d, True, d < n_diag - 1, (rd + d) % 2)

    lax.fori_loop(0, n_full // 2, two_steps, 0)

    @pl.when(n_full % 2 == 1)
    def _():
        step(n_full - 1, False, True, 0)
        diagonal(1)

    @pl.when(n_full % 2 == 0)
    def _():
        diagonal(0)

    o_t = acc_s[...] / l_s[...]
    for r in range(rep):
        o_ref[:, r * HEAD_DIM:(r + 1) * HEAD_DIM] = o_t[:, r * tq:(r + 1) * tq].T.astype(o_ref.dtype)


def _flash_t(q_t, bias_t, k, block_onehot, v_t, *, rep, tq, keys_per_var, out_dtype, name, n_split):
    s_len = q_t.shape[1]
    g, n_kt, _, tk = v_t.shape
    gw = rep * HEAD_DIM
    width = rep * tq
    n_var = bias_t.shape[1] // LANES
    assert (tk % tq == 0 or tq % tk == 0) and n_split % 2 == 0 and width % (n_split * LANES) == 0, \
        (tq, tk, width, n_split)
    return pl.pallas_call(
        functools.partial(_flash_t_body, rep=rep, tq=tq, tk=tk, keys_per_var=keys_per_var,
                          n_var=n_var, n_split=n_split),
        grid=(g, s_len // tq),
        in_specs=[pl.BlockSpec((gw, tq), lambda gi, qi: (gi, qi)),
                  pl.BlockSpec((1, n_var * LANES, tq), lambda gi, qi: (gi, 0, qi)),
                  pl.BlockSpec((s_len, HEAD_DIM), lambda gi, qi: (0, gi)),
                  pl.BlockSpec((s_len, LANES), lambda gi, qi: (0, 0)),
                  pl.BlockSpec((1, n_kt, HEAD_DIM, tk), lambda gi, qi: (gi, 0, 0, 0))],
        out_specs=pl.BlockSpec((tq, gw), lambda gi, qi: (qi, gi)),
        out_shape=jax.ShapeDtypeStruct((s_len, g * gw), out_dtype),
        scratch_shapes=[pltpu.VMEM((n_var, 2 * HEAD_DIM, width), BF16),
                        pltpu.VMEM((1, width), F32),
                        pltpu.VMEM((1, width), F32),
                        pltpu.VMEM((HEAD_DIM, width), F32),
                        pltpu.VMEM((2, tk, width), F32)],
        compiler_params=_cparams(("parallel", "arbitrary")),
        name=name,
    )(q_t, bias_t, k, block_onehot, v_t)


def _moba_select_body(q_ref, km_ref, bt_ref, *, tq, n_blk):
    q0 = pl.program_id(1) * tq
    nt = (((1,), (1,)), ((), ()))
    sg = lax.dot_general(km_ref[0], q_ref[...], nt, preferred_element_type=F32)
    j = lax.broadcasted_iota(jnp.int32, sg.shape, 0)
    cur = (q0 + lax.broadcasted_iota(jnp.int32, (1, tq), 1)) // MOBA_BLOCK
    past = j < cur
    score = jnp.where(j < n_blk, jnp.where(past, sg, NEG), BELOW_NEG)
    sel = _topk_mask(score, j.astype(F32), min(MOBA_TOPK, n_blk), 0)
    bt_ref[0] = jnp.where((sel & past) | (j == cur), 0.0, NEG).astype(BF16)


def _moba_select(q, k_mean, n_blk, tq=512):
    s_len = q.shape[0]
    tq = min(tq, s_len)
    n_heads = k_mean.shape[0]
    return pl.pallas_call(
        functools.partial(_moba_select_body, tq=tq, n_blk=n_blk),
        grid=(n_heads, s_len // tq),
        in_specs=[pl.BlockSpec((tq, HEAD_DIM), lambda hi, qi: (qi, hi)),
                  pl.BlockSpec((1, LANES, HEAD_DIM), lambda hi, qi: (hi, 0, 0))],
        out_specs=pl.BlockSpec((1, LANES, tq), lambda hi, qi: (hi, 0, qi)),
        out_shape=jax.ShapeDtypeStruct((n_heads, LANES, s_len), BF16),
        compiler_params=_cparams(("parallel", "parallel")),
        name="moba_select",
    )(q, k_mean)


def _window_body(q_ref, kt_ref, v_ref, oc_ref, os_ref, g_ref, o_ref, *, tq):
    qi = pl.program_id(1)
    q0 = qi * tq
    n_past = WINDOW // tq
    q = jnp.concatenate([q_ref[:, r * HEAD_DIM:(r + 1) * HEAD_DIM] for r in range(NSA_REP)], axis=0)
    t1 = q0 + lax.broadcasted_iota(jnp.int32, (tq, 1), 0)
    t = jnp.concatenate([t1] * NSA_REP, axis=0)
    lane = lax.broadcasted_iota(jnp.int32, (1, tq), 1)
    ones = jnp.ones((tq, HEAD_DIM), BF16)
    scores, tiles = [], []
    for i in range(n_past + 1):
        raw = qi - n_past + i
        idx = jnp.maximum(raw, 0)
        tiles.append(idx)
        s = jnp.dot(q, kt_ref[0, idx], preferred_element_type=F32)
        pos = raw * tq + lane
        if i == 0:
            s = jnp.where(pos > t - WINDOW, s, NEG)
        if i == n_past:
            s = jnp.where(pos <= t, s, NEG)
        else:
            s = s + jnp.where(raw >= 0, 0.0, NEG)
        scores.append(s)
    m = jnp.max(scores[0], axis=-1, keepdims=True)
    for s in scores[1:]:
        m = jnp.maximum(m, jnp.max(s, axis=-1, keepdims=True))
    acc = jnp.zeros((NSA_REP * tq, 2 * HEAD_DIM), F32)
    for i, s in enumerate(scores):
        start = pl.multiple_of(tiles[i] * tq, tq)
        v_aug = jnp.concatenate([v_ref[pl.ds(start, tq), :], ones], axis=1)
        acc = acc + jnp.dot(jnp.exp2(s - m).astype(BF16), v_aug, preferred_element_type=F32)
    o_win = acc[:, :HEAD_DIM] / acc[:, HEAD_DIM:]
    gate = g_ref[0]
    for r in range(NSA_REP):
        sl = slice(r * HEAD_DIM, (r + 1) * HEAD_DIM)
        o = (gate[:, r:r + 1] * oc_ref[:, sl]
             + gate[:, NSA_REP + r:NSA_REP + r + 1] * os_ref[:, sl]
             + gate[:, 2 * NSA_REP + r:2 * NSA_REP + r + 1] * o_win[r * tq:(r + 1) * tq])
        o_ref[:, sl] = o.astype(o_ref.dtype)


def _window_combine(q_r, kw_t, v, v_block0, o_cmp, o_slc, gates):
    s_len = q_r.shape[0]
    g, n_kt, _, tq = kw_t.shape
    assert WINDOW % tq == 0, tq
    gw = NSA_REP * HEAD_DIM
    tile = pl.BlockSpec((tq, gw), lambda gi, qi: (qi, gi))
    return pl.pallas_call(
        functools.partial(_window_body, tq=tq),
        grid=(g, s_len // tq),
        in_specs=[tile,
                  pl.BlockSpec((1, n_kt, HEAD_DIM, tq), lambda gi, qi: (gi, 0, 0, 0)),
                  pl.BlockSpec((s_len, HEAD_DIM), lambda gi, qi: (0, v_block0 + gi)),
                  tile, tile,
                  pl.BlockSpec((1, tq, LANES), lambda gi, qi: (gi, qi, 0))],
        out_specs=tile,
        out_shape=jax.ShapeDtypeStruct((s_len, g * gw), BF16),
        compiler_params=_cparams(("parallel", "parallel")),
        name="nsa_window_combine",
    )(q_r, kw_t, v, o_cmp, o_slc, gates)


def _sgu_body(zu_ref, zv_ref, gain_ref, w_ref, bt_ref, o_ref, *, tm):
    gd = zu_ref.shape[1] // SGU_GROUPS
    row = lax.broadcasted_iota(jnp.int32, (SGU_CHUNK, SGU_CHUNK), 0)
    col = lax.broadcasted_iota(jnp.int32, (SGU_CHUNK, SGU_CHUNK), 1)
    bt = bt_ref[...]
    for g in range(SGU_GROUPS):
        sl = slice(g * gd, (g + 1) * gd)
        v = _gelu(zv_ref[:, sl])
        v = (v * lax.rsqrt(jnp.mean(v * v, axis=-1, keepdims=True) + NORM_EPS) * gain_ref[:, sl]).astype(BF16)
        w = jnp.where(col <= row, w_ref[g], 0.0).astype(BF16)
        for c in range(tm // SGU_CHUNK):
            rs = slice(c * SGU_CHUNK, (c + 1) * SGU_CHUNK)
            mixed = jnp.dot(w, v[rs], preferred_element_type=F32) + bt[:, g:g + 1]
            o_ref[rs, sl] = (_gelu(zu_ref[rs, sl]) * mixed).astype(o_ref.dtype)


def _sgu(z_uv, gain, w_s, b_s, tm=512):
    s_len = z_uv.shape[0]
    width = z_uv.shape[1] // 2
    tm = min(tm, s_len)
    return pl.pallas_call(
        functools.partial(_sgu_body, tm=tm),
        grid=(s_len // tm,),
        in_specs=[pl.BlockSpec((tm, width), lambda i: (i, 0)),
                  pl.BlockSpec((tm, width), lambda i: (i, 1)),
                  pl.BlockSpec((1, width), lambda i: (0, 0)),
                  pl.BlockSpec((SGU_GROUPS, SGU_CHUNK, SGU_CHUNK), lambda i: (0, 0, 0)),
                  pl.BlockSpec((SGU_CHUNK, SGU_GROUPS), lambda i: (0, 0))],
        out_specs=pl.BlockSpec((tm, width), lambda i: (i, 0)),
        out_shape=jax.ShapeDtypeStruct((s_len, width), BF16),
        compiler_params=_cparams(("parallel",)),
        name="sgu",
    )(z_uv, z_uv, gain.reshape(1, width), w_s, b_s.T)


def _merge_body(oa_ref, ob_ref, oc_ref, pa_ref, pb_ref, pc_ref, ga_ref, gb_ref, gc_ref, y_ref):
    y = ga_ref[...] * jnp.dot(oa_ref[...], pa_ref[...], preferred_element_type=F32)
    y = y + gb_ref[...] * jnp.dot(ob_ref[...], pb_ref[...], preferred_element_type=F32)
    y = y + gc_ref[...] * jnp.dot(oc_ref[...], pc_ref[...], preferred_element_type=F32)
    y_ref[...] = y.astype(y_ref.dtype)


def _merge(o_a, o_b, o_c, p_a, p_b, p_c, layer, gm, tm=1024, tn=512):
    s_len = o_a.shape[0]
    d = p_a.shape[2]
    tm, tn = min(tm, s_len), _tile(d, tn)
    nj = d // tn

    def rows(w):
        return pl.BlockSpec((tm, w), lambda i, j: (i, 0))

    def cols(kdim):
        return pl.BlockSpec((None, kdim, tn), lambda i, j: (layer, 0, j))

    def gate(off):
        return pl.BlockSpec((tm, tn), lambda i, j: (i, off * nj + j))

    return pl.pallas_call(
        _merge_body,
        grid=(s_len // tm, nj),
        in_specs=[rows(o_a.shape[1]), rows(o_b.shape[1]), rows(o_c.shape[1]),
                  cols(p_a.shape[1]), cols(p_b.shape[1]), cols(p_c.shape[1]),
                  gate(0), gate(1), gate(2)],
        out_specs=pl.BlockSpec((tm, tn), lambda i, j: (i, j)),
        out_shape=jax.ShapeDtypeStruct((s_len, d), BF16),
        compiler_params=_cparams(("parallel", "parallel")),
        name="gated_merge",
    )(o_a, o_b, o_c, p_a, p_b, p_c, gm, gm, gm)


def _block_onehot(s_len, block):
    key = np.arange(s_len)
    return jnp.asarray(((key // block) % LANES)[:, None] == np.arange(LANES)[None, :], dtype=BF16)


def _tiles_t(v, n_heads, tk):
    s_len = v.shape[0]
    return v.reshape(s_len // tk, tk, n_heads, HEAD_DIM).transpose(2, 0, 3, 1)


def _overlap_t(n_pad, n_slc_pad):
    i = np.arange(n_pad)[None, :]
    j = np.arange(n_slc_pad)[:, None]
    ov = (i * CMP_STRIDE <= j * SLC_LEN + SLC_LEN - 1) & (i * CMP_STRIDE + CMP_LEN - 1 >= j * SLC_LEN)
    return jnp.asarray(ov, dtype=BF16)


def _layer(x, cos, sin, p, big, layer):
    s_len, d_model = x.shape
    scale = HEAD_DIM ** -0.5 * math.log2(math.e)
    w_in = p["w_in"]
    nsa_w = big["proj_a"].shape[1]
    sgu_w = big["proj_b"].shape[1]
    moba_w = big["proj_c"].shape[1]
    n_heads = nsa_w // HEAD_DIM
    n_groups = n_heads // NSA_REP
    kv_w = n_groups * HEAD_DIM
    moba_heads = moba_w // HEAD_DIM
    sizes = (nsa_w, kv_w, kv_w, kv_w, kv_w, kv_w, kv_w, 3 * n_heads, sgu_w, sgu_w,
             moba_w, moba_w, moba_w, d_model, d_model, d_model)
    offs = np.concatenate([[0], np.cumsum(sizes)])

    def seg(a, b):
        return w_in[:, offs[a]:offs[b]].astype(BF16)

    def tile_gain(gain, reps):
        return jnp.tile(gain, reps).reshape(1, reps * HEAD_DIM)

    h = _rmsnorm(x, p["norm_mix"])
    rope_extras = (cos, sin)
    rope_specs = (_spec_rope, _spec_rope)

    q_c, q_r = _matmul(
        h, seg(0, 1), functools.partial(_ep_q, scale=scale),
        [jax.ShapeDtypeStruct((s_len, nsa_w), BF16)] * 2, [_spec_tile, _spec_tile],
        extras=(tile_gain(p["nsa_q_norm"], n_heads),) + rope_extras,
        extra_specs=(_spec_col,) + rope_specs, ts=HEAVY_EPILOGUE_TS, name="proj_nsa_q")
    (kcvc,) = _matmul(h, seg(1, 3), _ep_cast, [jax.ShapeDtypeStruct((s_len, 2 * kv_w), F32)],
                      [_spec_tile], tk=FULL_K, name="proj_nsa_cmp_kv")
    (kskw,) = _matmul(
        h, jnp.concatenate([seg(3, 4), seg(5, 6)], axis=1), functools.partial(_ep_krot, block_mean=False),
        [jax.ShapeDtypeStruct((s_len, 2 * kv_w), BF16)], [_spec_tile],
        extras=(jnp.concatenate([tile_gain(p["nsa_ks_norm"], n_groups),
                                 tile_gain(p["nsa_kw_norm"], n_groups)], axis=1),) + rope_extras,
        extra_specs=(_spec_col,) + rope_specs, ts=HEAVY_EPILOGUE_TS, name="proj_nsa_k")
    (vsvw,) = _matmul(h, jnp.concatenate([seg(4, 5), seg(6, 7)], axis=1), _ep_cast,
                      [jax.ShapeDtypeStruct((s_len, 2 * kv_w), BF16)], [_spec_tile],
                      tk=FULL_K, name="proj_nsa_v")
    n_gate = 3 * n_heads
    w_gate = jnp.pad(seg(7, 8), ((0, 0), (0, LANES - n_gate)))
    b_gate = jnp.pad(p["nsa_gate_b"], (0, LANES - n_gate)).reshape(1, LANES)
    (gates,) = _matmul(h, w_gate, _ep_sigmoid_bias, [jax.ShapeDtypeStruct((s_len, LANES), F32)],
                       [_spec_tile], extras=(b_gate,), extra_specs=(_spec_col,), name="proj_nsa_gates")

    n_chunk = s_len // CMP_STRIDE
    chunks = kcvc.reshape(n_chunk, CMP_STRIDE, 2, n_groups, HEAD_DIM).transpose(2, 3, 0, 1, 4)
    chunks = chunks.reshape(2, n_groups, n_chunk, CMP_STRIDE * HEAD_DIM)
    kc = _compress(chunks[0], p["phi_pe_k"], p["phi_w1_k"], p["phi_w2_k"], p["nsa_kc_norm"], True)
    vc = _compress(chunks[1], p["phi_pe_v"], p["phi_w1_v"], p["phi_w2_v"], p["nsa_kc_norm"], False)

    n_slc = s_len // SLC_LEN
    n_slc_pad = -(-n_slc // LANES) * LANES
    o_cmp, sel_bias_t = _nsa_cmp(q_c, jnp.swapaxes(kc, 1, 2), vc, _overlap_t(n_chunk, n_slc_pad), n_slc)
    tk = min(512, s_len)
    o_slc = _flash_t(q_r.T, sel_bias_t, kskw, _block_onehot(s_len, SLC_LEN),
                     _tiles_t(vsvw[:, :kv_w], n_groups, tk), rep=NSA_REP, tq=min(512, s_len),
                     keys_per_var=LANES * SLC_LEN, out_dtype=F32, name="nsa_selected", n_split=8)
    kw_t = _tiles_t(kskw[:, kv_w:], n_groups, min(256, s_len))
    group_gates = gates[:, :n_gate].reshape(s_len, 3, n_groups, NSA_REP).transpose(2, 0, 1, 3)
    group_gates = jnp.pad(group_gates.reshape(n_groups, s_len, 3 * NSA_REP),
                          ((0, 0), (0, 0), (0, LANES - 3 * NSA_REP)))
    o_a = _window_combine(q_r, kw_t, vsvw, n_groups, o_cmp, o_slc, group_gates)

    (z_uv,) = _matmul(h, seg(8, 10), _ep_cast, [jax.ShapeDtypeStruct((s_len, 2 * sgu_w), F32)],
                      [_spec_tile], tk=FULL_K, name="proj_sgu")
    o_b = _sgu(z_uv, p["sgu_norm"], p["sgu_w"], p["sgu_b"])

    (mq,) = _matmul(
        h, seg(10, 11), functools.partial(_ep_qrot, scale=scale),
        [jax.ShapeDtypeStruct((s_len, moba_w), BF16)], [_spec_tile],
        extras=(tile_gain(p["moba_q_norm"], moba_heads),) + rope_extras,
        extra_specs=(_spec_col,) + rope_specs, ts=HEAVY_EPILOGUE_TS, name="proj_moba_q")
    tm_k = min(1024, s_len)
    mk, mk_mean = _matmul(
        h, seg(11, 12), functools.partial(_ep_krot, block_mean=True),
        [jax.ShapeDtypeStruct((s_len, moba_w), BF16),
         jax.ShapeDtypeStruct((s_len // tm_k, tm_k // MOBA_BLOCK, moba_w), F32)],
        [_spec_tile, _spec_blockmean],
        extras=(tile_gain(p["moba_k_norm"], moba_heads),) + rope_extras,
        extra_specs=(_spec_col,) + rope_specs, tm=tm_k, ts=HEAVY_EPILOGUE_TS, name="proj_moba_k")
    (mv,) = _matmul(h, seg(12, 13), _ep_cast, [jax.ShapeDtypeStruct((s_len, moba_w), BF16)],
                    [_spec_tile], tk=FULL_K, name="proj_moba_v")
    n_blk = s_len // MOBA_BLOCK
    assert n_blk <= LANES, n_blk
    k_mean = mk_mean.reshape(n_blk, moba_heads, HEAD_DIM).transpose(1, 0, 2)
    k_mean = jnp.pad(k_mean, ((0, 0), (0, LANES - n_blk), (0, 0))).astype(BF16)
    o_c = _flash_t(mq.T, _moba_select(mq, k_mean, n_blk), mk, _block_onehot(s_len, MOBA_BLOCK),
                   _tiles_t(mv, moba_heads, min(1024, s_len)), rep=1, tq=min(1024, s_len),
                   keys_per_var=LANES * MOBA_BLOCK, out_dtype=BF16, name="moba", n_split=4)

    (gm,) = _matmul(h, seg(13, 16), _ep_sigmoid, [jax.ShapeDtypeStruct((s_len, 3 * d_model), F32)],
                    [_spec_tile], tk=FULL_K, name="proj_merge_gates")
    y = _merge(o_a, o_b, o_c, big["proj_a"], big["proj_b"], big["proj_c"], layer, gm)
    (x,) = _matmul(y, big["w_out"], _ep_residual, [jax.ShapeDtypeStruct((s_len, d_model), F32)],
                   [_spec_tile], extras=(x,), extra_specs=(_spec_tile,), b_layer=layer, name="out_proj")

    h2 = _rmsnorm(x, p["norm_mlp"])
    (hid,) = _matmul(h2, big["mlp_w1"], _ep_relu2,
                     [jax.ShapeDtypeStruct((s_len, big["mlp_w1"].shape[2]), BF16)], [_spec_tile],
                     b_layer=layer, name="mlp_up")
    (x,) = _matmul(hid, big["mlp_w2"], _ep_residual, [jax.ShapeDtypeStruct((s_len, d_model), F32)],
                   [_spec_tile], extras=(x,), extra_specs=(_spec_tile,), b_layer=layer, name="mlp_down")
    return x


_LAYER_PARAMS = ("norm_mix", "norm_mlp", "w_in", "nsa_gate_b", "nsa_q_norm", "nsa_kc_norm", "nsa_ks_norm",
                 "nsa_kw_norm", "phi_pe_k", "phi_w1_k", "phi_w2_k", "phi_pe_v", "phi_w1_v", "phi_w2_v",
                 "sgu_norm", "sgu_w", "sgu_b", "moba_q_norm", "moba_k_norm", "proj_a", "proj_b", "proj_c",
                 "w_out", "mlp_w1", "mlp_w2")
_BIG_PARAMS = ("proj_a", "proj_b", "proj_c", "w_out", "mlp_w1", "mlp_w2")


def kernel(x, positions, norm_mix, norm_mlp, w_in, nsa_gate_b, nsa_q_norm, nsa_kc_norm, nsa_ks_norm, nsa_kw_norm, phi_pe_k, phi_w1_k, phi_w2_k, phi_pe_v, phi_w1_v, phi_w2_v, sgu_norm, sgu_w, sgu_b, moba_q_norm, moba_k_norm, proj_a, proj_b, proj_c, w_out, mlp_w1, mlp_w2):
    stacked = dict(zip(_LAYER_PARAMS, (norm_mix, norm_mlp, w_in, nsa_gate_b, nsa_q_norm, nsa_kc_norm,
                                       nsa_ks_norm, nsa_kw_norm, phi_pe_k, phi_w1_k, phi_w2_k, phi_pe_v,
                                       phi_w1_v, phi_w2_v, sgu_norm, sgu_w, sgu_b, moba_q_norm, moba_k_norm,
                                       proj_a, proj_b, proj_c, w_out, mlp_w1, mlp_w2)))
    depth = w_in.shape[0]
    big = {k: stacked.pop(k).astype(BF16) for k in _BIG_PARAMS}
    inv = ROPE_THETA ** (-jnp.arange(0, HEAD_DIM, 2, dtype=F32) / HEAD_DIM)
    batch, s_len, d_model = x.shape
    outs = []
    for b in range(batch):
        ang = positions[b].astype(F32)[:, None] * inv
        cos = jnp.concatenate([jnp.cos(ang), jnp.cos(ang)], axis=-1)
        sin = jnp.concatenate([-jnp.sin(ang), jnp.sin(ang)], axis=-1)
        xb = x.reshape(s_len, d_model) if batch == 1 else x[b]
        for l in range(depth):
            xb = _layer(xb, cos, sin, {k: v[l] for k, v in stacked.items()}, big, l)
        outs.append(xb)
    return outs[0].reshape(1, s_len, d_model) if batch == 1 else jnp.stack(outs)
```
